```python
import jax, jax.numpy as jnp
from jax import lax
import numpy as np

D_MODEL = 1024
BATCH = 8
SEQ = 4096
DEPTH = 2

QBLOCK = 128
WINDOW = 128
EPS = 1e-6
ROPE_THETA = 10000.0
MLA_HEADS = D_MODEL // 128
MLA_Q_RANK = D_MODEL // 4
MLA_KV_RANK = D_MODEL // 8
MLA_NOPE = 64
MLA_ROPE = 32
MLA_V = 64
SWA_HEADS = D_MODEL // 128
SWA_KV_HEADS = SWA_HEADS // 4
SWA_HEAD_DIM = 64
FOX_HEADS = D_MODEL // 64
FOX_HEAD_DIM = 64
FOX_WIDTH = FOX_HEADS * FOX_HEAD_DIM
FORGET_BIAS_INIT = 2.0

EVEN_WIDTH = MLA_HEADS * MLA_V + SWA_HEADS * SWA_HEAD_DIM
EVEN_SPLITS = (MLA_Q_RANK, MLA_KV_RANK, MLA_ROPE, SWA_HEADS * SWA_HEAD_DIM,
               SWA_KV_HEADS * SWA_HEAD_DIM, SWA_KV_HEADS * SWA_HEAD_DIM, EVEN_WIDTH)
ODD_SPLITS = (FOX_WIDTH, FOX_WIDTH, FOX_WIDTH, FOX_HEADS, FOX_WIDTH)
N_EVEN = (DEPTH + 1) // 2
N_ODD = DEPTH // 2

kernel_name = "hybrid_mla_swa_fox_gated"


def rms_norm(x, g):
    xf = x.astype(jnp.float32)
    y = xf * lax.rsqrt(jnp.mean(xf * xf, axis=-1, keepdims=True) + EPS)
    return (y * g.astype(jnp.float32)).astype(x.dtype)


def split_cols(z, sizes):
    idx = [int(v) for v in np.cumsum(sizes)[:-1]]
    return jnp.split(z, idx, axis=-1)


def rope_angles(positions, dim):
    inv_freq = 1.0 / (ROPE_THETA ** (jnp.arange(0, dim, 2, dtype=jnp.float32) / dim))
    return positions.astype(jnp.float32)[..., None] * inv_freq


def apply_rope(x, ang):
    cos, sin = jnp.cos(ang), jnp.sin(ang)
    xf = x.astype(jnp.float32)
    x1, x2 = jnp.split(xf, 2, axis=-1)
    return jnp.concatenate([x1 * cos - x2 * sin, x2 * cos + x1 * sin], axis=-1).astype(x.dtype)


def alibi_slopes(n):
    return 2.0 ** (-8.0 * (jnp.arange(n, dtype=jnp.float32) + 1.0) / n)


def causal_block_attention(q, k, v, log_cum=None):
    B, S, H, Dk = q.shape
    Dv = v.shape[-1]
    nb = S // QBLOCK
    scale = Dk ** -0.5
    kpos = jnp.arange(S)
    lc_t = None if log_cum is None else jnp.transpose(log_cum, (0, 2, 1))

    def one_block(i):
        start = i * QBLOCK
        qi = lax.dynamic_slice_in_dim(q, start, QBLOCK, axis=1)
        s = jnp.einsum('bqhd,bkhd->bhqk', qi, k, preferred_element_type=jnp.float32) * scale
        if lc_t is not None:
            ci = lax.dynamic_slice_in_dim(lc_t, start, QBLOCK, axis=2)
            s = s + ci[..., :, None] - lc_t[..., None, :]
        qpos = start + jnp.arange(QBLOCK)
        mask = kpos[None, :] <= qpos[:, None]
        s = jnp.where(mask, s, -jnp.inf)
        p = jax.nn.softmax(s, axis=-1)
        return jnp.einsum('bhqk,bkhd->bqhd', p.astype(v.dtype), v)

    out = lax.map(one_block, jnp.arange(nb))
    return jnp.transpose(out, (1, 0, 2, 3, 4)).reshape(B, S, H, Dv)


def sliding_window_sink_attention(q, k, v, sinks, slopes):
    B, S, H, D = q.shape
    KV = k.shape[2]
    G = H // KV
    W = WINDOW
    nb = S // W
    qb = q.reshape(B, nb, W, KV, G, D)
    pad = ((0, 0), (W, 0), (0, 0), (0, 0))
    kp = jnp.pad(k, pad).reshape(B, nb + 1, W, KV, D)
    vp = jnp.pad(v, pad).reshape(B, nb + 1, W, KV, D)
    kb = jnp.concatenate([kp[:, :-1], kp[:, 1:]], axis=2)
    vb = jnp.concatenate([vp[:, :-1], vp[:, 1:]], axis=2)
    s = jnp.einsum('bnqkgd,bnckd->bnkgqc', qb, kb, preferred_element_type=jnp.float32) * (D ** -0.5)
    a = jnp.arange(W)[:, None]
    c = jnp.arange(2 * W)[None, :]
    dist = (W + a - c).astype(jnp.float32)
    blk = jnp.arange(nb)[:, None, None]
    valid = (dist >= 0) & (dist < W) & (blk * W + c[None] - W >= 0)
    s = s - slopes.reshape(KV, G)[:, :, None, None] * dist
    s = jnp.where(valid[None, :, None, None], s, -jnp.inf)
    sink = sinks.astype(jnp.float32).reshape(KV, G)[:, :, None, None]
    m = jnp.maximum(jnp.max(s, axis=-1, keepdims=True), sink)
    p = jnp.exp(s - m)
    p = p / (jnp.sum(p, axis=-1, keepdims=True) + jnp.exp(sink - m))
    o = jnp.einsum('bnkgqc,bnckd->bnqkgd', p.astype(v.dtype), vb)
    return o.reshape(B, S, H, D)


def mla_swa_layer(x, positions, g_in, w_in, g_q_a, w_q_up, g_kv_a, w_kv_up, sinks, w_out):
    B, S, _ = x.shape
    h = rms_norm(x, g_in)
    z = h @ w_in
    cq, ckv, kpe, q_s, k_s, v_s, gate = split_cols(z, EVEN_SPLITS)
    q = (rms_norm(cq, g_q_a) @ w_q_up).reshape(B, S, MLA_HEADS, MLA_NOPE + MLA_ROPE)
    q_nope, q_pe = q[..., :MLA_NOPE], q[..., MLA_NOPE:]
    kv = (rms_norm(ckv, g_kv_a) @ w_kv_up).reshape(B, S, MLA_HEADS, MLA_NOPE + MLA_V)
    k_nope, v_m = kv[..., :MLA_NOPE], kv[..., MLA_NOPE:]
    ang = rope_angles(positions, MLA_ROPE)
    q_pe = apply_rope(q_pe, ang[:, :, None, :])
    k_pe = apply_rope(kpe, ang)[:, :, None, :]
    qm = jnp.concatenate([q_nope, q_pe], axis=-1)
    km = jnp.concatenate([k_nope, jnp.broadcast_to(k_pe, (B, S, MLA_HEADS, MLA_ROPE))], axis=-1)
    o_mla = causal_block_attention(qm, km, v_m)
    o_swa = sliding_window_sink_attention(
        q_s.reshape(B, S, SWA_HEADS, SWA_HEAD_DIM),
        k_s.reshape(B, S, SWA_KV_HEADS, SWA_HEAD_DIM),
        v_s.reshape(B, S, SWA_KV_HEADS, SWA_HEAD_DIM),
        sinks, alibi_slopes(SWA_HEADS))
    o = jnp.concatenate([o_mla.reshape(B, S, -1), o_swa.reshape(B, S, -1)], axis=-1)
    return x + (o * jax.nn.silu(gate)) @ w_out


def fox_layer(x, g_in, w_in, b_f, w_out):
    B, S, _ = x.shape
    h = rms_norm(x, g_in)
    z = h @ w_in
    q, k, v, f_logit, gate = split_cols(z, ODD_SPLITS)
    log_f = jax.nn.log_sigmoid(f_logit.astype(jnp.float32) + b_f.astype(jnp.float32))
    log_cum = jnp.cumsum(log_f, axis=1)
    o = causal_block_attention(q.reshape(B, S, FOX_HEADS, FOX_HEAD_DIM),
                               k.reshape(B, S, FOX_HEADS, FOX_HEAD_DIM),
                               v.reshape(B, S, FOX_HEADS, FOX_HEAD_DIM), log_cum=log_cum)
    return x + (o.reshape(B, S, FOX_WIDTH) * jax.nn.silu(gate)) @ w_out


def setup_inputs(seed: int = 0) -> dict:
    key = jax.random.key(seed)
    ks = jax.random.split(key, 16)
    f32 = jnp.float32

    def w(k, shape, fan_in):
        return jax.random.normal(k, shape, f32) * (fan_in ** -0.5)

    def gain(k, shape):
        return 1.0 + 0.05 * jax.random.normal(k, shape, f32)

    x = jax.random.normal(ks[0], (BATCH, SEQ, D_MODEL), f32)
    positions = jnp.broadcast_to(jnp.arange(SEQ, dtype=jnp.int32), (BATCH, SEQ))
    return {
        "x": x,
        "positions": positions,
        "e_g_in": gain(ks[1], (N_EVEN, D_MODEL)),
        "e_w_in": w(ks[2], (N_EVEN, D_MODEL, sum(EVEN_SPLITS)), D_MODEL),
        "e_g_q_a": gain(ks[3], (N_EVEN, MLA_Q_RANK)),
        "e_w_q_up": w(ks[4], (N_EVEN, MLA_Q_RANK, MLA_HEADS * (MLA_NOPE + MLA_ROPE)), MLA_Q_RANK),
        "e_g_kv_a": gain(ks[5], (N_EVEN, MLA_KV_RANK)),
        "e_w_kv_up": w(ks[6], (N_EVEN, MLA_KV_RANK, MLA_HEADS * (MLA_NOPE + MLA_V)), MLA_KV_RANK),
        "e_sinks": jax.random.normal(ks[7], (N_EVEN, SWA_HEADS), f32),
        "e_w_out": w(ks[8], (N_EVEN, EVEN_WIDTH, D_MODEL), EVEN_WIDTH),
        "o_g_in": gain(ks[9], (N_ODD, D_MODEL)),
        "o_w_in": w(ks[10], (N_ODD, D_MODEL, sum(ODD_SPLITS)), D_MODEL),
        "o_b_f": FORGET_BIAS_INIT + 0.5 * jax.random.normal(ks[11], (N_ODD, FOX_HEADS), f32),
        "o_w_out": w(ks[12], (N_ODD, FOX_WIDTH, D_MODEL), FOX_WIDTH),
        "g_final": gain(ks[13], (D_MODEL,)),
    }


def reference(x, positions, e_g_in, e_w_in, e_g_q_a, e_w_q_up, e_g_kv_a, e_w_kv_up, e_sinks,
              e_w_out, o_g_in, o_w_in, o_b_f, o_w_out, g_final):
    for layer in range(DEPTH):
        j = layer // 2
        if layer % 2 == 0:
            x = mla_swa_layer(x, positions, e_g_in[j], e_w_in[j], e_g_q_a[j], e_w_q_up[j],
                              e_g_kv_a[j], e_w_kv_up[j], e_sinks[j], e_w_out[j])
        else:
            x = fox_layer(x, o_g_in[j], o_w_in[j], o_b_f[j], o_w_out[j])
    return rms_norm(x, g_final)
```

```python
import functools

import jax
import jax.numpy as jnp
import numpy as np
from jax import lax
from jax.experimental import pallas as pl
from jax.experimental.pallas import tpu as pltpu

F32 = jnp.float32
BF16 = jnp.bfloat16

D_MODEL = 1024
EPS = 1e-6
ROPE_THETA = 10000.0
WINDOW = 128

MLA_HEADS = 8
MLA_Q_RANK = 256
MLA_KV_RANK = 128
MLA_NOPE = 64
MLA_ROPE = 32
MLA_V = 64
SWA_HEADS = 8
SWA_KV_HEADS = 2
SWA_HEAD_DIM = 64
FOX_HEADS = 16
FOX_HEAD_DIM = 64
FOX_WIDTH = FOX_HEADS * FOX_HEAD_DIM

LANES = 128
HALF = 64
V7X_VMEM_LIMIT_BYTES = 56 * 1024 * 1024

PROJ_ROWS = 512
ATTN_TILE = 512
SWA_ROWS = 512
CUM_CHUNK = 256


def _dot(a, b):
    return jnp.dot(a, b, preferred_element_type=F32)


def _dot_nt(a, b):
    return lax.dot_general(a, b, (((1,), (1,)), ((), ())), preferred_element_type=F32)


def _rms(xf, g):
    ms = jnp.mean(xf * xf, axis=-1, keepdims=True)
    return xf * lax.rsqrt(ms + EPS) * g


def _silu(x):
    return x * (1.0 / (1.0 + jnp.exp(-x)))


def _params(n_parallel):
    return pltpu.CompilerParams(
        dimension_semantics=("parallel",) * n_parallel,
        vmem_limit_bytes=V7X_VMEM_LIMIT_BYTES,
    )


def _resident(shape):
    nd = len(shape)
    return pl.BlockSpec(shape, lambda *_: (0,) * nd, pipeline_mode=pl.Buffered(1))


def _proj0_kernel(x_ref, pos_ref, g_in_ref, w_a_ref, w_qs_ref, w_kvs_ref, w_gate_ref,
                  g_q_ref, w_q_ref, w_qsw_ref, g_kv_ref, w_k_ref, w_v_ref, rope_ref,
                  qm_ref, km_ref, vm_ref, qs_ref, kvs_ref, gate_ref):
    h = _rms(x_ref[0], g_in_ref[...]).astype(BF16)

    za = _dot(h, w_a_ref[...])
    cqn = _rms(za[:, :MLA_Q_RANK], g_q_ref[...]).astype(BF16)
    ckvn = _rms(za[:, MLA_Q_RANK:MLA_Q_RANK + MLA_KV_RANK], g_kv_ref[...]).astype(BF16)
    kpe = za[:, 384:512]
    kpe_sw = za[:, 512:640]

    ang = pos_ref[0].astype(F32) * rope_ref[0:1, :]
    cos_t = jnp.cos(ang) * rope_ref[1:2, :]
    sin_t = jnp.sin(ang) * rope_ref[2:3, :]
    scale = (MLA_NOPE + MLA_ROPE) ** -0.5
    q_cos = (cos_t + rope_ref[3:4, :]) * scale
    q_sin = sin_t * scale
    kpe_rot = kpe * cos_t + kpe_sw * sin_t

    for c in range(MLA_HEADS // 2):
        cols = slice(c * 2 * LANES, (c + 1) * 2 * LANES)
        q = _dot(cqn, w_q_ref[:, cols])
        q_sw = _dot(cqn, w_qsw_ref[:, cols])
        kn = _dot(ckvn, w_k_ref[:, cols])
        for hh in range(2):
            lo = hh * LANES
            qm_ref[0, :, c * 2 * LANES + lo:c * 2 * LANES + lo + LANES] = (
                q[:, lo:lo + LANES] * q_cos + q_sw[:, lo:lo + LANES] * q_sin).astype(BF16)
            km_ref[0, :, c * 2 * LANES + lo:c * 2 * LANES + lo + LANES] = (
                kn[:, lo:lo + LANES] + kpe_rot).astype(BF16)
    vm_ref[0] = _dot(ckvn, w_v_ref[...]).astype(BF16)
    qs_ref[0] = _dot(h, w_qs_ref[...]).astype(BF16)
    kvs_ref[0] = _dot(h, w_kvs_ref[...]).astype(BF16)
    gate_ref[0] = _dot(h, w_gate_ref[...]).astype(BF16)


def _proj0(x, pos3, g_in, w_a, w_qs, w_kvs, w_gate, g_q, w_q, w_qsw, g_kv, w_k, w_v, rope):
    B, S, D = x.shape
    tm = PROJ_ROWS
    row = lambda width: pl.BlockSpec((1, tm, width), lambda b, i: (b, i, 0))
    out_widths = (MLA_HEADS * LANES, MLA_HEADS * LANES, MLA_HEADS * MLA_V,
                  SWA_HEADS * SWA_HEAD_DIM, 4 * LANES, D)
    return pl.pallas_call(
        _proj0_kernel,
        grid=(B, S // tm),
        in_specs=[row(D), row(1)] + [_resident(a.shape) for a in
                                     (g_in, w_a, w_qs, w_kvs, w_gate, g_q, w_q, w_qsw, g_kv, w_k, w_v, rope)],
        out_specs=[row(w) for w in out_widths],
        out_shape=[jax.ShapeDtypeStruct((B, S, w), BF16) for w in out_widths],
        compiler_params=_params(2),
        name="proj0",
    )(x, pos3, g_in, w_a, w_qs, w_kvs, w_gate, g_q, w_q, w_qsw, g_kv, w_k, w_v, rope)


def _flash_kernel(*refs, head_lanes, has_bias):
    if has_bias:
        q_ref, k_ref, v_ref, lc_ref, o_ref, m_scr, l_scr, acc_scr = refs
    else:
        q_ref, k_ref, v_ref, o_ref, m_scr, l_scr, acc_scr = refs
        lc_ref = None
    T = ATTN_TILE
    nq = q_ref.shape[1] // T
    lane = lax.broadcasted_iota(jnp.int32, (T, LANES), 1)
    row_id = lax.broadcasted_iota(jnp.int32, (T, T), 0)
    col_id = lax.broadcasted_iota(jnp.int32, (T, T), 1)

    for hh in range(2):
        keep = (lane < HALF) if hh == 0 else (lane >= HALF)

        def q_tile_body(i, carry, hh=hh, keep=keep):
            q0 = pl.multiple_of(i * T, T)
            if head_lanes == LANES:
                q_h = q_ref[0, pl.ds(q0, T), hh * LANES:(hh + 1) * LANES]
            else:
                q_h = jnp.where(keep, q_ref[0, pl.ds(q0, T), :], jnp.zeros((), BF16))
            m_scr[...] = jnp.full((T, LANES), -jnp.inf, F32)
            l_scr[...] = jnp.zeros((T, LANES), F32)
            acc_scr[...] = jnp.zeros((T, LANES), F32)
            if has_bias:
                lc_q0 = lc_ref[0, hh, i][:, 0:1]

            def update(j, masked):
                k0 = pl.multiple_of(j * T, T)
                if head_lanes == LANES:
                    k_t = k_ref[0, pl.ds(k0, T), hh * LANES:(hh + 1) * LANES]
                else:
                    k_t = k_ref[0, pl.ds(k0, T), :]
                s = _dot_nt(q_h, k_t)
                if has_bias:
                    s = s + (lc_q0 - lc_ref[0, hh, j])
                if masked:
                    s = jnp.where(col_id <= row_id, s, -jnp.inf)
                m_prev = m_scr[...]
                m_next = jnp.maximum(m_prev, jnp.max(s, axis=1, keepdims=True))
                alpha = jnp.exp(m_prev - m_next)
                p = jnp.exp(s - jnp.concatenate([m_next] * (T // LANES), axis=1))
                l_scr[...] = alpha * l_scr[...] + jnp.sum(p, axis=1, keepdims=True)
                acc_scr[...] = alpha * acc_scr[...] + _dot(p.astype(BF16), v_ref[0, pl.ds(k0, T), :])
                m_scr[...] = m_next

            def full_body(j, c):
                update(j, masked=False)
                return c

            lax.fori_loop(0, i, full_body, 0)
            update(i, masked=True)

            o_h = acc_scr[...] / l_scr[...]
            if hh == 0:
                o_ref[0, pl.ds(q0, T), :] = o_h.astype(o_ref.dtype)
            else:
                o_ref[0, pl.ds(q0, T), :] = jnp.where(
                    keep, o_h.astype(o_ref.dtype), o_ref[0, pl.ds(q0, T), :])
            return carry

        lax.fori_loop(0, nq, q_tile_body, 0)


def _flash(q, k, v, lc, *, q_col0, k_col0, v_col0, n_pairs, head_lanes):
    B, S, _ = q.shape
    qk_w = 2 * head_lanes
    qk_step = qk_w // LANES
    in_specs = [
        pl.BlockSpec((1, S, qk_w), lambda b, p: (b, 0, q_col0 // qk_step + p)),
        pl.BlockSpec((1, S, qk_w), lambda b, p: (b, 0, k_col0 // qk_step + p)),
        pl.BlockSpec((1, S, LANES), lambda b, p: (b, 0, v_col0 + p)),
    ]
    args = [q, k, v]
    if lc is not None:
        in_specs.append(pl.BlockSpec((1, 2) + lc.shape[2:], lambda b, p: (b, p, 0, 0, 0)))
        args.append(lc)
    return pl.pallas_call(
        functools.partial(_flash_kernel, head_lanes=head_lanes, has_bias=lc is not None),
        grid=(B, n_pairs),
        in_specs=in_specs,
        out_specs=pl.BlockSpec((1, S, LANES), lambda b, p: (b, 0, p)),
        out_shape=jax.ShapeDtypeStruct((B, S, n_pairs * LANES), BF16),
        scratch_shapes=[pltpu.VMEM((ATTN_TILE, LANES), F32)] * 3,
        compiler_params=_params(2),
        name="fox" if lc is not None else "mla",
    )(*args)


def _swa_kernel(slopes_ref, sinks_ref, qs_ref, kvs_ref, o_ref):
    W = WINDOW
    group = SWA_HEADS // SWA_KV_HEADS
    step = pl.program_id(1)
    lane = lax.broadcasted_iota(jnp.int32, (W, LANES), 1)
    lo_half = lane < HALF
    a_id = lax.broadcasted_iota(jnp.int32, (W, 2 * W), 0)
    c_id = lax.broadcasted_iota(jnp.int32, (W, 2 * W), 1)
    zero = jnp.zeros((), BF16)

    for blk in range(SWA_ROWS // W):
        t0 = step * SWA_ROWS + blk * W
        k_start = pl.multiple_of(jnp.maximum(t0 - W, 0), W)
        dist_i = (t0 + a_id) - (k_start + c_id)
        valid = (dist_i >= 0) & (dist_i < W)
        dist = dist_i.astype(F32)
        rows = slice(blk * W, (blk + 1) * W)
        for g in range(SWA_KV_HEADS):
            k_t = kvs_ref[0, pl.ds(k_start, 2 * W), g * LANES:(g + 1) * LANES]
            v_t = kvs_ref[0, pl.ds(k_start, 2 * W), (2 + g) * LANES:(3 + g) * LANES]
            q_parts = []
            for pair in range(group // 2):
                q_pair = qs_ref[0, rows, (g * 2 + pair) * LANES:(g * 2 + pair + 1) * LANES]
                q_parts.append(jnp.where(lo_half, q_pair, zero))
                q_parts.append(jnp.where(lo_half, zero, q_pair))
            s_all = _dot_nt(jnp.concatenate(q_parts, axis=0), k_t)
            p_parts, denoms = [], []
            for hq in range(group):
                head = g * group + hq
                sink = sinks_ref[head]
                s = s_all[hq * W:(hq + 1) * W] - slopes_ref[head] * dist
                s = jnp.where(valid, s, -jnp.inf)
                m = jnp.maximum(jnp.max(s, axis=1, keepdims=True), sink)
                p = jnp.exp(s - m)
                denoms.append(jnp.sum(p, axis=1, keepdims=True) + jnp.exp(sink - m))
                p_parts.append(p.astype(BF16))
            o_all = _dot(jnp.concatenate(p_parts, axis=0), v_t)
            for pair in range(group // 2):
                o_even = o_all[(2 * pair) * W:(2 * pair + 1) * W] / denoms[2 * pair]
                o_odd = o_all[(2 * pair + 1) * W:(2 * pair + 2) * W] / denoms[2 * pair + 1]
                o_ref[0, rows, (g * 2 + pair) * LANES:(g * 2 + pair + 1) * LANES] = jnp.where(
                    lo_half, o_even, o_odd).astype(o_ref.dtype)


def _swa(slopes, sinks, qs, kvs):
    B, S, C = qs.shape
    smem = pl.BlockSpec(memory_space=pltpu.SMEM)
    return pl.pallas_call(
        _swa_kernel,
        grid=(B, S // SWA_ROWS),
        in_specs=[smem, smem,
                  pl.BlockSpec((1, SWA_ROWS, C), lambda b, i: (b, i, 0)),
                  pl.BlockSpec((1, S, kvs.shape[2]), lambda b, i: (b, 0, 0))],
        out_specs=pl.BlockSpec((1, SWA_ROWS, C), lambda b, i: (b, i, 0)),
        out_shape=jax.ShapeDtypeStruct((B, S, C), BF16),
        compiler_params=_params(2),
        name="swa",
    )(slopes, sinks, qs, kvs)


def _mid_kernel(x_ref, om_ref, os_ref, gate_ref, w_out_ref, g_in_ref, w_in_ref, w_f_ref,
                x1_ref, z_ref, f_ref):
    half = D_MODEL // 2
    gate = _silu(gate_ref[0].astype(F32))
    a_m = (om_ref[0].astype(F32) * gate[:, :half]).astype(BF16)
    a_s = (os_ref[0].astype(F32) * gate[:, half:]).astype(BF16)
    x1 = x_ref[0] + (_dot(a_m, w_out_ref[:half, :]) + _dot(a_s, w_out_ref[half:, :]))
    x1_ref[0] = x1
    h = _rms(x1, g_in_ref[...]).astype(BF16)
    n_out = z_ref.shape[2]
    chunk = 1024
    for c in range(n_out // chunk):
        z_ref[0, :, c * chunk:(c + 1) * chunk] = _dot(h, w_in_ref[:, c * chunk:(c + 1) * chunk]).astype(BF16)
    f_ref[0] = _dot(h, w_f_ref[...])


def _mid(x, o_mla, o_swa, gate, w_out, g_in, w_in, w_f):
    B, S, D = x.shape
    tm = PROJ_ROWS
    n_out = w_in.shape[1]
    row = lambda width: pl.BlockSpec((1, tm, width), lambda b, i: (b, i, 0))
    return pl.pallas_call(
        _mid_kernel,
        grid=(B, S // tm),
        in_specs=[row(D), row(D // 2), row(D // 2), row(D)] + [_resident(a.shape) for a in (w_out, g_in, w_in, w_f)],
        out_specs=[row(D), row(n_out), row(LANES)],
        out_shape=[jax.ShapeDtypeStruct((B, S, D), F32),
                   jax.ShapeDtypeStruct((B, S, n_out), BF16),
                   jax.ShapeDtypeStruct((B, S, LANES), F32)],
        compiler_params=_params(2),
        name="mid",
    )(x, o_mla, o_swa, gate, w_out, g_in, w_in, w_f)


def _split3(x):
    hi = x.astype(BF16)
    r1 = x - hi.astype(F32)
    mid = r1.astype(BF16)
    lo = (r1 - mid.astype(F32)).astype(BF16)
    return hi, mid, lo


def _lcum_kernel(f_ref, b_ref, lc_ref):
    S = f_ref.shape[1]
    log_f = jax.nn.log_sigmoid(f_ref[0] + b_ref[...])
    log_f_t = log_f.T[:FOX_HEADS]
    n = CUM_CHUNK
    tri = (lax.broadcasted_iota(jnp.int32, (n, n), 0) <= lax.broadcasted_iota(jnp.int32, (n, n), 1)).astype(BF16)
    carry = jnp.zeros((FOX_HEADS, 1), F32)
    for c in range(S // n):
        hi, mid, lo = _split3(log_f_t[:, c * n:(c + 1) * n])
        cs = (_dot(hi, tri) + _dot(mid, tri)) + _dot(lo, tri) + carry
        lc_ref[0, :, c * n:(c + 1) * n] = cs
        carry = cs[:, n - 1:n]


def _lcum(f, b_pad):
    B, S, _ = f.shape
    return pl.pallas_call(
        _lcum_kernel,
        grid=(B,),
        in_specs=[pl.BlockSpec((1, S, LANES), lambda b: (b, 0, 0)), _resident(b_pad.shape)],
        out_specs=pl.BlockSpec((1, FOX_HEADS, S), lambda b: (b, 0, 0)),
        out_shape=jax.ShapeDtypeStruct((B, FOX_HEADS, S), F32),
        compiler_params=_params(1),
        name="lcum",
    )(f, b_pad)


def _final_kernel(x1_ref, o_ref, gate_ref, w_out_ref, g_ref, y_ref):
    a = (o_ref[0].astype(F32) * _silu(gate_ref[0].astype(F32))).astype(BF16)
    x2 = x1_ref[0] + _dot(a, w_out_ref[...])
    y_ref[0] = _rms(x2, g_ref[...])


def _final(x1, o_fox, z1, w_out, g_final):
    B, S, D = x1.shape
    tm = PROJ_ROWS
    gate_col = 3 * FOX_WIDTH // D
    return pl.pallas_call(
        _final_kernel,
        grid=(B, S // tm),
        in_specs=[pl.BlockSpec((1, tm, D), lambda b, i: (b, i, 0)),
                  pl.BlockSpec((1, tm, D), lambda b, i: (b, i, 0)),
                  pl.BlockSpec((1, tm, D), lambda b, i: (b, i, gate_col)),
                  _resident(w_out.shape), _resident(g_final.shape)],
        out_specs=pl.BlockSpec((1, tm, D), lambda b, i: (b, i, 0)),
        out_shape=jax.ShapeDtypeStruct((B, S, D), F32),
        compiler_params=_params(2),
        name="final",
    )(x1, o_fox, z1, w_out, g_final)


def _pad_cols(w, left, total):
    return jnp.pad(w, ((0, 0), (left, total - left - w.shape[1])))


def _layer0_weights(w_in, w_q_up, w_kv_up):
    sizes = (MLA_Q_RANK, MLA_KV_RANK, MLA_ROPE, SWA_HEADS * SWA_HEAD_DIM,
             SWA_KV_HEADS * SWA_HEAD_DIM, SWA_KV_HEADS * SWA_HEAD_DIM, D_MODEL)
    w_cq, w_ckv, w_kpe, w_qs, w_ks, w_vs, w_gate = jnp.split(w_in, np.cumsum(sizes)[:-1].tolist(), axis=1)
    half = MLA_ROPE // 2
    w_kpe_sw = jnp.concatenate([w_kpe[:, half:], w_kpe[:, :half]], axis=1)
    w_a = jnp.concatenate([w_cq, w_ckv, _pad_cols(w_kpe, MLA_NOPE, LANES), _pad_cols(w_kpe_sw, MLA_NOPE, LANES)], axis=1)
    d = SWA_HEAD_DIM
    w_kvs = jnp.concatenate([w_ks[:, :d], w_ks[:, :d], w_ks[:, d:], w_ks[:, d:],
                             w_vs[:, :d], w_vs[:, :d], w_vs[:, d:], w_vs[:, d:]], axis=1)
    qk = MLA_NOPE + MLA_ROPE
    wq3 = w_q_up.reshape(MLA_Q_RANK, MLA_HEADS, qk)
    w_q = jnp.pad(wq3, ((0, 0), (0, 0), (0, LANES - qk))).reshape(MLA_Q_RANK, MLA_HEADS * LANES)
    wq_sw = jnp.concatenate([wq3[:, :, MLA_NOPE + half:], wq3[:, :, MLA_NOPE:MLA_NOPE + half]], axis=2)
    w_qsw = jnp.pad(wq_sw, ((0, 0), (0, 0), (MLA_NOPE, LANES - qk))).reshape(MLA_Q_RANK, MLA_HEADS * LANES)
    wkv3 = w_kv_up.reshape(MLA_KV_RANK, MLA_HEADS, MLA_NOPE + MLA_V)
    w_k = jnp.pad(wkv3[:, :, :MLA_NOPE], ((0, 0), (0, 0), (0, LANES - MLA_NOPE))).reshape(MLA_KV_RANK, MLA_HEADS * LANES)
    w_v = wkv3[:, :, MLA_NOPE:].reshape(MLA_KV_RANK, MLA_HEADS * MLA_V)
    swa_scale = SWA_HEAD_DIM ** -0.5
    ws = [w_a, w_qs * swa_scale, w_kvs, w_gate, w_q, w_qsw, w_k, w_v]
    return [w.astype(BF16) for w in ws]


def _rope_table():
    half = MLA_ROPE // 2
    inv_freq = 1.0 / (ROPE_THETA ** (jnp.arange(0, MLA_ROPE, 2, dtype=F32) / MLA_ROPE))
    t = jnp.zeros((8, LANES), F32)
    t = t.at[0, MLA_NOPE:MLA_NOPE + half].set(inv_freq).at[0, MLA_NOPE + half:MLA_NOPE + MLA_ROPE].set(inv_freq)
    t = t.at[1, MLA_NOPE:MLA_NOPE + MLA_ROPE].set(1.0)
    t = t.at[2, MLA_NOPE:MLA_NOPE + half].set(-1.0).at[2, MLA_NOPE + half:MLA_NOPE + MLA_ROPE].set(1.0)
    t = t.at[3, :MLA_NOPE].set(1.0)
    return t


def kernel(x, positions, e_g_in, e_w_in, e_g_q_a, e_w_q_up, e_g_kv_a, e_w_kv_up, e_sinks,
           e_w_out, o_g_in, o_w_in, o_b_f, o_w_out, g_final):
    B, S, D = x.shape
    assert D == D_MODEL and S % PROJ_ROWS == 0 and S % ATTN_TILE == 0 and S % SWA_ROWS == 0
    row = lambda g: g.reshape(1, -1).astype(F32)

    w0 = _layer0_weights(e_w_in[0], e_w_q_up[0], e_w_kv_up[0])
    w_a, w_qs, w_kvs, w_gate, w_q, w_qsw, w_k, w_v = w0
    qm, km, vm, qs, kvs, gate0 = _proj0(
        x, positions.reshape(B, S, 1), row(e_g_in[0]), w_a, w_qs, w_kvs, w_gate,
        row(e_g_q_a[0]), w_q, w_qsw, row(e_g_kv_a[0]), w_k, w_v, _rope_table())
    o_mla = _flash(qm, km, vm, None, q_col0=0, k_col0=0, v_col0=0,
                   n_pairs=MLA_HEADS // 2, head_lanes=LANES)
    slopes = 2.0 ** (-8.0 * (jnp.arange(SWA_HEADS, dtype=F32) + 1.0) / SWA_HEADS)
    o_swa = _swa(slopes, e_sinks[0].astype(F32), qs, kvs)

    sizes = (FOX_WIDTH, FOX_WIDTH, FOX_WIDTH, FOX_HEADS, FOX_WIDTH)
    w1_q, w1_k, w1_v, w1_f, w1_gate = jnp.split(o_w_in[0], np.cumsum(sizes)[:-1].tolist(), axis=1)
    fox_scale = FOX_HEAD_DIM ** -0.5
    w1 = jnp.concatenate([w1_q * fox_scale, w1_k, w1_v, w1_gate], axis=1).astype(BF16)
    w1_f_pad = _pad_cols(w1_f, 0, LANES).astype(BF16)
    x1, z1, f_logit = _mid(x, o_mla, o_swa, gate0, e_w_out[0].astype(BF16), row(o_g_in[0]), w1, w1_f_pad)

    lc = _lcum(f_logit, _pad_cols(row(o_b_f[0]), 0, LANES))
    lc4 = lc.reshape(B, FOX_HEADS, S // ATTN_TILE, 1, ATTN_TILE)
    n_blk = FOX_WIDTH // LANES
    o_fox = _flash(z1, z1, z1, lc4, q_col0=0, k_col0=n_blk, v_col0=2 * n_blk,
                   n_pairs=FOX_HEADS // 2, head_lanes=HALF)
    return _final(x1, o_fox, z1, o_w_out[0].astype(BF16), row(g_final))
```

```python
import functools

import jax
import jax.numpy as jnp
import numpy as np
from jax import lax
from jax.experimental import pallas as pl
from jax.experimental.pallas import tpu as pltpu

F32 = jnp.float32
BF16 = jnp.bfloat16

D_MODEL = 1024
EPS = 1e-6
ROPE_THETA = 10000.0
WINDOW = 128

MLA_HEADS = 8
MLA_Q_RANK = 256
MLA_KV_RANK = 128
MLA_NOPE = 64
MLA_ROPE = 32
MLA_V = 64
SWA_HEADS = 8
SWA_KV_HEADS = 2
SWA_HEAD_DIM = 64
FOX_HEADS = 16
FOX_HEAD_DIM = 64
FOX_WIDTH = FOX_HEADS * FOX_HEAD_DIM

LANES = 128
HALF = 64
V7X_VMEM_LIMIT_BYTES = 56 * 1024 * 1024

PROJ_ROWS = 512
ATTN_TILE = 512
SWA_ROWS = 512
CUM_CHUNK = 256
LOG2E = 1.4426950408889634


def _dot(a, b):
    return jnp.dot(a, b, preferred_element_type=F32)


def _dot_nt(a, b):
    return lax.dot_general(a, b, (((1,), (1,)), ((), ())), preferred_element_type=F32)


def _rms(xf, g):
    ms = jnp.mean(xf * xf, axis=-1, keepdims=True)
    return xf * lax.rsqrt(ms + EPS) * g


def _silu(x):
    return x * (1.0 / (1.0 + jnp.exp(-x)))


def _params(n_parallel):
    return pltpu.CompilerParams(
        dimension_semantics=("parallel",) * n_parallel,
        vmem_limit_bytes=V7X_VMEM_LIMIT_BYTES,
    )


def _resident(shape):
    nd = len(shape)
    return pl.BlockSpec(shape, lambda *_: (0,) * nd, pipeline_mode=pl.Buffered(1))


def _proj0_kernel(x_ref, pos_ref, g_in_ref, w_a_ref, w_qs_ref, w_kvs_ref, w_gate_ref,
                  g_q_ref, w_q_ref, w_qsw_ref, g_kv_ref, w_k_ref, w_v_ref, rope_ref,
                  qm_ref, km_ref, vm_ref, qs_ref, kvs_ref, gate_ref):
    h = _rms(x_ref[0], g_in_ref[...]).astype(BF16)

    za = _dot(h, w_a_ref[...])
    cqn = _rms(za[:, :MLA_Q_RANK], g_q_ref[...]).astype(BF16)
    ckvn = _rms(za[:, MLA_Q_RANK:MLA_Q_RANK + MLA_KV_RANK], g_kv_ref[...]).astype(BF16)
    kpe = za[:, 384:512]
    kpe_sw = za[:, 512:640]

    ang = pos_ref[0].astype(F32) * rope_ref[0:1, :]
    cos_t = jnp.cos(ang) * rope_ref[1:2, :]
    sin_t = jnp.sin(ang) * rope_ref[2:3, :]
    scale = (MLA_NOPE + MLA_ROPE) ** -0.5 * LOG2E
    q_cos = (cos_t + rope_ref[3:4, :]) * scale
    q_sin = sin_t * scale
    kpe_rot = kpe * cos_t + kpe_sw * sin_t

    for c in range(MLA_HEADS // 2):
        cols = slice(c * 2 * LANES, (c + 1) * 2 * LANES)
        q = _dot(cqn, w_q_ref[:, cols])
        q_sw = _dot(cqn, w_qsw_ref[:, cols])
        kn = _dot(ckvn, w_k_ref[:, cols])
        for hh in range(2):
            lo = hh * LANES
            qm_ref[0, :, c * 2 * LANES + lo:c * 2 * LANES + lo + LANES] = (
                q[:, lo:lo + LANES] * q_cos + q_sw[:, lo:lo + LANES] * q_sin).astype(BF16)
            km_ref[0, :, c * 2 * LANES + lo:c * 2 * LANES + lo + LANES] = (
                kn[:, lo:lo + LANES] + kpe_rot).astype(BF16)
    vm_ref[0] = _dot(ckvn, w_v_ref[...]).astype(BF16)
    qs_ref[0] = _dot(h, w_qs_ref[...]).astype(BF16)
    kvs_ref[0] = _dot(h, w_kvs_ref[...]).astype(BF16)
    gate_ref[0] = _dot(h, w_gate_ref[...]).astype(BF16)


def _proj0(x, pos3, g_in, w_a, w_qs, w_kvs, w_gate, g_q, w_q, w_qsw, g_kv, w_k, w_v, rope):
    B, S, D = x.shape
    tm = PROJ_ROWS
    row = lambda width: pl.BlockSpec((1, tm, width), lambda b, i: (b, i, 0))
    out_widths = (MLA_HEADS * LANES, MLA_HEADS * LANES, MLA_HEADS * MLA_V,
                  SWA_HEADS * SWA_HEAD_DIM, 4 * LANES, D)
    return pl.pallas_call(
        _proj0_kernel,
        grid=(B, S // tm),
        in_specs=[row(D), row(1)] + [_resident(a.shape) for a in
                                     (g_in, w_a, w_qs, w_kvs, w_gate, g_q, w_q, w_qsw, g_kv, w_k, w_v, rope)],
        out_specs=[row(w) for w in out_widths],
        out_shape=[jax.ShapeDtypeStruct((B, S, w), BF16) for w in out_widths],
        compiler_params=_params(2),
        name="proj0",
    )(x, pos3, g_in, w_a, w_qs, w_kvs, w_gate, g_q, w_q, w_qsw, g_kv, w_k, w_v, rope)


def _flash_kernel(*refs, head_lanes, has_bias):
    ti_ref, tj_ref, q_ref, k_ref, v_ref = refs[:5]
    lc_ref = refs[5] if has_bias else None
    o_ref = refs[5 + has_bias]
    scr = refs[6 + has_bias:]
    p_scr, alpha_scr = scr[0:2], scr[2:4]
    m_scr, l_scr, acc_scr, mask_scr = scr[4:]
    T = ATTN_TILE
    nq = q_ref.shape[1] // T
    n_steps = ti_ref.shape[0]
    n_chunks = T // LANES
    lane = lax.broadcasted_iota(jnp.int32, (T, LANES), 1)
    lo_half = lane < HALF
    zero = jnp.zeros((), BF16)

    def rows(i):
        return pl.ds(pl.multiple_of(i * T, T), T)

    def softmax(t, slot, maybe_diag):
        i, j = ti_ref[t], tj_ref[t]
        r = rows(i)
        if head_lanes == HALF:
            q_pair = q_ref[0, rows(i), :]
            q_heads = [jnp.where(lo_half, q_pair, zero), jnp.where(lo_half, zero, q_pair)]
        if maybe_diag:
            mask_add = mask_scr[jnp.where(i == j, 1, 0)]
        for hh in range(2):
            if head_lanes == LANES:
                q_h = q_ref[0, rows(i), hh * LANES:(hh + 1) * LANES]
                k_t = k_ref[0, rows(j), hh * LANES:(hh + 1) * LANES]
            else:
                q_h = q_heads[hh]
                k_t = k_ref[0, rows(j), :]
            s = _dot_nt(q_h, k_t)
            if has_bias:
                s = s + (lc_ref[0, hh, i][:, 0:1] - lc_ref[0, hh, j])
            if maybe_diag:
                s = s + mask_add
            m_prev = m_scr[hh, r, :]
            m_next = jnp.maximum(m_prev, jnp.max(s, axis=1, keepdims=True))
            alpha = jnp.exp2(m_prev - m_next)
            alpha_scr[slot][hh] = alpha
            p = jnp.exp2(s - jnp.concatenate([m_next] * n_chunks, axis=1))
            p_scr[slot][hh] = p.astype(BF16)
            p_lanes = p[:, :LANES]
            for c in range(1, n_chunks):
                p_lanes = p_lanes + p[:, c * LANES:(c + 1) * LANES]
            l_scr[hh, r, :] = alpha * l_scr[hh, r, :] + p_lanes
            m_scr[hh, r, :] = m_next

    def accumulate(t, slot):
        r = rows(ti_ref[t])
        v_t = v_ref[0, rows(tj_ref[t]), :]
        for hh in range(2):
            acc_scr[hh, r, :] = alpha_scr[slot][hh] * acc_scr[hh, r, :] + _dot(p_scr[slot][hh], v_t)

    m_scr[...] = jnp.full(m_scr.shape, -jnp.inf, F32)
    l_scr[...] = jnp.zeros(l_scr.shape, F32)
    acc_scr[...] = jnp.zeros(acc_scr.shape, F32)
    row_id = lax.broadcasted_iota(jnp.int32, (T, T), 0)
    col_id = lax.broadcasted_iota(jnp.int32, (T, T), 1)
    mask_scr[0] = jnp.zeros((T, T), F32)
    mask_scr[1] = jnp.where(col_id <= row_id, 0.0, -jnp.inf).astype(F32)

    softmax(0, 0, True)

    def body(u, carry):
        t = 2 * u + 1
        softmax(t, 1, False)
        accumulate(t - 1, 0)
        softmax(t + 1, 0, True)
        accumulate(t, 1)
        return carry

    lax.fori_loop(0, n_steps // 2 - 1, body, 0)
    softmax(n_steps - 1, 1, False)
    accumulate(n_steps - 2, 0)
    accumulate(n_steps - 1, 1)

    def normalize(i, carry):
        r = rows(i)
        o_heads = [acc_scr[hh, r, :] / jnp.sum(l_scr[hh, r, :], axis=1, keepdims=True) for hh in range(2)]
        o_ref[0, r, :] = jnp.where(lo_half, o_heads[0], o_heads[1]).astype(o_ref.dtype)
        return carry

    lax.fori_loop(0, nq, normalize, 0)


def _tile_schedule(nq):
    diag = [(i, i) for i in range(nq)]
    full = [(i, j) for i in range(nq) for j in range(i)]
    n_even = (len(diag) + len(full)) // 2
    assert (len(diag) + len(full)) % 2 == 0 and n_even >= len(diag) and n_even >= 2
    even = diag + full[:n_even - len(diag)]
    odd = full[n_even - len(diag):]
    return np.array([t for pair in zip(even, odd) for t in pair], np.int32)


def _flash(q, k, v, lc, *, q_col0, k_col0, v_col0, n_pairs, head_lanes):
    B, S, _ = q.shape
    T = ATTN_TILE
    qk_w = 2 * head_lanes
    qk_step = qk_w // LANES
    steps = _tile_schedule(S // T)
    smem = pl.BlockSpec(memory_space=pltpu.SMEM)
    in_specs = [
        smem, smem,
        pl.BlockSpec((1, S, qk_w), lambda b, p: (b, 0, q_col0 // qk_step + p)),
        pl.BlockSpec((1, S, qk_w), lambda b, p: (b, 0, k_col0 // qk_step + p)),
        pl.BlockSpec((1, S, LANES), lambda b, p: (b, 0, v_col0 + p)),
    ]
    args = [jnp.asarray(steps[:, 0]), jnp.asarray(steps[:, 1]), q, k, v]
    if lc is not None:
        in_specs.append(pl.BlockSpec((1, 2) + lc.shape[2:], lambda b, p: (b, p, 0, 0, 0)))
        args.append(lc)
    per_slot = lambda shape, dtype: [pltpu.VMEM(shape, dtype)] * 2
    scratch = (per_slot((2, T, T), BF16) + per_slot((2, T, LANES), F32)
               + [pltpu.VMEM((2, S, LANES), F32)] * 3
               + [pltpu.VMEM((2, T, T), F32)])
    return pl.pallas_call(
        functools.partial(_flash_kernel, head_lanes=head_lanes, has_bias=lc is not None),
        grid=(B, n_pairs),
        in_specs=in_specs,
        out_specs=pl.BlockSpec((1, S, LANES), lambda b, p: (b, 0, p)),
        out_shape=jax.ShapeDtypeStruct((B, S, n_pairs * LANES), BF16),
        scratch_shapes=scratch,
        compiler_params=_params(2),
        name="fox" if lc is not None else "mla",
    )(*args)


def _swa_kernel(slopes_ref, sinks_ref, qs_ref, kvs_ref, o_ref):
    W = WINDOW
    group = SWA_HEADS // SWA_KV_HEADS
    step = pl.program_id(1)
    lane = lax.broadcasted_iota(jnp.int32, (W, LANES), 1)
    lo_half = lane < HALF
    a_id = lax.broadcasted_iota(jnp.int32, (W, 2 * W), 0)
    c_id = lax.broadcasted_iota(jnp.int32, (W, 2 * W), 1)
    zero = jnp.zeros((), BF16)

    for blk in range(SWA_ROWS // W):
        t0 = step * SWA_ROWS + blk * W
        k_start = pl.multiple_of(jnp.maximum(t0 - W, 0), W)
        dist_i = (t0 + a_id) - (k_start + c_id)
        valid = (dist_i >= 0) & (dist_i < W)
        dist = dist_i.astype(F32)
        rows = slice(blk * W, (blk + 1) * W)
        for g in range(SWA_KV_HEADS):
            k_t = kvs_ref[0, pl.ds(k_start, 2 * W), g * LANES:(g + 1) * LANES]
            v_t = kvs_ref[0, pl.ds(k_start, 2 * W), (2 + g) * LANES:(3 + g) * LANES]
            q_parts = []
            for pair in range(group // 2):
                q_pair = qs_ref[0, rows, (g * 2 + pair) * LANES:(g * 2 + pair + 1) * LANES]
                q_parts.append(jnp.where(lo_half, q_pair, zero))
                q_parts.append(jnp.where(lo_half, zero, q_pair))
            s_all = _dot_nt(jnp.concatenate(q_parts, axis=0), k_t)
            p_parts, denoms = [], []
            for hq in range(group):
                head = g * group + hq
                sink = sinks_ref[head]
                s = s_all[hq * W:(hq + 1) * W] - slopes_ref[head] * dist
                s = jnp.where(valid, s, -jnp.inf)
                m = jnp.maximum(jnp.max(s, axis=1, keepdims=True), sink)
                p = jnp.exp(s - m)
                denoms.append(jnp.sum(p, axis=1, keepdims=True) + jnp.exp(sink - m))
                p_parts.append(p.astype(BF16))
            o_all = _dot(jnp.concatenate(p_parts, axis=0), v_t)
            for pair in range(group // 2):
                o_even = o_all[(2 * pair) * W:(2 * pair + 1) * W] / denoms[2 * pair]
                o_odd = o_all[(2 * pair + 1) * W:(2 * pair + 2) * W] / denoms[2 * pair + 1]
                o_ref[0, rows, (g * 2 + pair) * LANES:(g * 2 + pair + 1) * LANES] = jnp.where(
                    lo_half, o_even, o_odd).astype(o_ref.dtype)


def _swa(slopes, sinks, qs, kvs):
    B, S, C = qs.shape
    smem = pl.BlockSpec(memory_space=pltpu.SMEM)
    return pl.pallas_call(
        _swa_kernel,
        grid=(B, S // SWA_ROWS),
        in_specs=[smem, smem,
                  pl.BlockSpec((1, SWA_ROWS, C), lambda b, i: (b, i, 0)),
                  pl.BlockSpec((1, S, kvs.shape[2]), lambda b, i: (b, 0, 0))],
        out_specs=pl.BlockSpec((1, SWA_ROWS, C), lambda b, i: (b, i, 0)),
        out_shape=jax.ShapeDtypeStruct((B, S, C), BF16),
        compiler_params=_params(2),
        name="swa",
    )(slopes, sinks, qs, kvs)


def _mid_kernel(x_ref, om_ref, os_ref, gate_ref, w_out_ref, g_in_ref, w_in_ref, w_f_ref,
                x1_ref, z_ref, f_ref):
    half = D_MODEL // 2
    gate = _silu(gate_ref[0].astype(F32))
    a_m = (om_ref[0].astype(F32) * gate[:, :half]).astype(BF16)
    a_s = (os_ref[0].astype(F32) * gate[:, half:]).astype(BF16)
    x1 = x_ref[0] + (_dot(a_m, w_out_ref[:half, :]) + _dot(a_s, w_out_ref[half:, :]))
    x1_ref[0] = x1
    h = _rms(x1, g_in_ref[...]).astype(BF16)
    n_out = z_ref.shape[2]
    chunk = FOX_WIDTH
    for c in range(n_out // chunk):
        z = _dot(h, w_in_ref[:, c * chunk:(c + 1) * chunk])
        if c == 0:
            z = z * LOG2E
        z_ref[0, :, c * chunk:(c + 1) * chunk] = z.astype(BF16)
    f_ref[0] = _dot(h, w_f_ref[...])


def _mid(x, o_mla, o_swa, gate, w_out, g_in, w_in, w_f):
    B, S, D = x.shape
    tm = PROJ_ROWS
    n_out = w_in.shape[1]
    row = lambda width: pl.BlockSpec((1, tm, width), lambda b, i: (b, i, 0))
    return pl.pallas_call(
        _mid_kernel,
        grid=(B, S // tm),
        in_specs=[row(D), row(D // 2), row(D // 2), row(D)] + [_resident(a.shape) for a in (w_out, g_in, w_in, w_f)],
        out_specs=[row(D), row(n_out), row(LANES)],
        out_shape=[jax.ShapeDtypeStruct((B, S, D), F32),
                   jax.ShapeDtypeStruct((B, S, n_out), BF16),
                   jax.ShapeDtypeStruct((B, S, LANES), F32)],
        compiler_params=_params(2),
        name="mid",
    )(x, o_mla, o_swa, gate, w_out, g_in, w_in, w_f)


def _split3(x):
    hi = x.astype(BF16)
    r1 = x - hi.astype(F32)
    mid = r1.astype(BF16)
    lo = (r1 - mid.astype(F32)).astype(BF16)
    return hi, mid, lo


def _lcum_kernel(f_ref, b_ref, lc_ref):
    S = f_ref.shape[1]
    log_f = jax.nn.log_sigmoid(f_ref[0] + b_ref[...])
    log_f_t = log_f.T[:FOX_HEADS]
    n = CUM_CHUNK
    tri = (lax.broadcasted_iota(jnp.int32, (n, n), 0) <= lax.broadcasted_iota(jnp.int32, (n, n), 1)).astype(BF16)
    carry = jnp.zeros((FOX_HEADS, 1), F32)
    for c in range(S // n):
        hi, mid, lo = _split3(log_f_t[:, c * n:(c + 1) * n])
        cs = (_dot(hi, tri) + _dot(mid, tri)) + _dot(lo, tri) + carry
        lc_ref[0, :, c * n:(c + 1) * n] = cs * LOG2E
        carry = cs[:, n - 1:n]


def _lcum(f, b_pad):
    B, S, _ = f.shape
    return pl.pallas_call(
        _lcum_kernel,
        grid=(B,),
        in_specs=[pl.BlockSpec((1, S, LANES), lambda b: (b, 0, 0)), _resident(b_pad.shape)],
        out_specs=pl.BlockSpec((1, FOX_HEADS, S), lambda b: (b, 0, 0)),
        out_shape=jax.ShapeDtypeStruct((B, FOX_HEADS, S), F32),
        compiler_params=_params(1),
        name="lcum",
    )(f, b_pad)


def _final_kernel(x1_ref, o_ref, gate_ref, w_out_ref, g_ref, y_ref):
    a = (o_ref[0].astype(F32) * _silu(gate_ref[0].astype(F32))).astype(BF16)
    x2 = x1_ref[0] + _dot(a, w_out_ref[...])
    y_ref[0] = _rms(x2, g_ref[...])


def _final(x1, o_fox, z1, w_out, g_final):
    B, S, D = x1.shape
    tm = PROJ_ROWS
    gate_col = 3 * FOX_WIDTH // D
    return pl.pallas_call(
        _final_kernel,
        grid=(B, S // tm),
        in_specs=[pl.BlockSpec((1, tm, D), lambda b, i: (b, i, 0)),
                  pl.BlockSpec((1, tm, D), lambda b, i: (b, i, 0)),
                  pl.BlockSpec((1, tm, D), lambda b, i: (b, i, gate_col)),
                  _resident(w_out.shape), _resident(g_final.shape)],
        out_specs=pl.BlockSpec((1, tm, D), lambda b, i: (b, i, 0)),
        out_shape=jax.ShapeDtypeStruct((B, S, D), F32),
        compiler_params=_params(2),
        name="final",
    )(x1, o_fox, z1, w_out, g_final)


def _pad_cols(w, left, total):
    return jnp.pad(w, ((0, 0), (left, total - left - w.shape[1])))


def _layer0_weights(w_in, w_q_up, w_kv_up):
    sizes = (MLA_Q_RANK, MLA_KV_RANK, MLA_ROPE, SWA_HEADS * SWA_HEAD_DIM,
             SWA_KV_HEADS * SWA_HEAD_DIM, SWA_KV_HEADS * SWA_HEAD_DIM, D_MODEL)
    w_cq, w_ckv, w_kpe, w_qs, w_ks, w_vs, w_gate = jnp.split(w_in, np.cumsum(sizes)[:-1].tolist(), axis=1)
    half = MLA_ROPE // 2
    w_kpe_sw = jnp.concatenate([w_kpe[:, half:], w_kpe[:, :half]], axis=1)
    w_a = jnp.concatenate([w_cq, w_ckv, _pad_cols(w_kpe, MLA_NOPE, LANES), _pad_cols(w_kpe_sw, MLA_NOPE, LANES)], axis=1)
    d = SWA_HEAD_DIM
    w_kvs = jnp.concatenate([w_ks[:, :d], w_ks[:, :d], w_ks[:, d:], w_ks[:, d:],
                             w_vs[:, :d], w_vs[:, :d], w_vs[:, d:], w_vs[:, d:]], axis=1)
    qk = MLA_NOPE + MLA_ROPE
    wq3 = w_q_up.reshape(MLA_Q_RANK, MLA_HEADS, qk)
    w_q = jnp.pad(wq3, ((0, 0), (0, 0), (0, LANES - qk))).reshape(MLA_Q_RANK, MLA_HEADS * LANES)
    wq_sw = jnp.concatenate([wq3[:, :, MLA_NOPE + half:], wq3[:, :, MLA_NOPE:MLA_NOPE + half]], axis=2)
    w_qsw = jnp.pad(wq_sw, ((0, 0), (0, 0), (MLA_NOPE, LANES - qk))).reshape(MLA_Q_RANK, MLA_HEADS * LANES)
    wkv3 = w_kv_up.reshape(MLA_KV_RANK, MLA_HEADS, MLA_NOPE + MLA_V)
    w_k = jnp.pad(wkv3[:, :, :MLA_NOPE], ((0, 0), (0, 0), (0, LANES - MLA_NOPE))).reshape(MLA_KV_RANK, MLA_HEADS * LANES)
    w_v = wkv3[:, :, MLA_NOPE:].reshape(MLA_KV_RANK, MLA_HEADS * MLA_V)
    swa_scale = SWA_HEAD_DIM ** -0.5
    ws = [w_a, w_qs * swa_scale, w_kvs, w_gate, w_q, w_qsw, w_k, w_v]
    return [w.astype(BF16) for w in ws]


def _rope_table():
    half = MLA_ROPE // 2
    inv_freq = 1.0 / (ROPE_THETA ** (jnp.arange(0, MLA_ROPE, 2, dtype=F32) / MLA_ROPE))
    t = jnp.zeros((8, LANES), F32)
    t = t.at[0, MLA_NOPE:MLA_NOPE + half].set(inv_freq).at[0, MLA_NOPE + half:MLA_NOPE + MLA_ROPE].set(inv_freq)
    t = t.at[1, MLA_NOPE:MLA_NOPE + MLA_ROPE].set(1.0)
    t = t.at[2, MLA_NOPE:MLA_NOPE + half].set(-1.0).at[2, MLA_NOPE + half:MLA_NOPE + MLA_ROPE].set(1.0)
    t = t.at[3, :MLA_NOPE].set(1.0)
    return t


def kernel(x, positions, e_g_in, e_w_in, e_g_q_a, e_w_q_up, e_g_kv_a, e_w_kv_up, e_sinks,
           e_w_out, o_g_in, o_w_in, o_b_f, o_w_out, g_final):
    B, S, D = x.shape
    assert D == D_MODEL and S % PROJ_ROWS == 0 and S % ATTN_TILE == 0 and S % SWA_ROWS == 0
    row = lambda g: g.reshape(1, -1).astype(F32)

    w0 = _layer0_weights(e_w_in[0], e_w_q_up[0], e_w_kv_up[0])
    w_a, w_qs, w_kvs, w_gate, w_q, w_qsw, w_k, w_v = w0
    qm, km, vm, qs, kvs, gate0 = _proj0(
        x, positions.reshape(B, S, 1), row(e_g_in[0]), w_a, w_qs, w_kvs, w_gate,
        row(e_g_q_a[0]), w_q, w_qsw, row(e_g_kv_a[0]), w_k, w_v, _rope_table())
    o_mla = _flash(qm, km, vm, None, q_col0=0, k_col0=0, v_col0=0,
                   n_pairs=MLA_HEADS // 2, head_lanes=LANES)
    slopes = 2.0 ** (-8.0 * (jnp.arange(SWA_HEADS, dtype=F32) + 1.0) / SWA_HEADS)
    o_swa = _swa(slopes, e_sinks[0].astype(F32), qs, kvs)

    sizes = (FOX_WIDTH, FOX_WIDTH, FOX_WIDTH, FOX_HEADS, FOX_WIDTH)
    w1_q, w1_k, w1_v, w1_f, w1_gate = jnp.split(o_w_in[0], np.cumsum(sizes)[:-1].tolist(), axis=1)
    fox_scale = FOX_HEAD_DIM ** -0.5
    w1 = jnp.concatenate([w1_q * fox_scale, w1_k, w1_v, w1_gate], axis=1).astype(BF16)
    w1_f_pad = _pad_cols(w1_f, 0, LANES).astype(BF16)
    x1, z1, f_logit = _mid(x, o_mla, o_swa, gate0, e_w_out[0].astype(BF16), row(o_g_in[0]), w1, w1_f_pad)

    lc = _lcum(f_logit, _pad_cols(row(o_b_f[0]), 0, LANES))
    lc4 = lc.reshape(B, FOX_HEADS, S // ATTN_TILE, 1, ATTN_TILE)
    n_blk = FOX_WIDTH // LANES
    o_fox = _flash(z1, z1, z1, lc4, q_col0=0, k_col0=n_blk, v_col0=2 * n_blk,
                   n_pairs=FOX_HEADS // 2, head_lanes=HALF)
    return _final(x1, o_fox, z1, o_w_out[0].astype(BF16), row(g_final))
```

```python
import functools

import jax
import jax.numpy as jnp
import numpy as np
from jax import lax
from jax.experimental import pallas as pl
from jax.experimental.pallas import tpu as pltpu

F32 = jnp.float32
BF16 = jnp.bfloat16

D_MODEL = 1024
EPS = 1e-6
ROPE_THETA = 10000.0
WINDOW = 128

MLA_HEADS = 8
MLA_Q_RANK = 256
MLA_KV_RANK = 128
MLA_NOPE = 64
MLA_ROPE = 32
MLA_V = 64
SWA_HEADS = 8
SWA_KV_HEADS = 2
SWA_HEAD_DIM = 64
FOX_HEADS = 16
FOX_HEAD_DIM = 64
FOX_WIDTH = FOX_HEADS * FOX_HEAD_DIM

LANES = 128
HALF = 64
V7X_VMEM_LIMIT_BYTES = 56 * 1024 * 1024

PROJ_ROWS = 512
ATTN_TILE = 512
SWA_ROWS = 512
CUM_CHUNK = 256
LOG2E = 1.4426950408889634


def _dot(a, b):
    return jnp.dot(a, b, preferred_element_type=F32)


def _dot_nt(a, b):
    return lax.dot_general(a, b, (((1,), (1,)), ((), ())), preferred_element_type=F32)


def _rms(xf, g):
    ms = jnp.mean(xf * xf, axis=-1, keepdims=True)
    return xf * lax.rsqrt(ms + EPS) * g


def _silu(x):
    return x * (1.0 / (1.0 + jnp.exp(-x)))


def _params(n_parallel):
    return pltpu.CompilerParams(
        dimension_semantics=("parallel",) * n_parallel,
        vmem_limit_bytes=V7X_VMEM_LIMIT_BYTES,
    )


def _resident(shape):
    nd = len(shape)
    return pl.BlockSpec(shape, lambda *_: (0,) * nd, pipeline_mode=pl.Buffered(1))


def _proj0_kernel(x_ref, pos_ref, g_in_ref, w_a_ref, w_qs_ref, w_kvs_ref, w_gate_ref,
                  g_q_ref, w_q_ref, w_qsw_ref, g_kv_ref, w_k_ref, w_v_ref, rope_ref,
                  qm_ref, km_ref, vm_ref, qs_ref, kvs_ref, gate_ref):
    h = _rms(x_ref[0], g_in_ref[...]).astype(BF16)

    za = _dot(h, w_a_ref[...])
    cqn = _rms(za[:, :MLA_Q_RANK], g_q_ref[...]).astype(BF16)
    ckvn = _rms(za[:, MLA_Q_RANK:MLA_Q_RANK + MLA_KV_RANK], g_kv_ref[...]).astype(BF16)
    kpe = za[:, 384:512]
    kpe_sw = za[:, 512:640]

    ang = pos_ref[0].astype(F32) * rope_ref[0:1, :]
    cos_t = jnp.cos(ang) * rope_ref[1:2, :]
    sin_t = jnp.sin(ang) * rope_ref[2:3, :]
    scale = (MLA_NOPE + MLA_ROPE) ** -0.5 * LOG2E
    q_cos = (cos_t + rope_ref[3:4, :]) * scale
    q_sin = sin_t * scale
    kpe_rot = kpe * cos_t + kpe_sw * sin_t

    for c in range(MLA_HEADS // 2):
        cols = slice(c * 2 * LANES, (c + 1) * 2 * LANES)
        q = _dot(cqn, w_q_ref[:, cols])
        q_sw = _dot(cqn, w_qsw_ref[:, cols])
        kn = _dot(ckvn, w_k_ref[:, cols])
        for hh in range(2):
            lo = hh * LANES
            qm_ref[0, :, c * 2 * LANES + lo:c * 2 * LANES + lo + LANES] = (
                q[:, lo:lo + LANES] * q_cos + q_sw[:, lo:lo + LANES] * q_sin).astype(BF16)
            km_ref[0, :, c * 2 * LANES + lo:c * 2 * LANES + lo + LANES] = (
                kn[:, lo:lo + LANES] + kpe_rot).astype(BF16)
    v = _dot(ckvn, w_v_ref[...])
    ones_half = (lax.broadcasted_iota(jnp.int32, v.shape, 1) & HALF) != 0
    vm_ref[0] = jnp.where(ones_half, 1.0, v).astype(BF16)
    qs_ref[0] = _dot(h, w_qs_ref[...]).astype(BF16)
    kvs_ref[0] = _dot(h, w_kvs_ref[...]).astype(BF16)
    gate_ref[0] = _dot(h, w_gate_ref[...]).astype(BF16)


def _proj0(x, pos3, g_in, w_a, w_qs, w_kvs, w_gate, g_q, w_q, w_qsw, g_kv, w_k, w_v, rope):
    B, S, D = x.shape
    tm = PROJ_ROWS
    row = lambda width: pl.BlockSpec((1, tm, width), lambda b, i: (b, i, 0))
    out_widths = (MLA_HEADS * LANES, MLA_HEADS * LANES, MLA_HEADS * LANES,
                  SWA_HEADS * SWA_HEAD_DIM, 4 * LANES, D)
    return pl.pallas_call(
        _proj0_kernel,
        grid=(B, S // tm),
        in_specs=[row(D), row(1)] + [_resident(a.shape) for a in
                                     (g_in, w_a, w_qs, w_kvs, w_gate, g_q, w_q, w_qsw, g_kv, w_k, w_v, rope)],
        out_specs=[row(w) for w in out_widths],
        out_shape=[jax.ShapeDtypeStruct((B, S, w), BF16) for w in out_widths],
        compiler_params=_params(2),
        name="proj0",
    )(x, pos3, g_in, w_a, w_qs, w_kvs, w_gate, g_q, w_q, w_qsw, g_kv, w_k, w_v, rope)


def _flash_kernel(*refs, head_lanes, has_bias):
    ti_ref, tj_ref, q_ref, k_ref, v_ref = refs[:5]
    lc_ref = refs[5] if has_bias else None
    o_ref = refs[5 + has_bias]
    scr = refs[6 + has_bias:]
    p_scr, alpha_scr = scr[0:2], scr[2:4]
    m_scr, acc_scr, mask_scr = scr[4:]
    T = ATTN_TILE
    nq = q_ref.shape[1] // T
    n_steps = ti_ref.shape[0]
    n_chunks = T // LANES
    lane = lax.broadcasted_iota(jnp.int32, (T, LANES), 1)
    lo_half = lane < HALF
    zero = jnp.zeros((), BF16)

    def rows(i):
        return pl.ds(pl.multiple_of(i * T, T), T)

    def softmax(t, slot, diag):
        i, j = ti_ref[t], tj_ref[t]
        r = rows(i)
        if head_lanes == HALF:
            q_pair = q_ref[0, rows(i), :]
            q_heads = [jnp.where(lo_half, q_pair, zero), jnp.where(lo_half, zero, q_pair)]
        for hh in range(2):
            if head_lanes == LANES:
                q_h = q_ref[0, rows(i), hh * LANES:(hh + 1) * LANES]
                k_t = k_ref[0, rows(j), hh * LANES:(hh + 1) * LANES]
            else:
                q_h = q_heads[hh]
                k_t = k_ref[0, rows(j), :]
            s = _dot_nt(q_h, k_t)
            if has_bias:
                s = s + (lc_ref[0, hh, i][:, 0:1] - lc_ref[0, hh, j])
            if diag:
                s = s + mask_scr[...]
            m_prev = m_scr[hh, r, :]
            m_next = jnp.maximum(m_prev, jnp.max(s, axis=1, keepdims=True))
            alpha_scr[slot][hh] = jnp.exp2(m_prev - m_next)
            p = jnp.exp2(s - jnp.concatenate([m_next] * n_chunks, axis=1))
            p_scr[slot][hh] = p.astype(BF16)
            m_scr[hh, r, :] = m_next

    def accumulate(t, slot):
        r = rows(ti_ref[t])
        kr = rows(tj_ref[t])
        for hh in range(2):
            v_t = v_ref[0, kr, hh * LANES:(hh + 1) * LANES]
            acc_scr[hh, r, :] = alpha_scr[slot][hh] * acc_scr[hh, r, :] + _dot(p_scr[slot][hh], v_t)

    m_scr[...] = jnp.full(m_scr.shape, -jnp.inf, F32)
    acc_scr[...] = jnp.zeros(acc_scr.shape, F32)
    row_id = lax.broadcasted_iota(jnp.int32, (T, T), 0)
    col_id = lax.broadcasted_iota(jnp.int32, (T, T), 1)
    mask_scr[...] = jnp.where(col_id <= row_id, 0.0, -jnp.inf).astype(F32)

    softmax(0, 0, True)

    def make_body(diag):
        def body(u, carry):
            t = 2 * u + 1
            softmax(t, 1, False)
            accumulate(t - 1, 0)
            softmax(t + 1, 0, diag)
            accumulate(t, 1)
            return carry
        return body

    lax.fori_loop(0, nq - 1, make_body(True), 0)
    lax.fori_loop(nq - 1, n_steps // 2 - 1, make_body(False), 0)
    softmax(n_steps - 1, 1, False)
    accumulate(n_steps - 2, 0)
    accumulate(n_steps - 1, 1)

    def normalize(i, carry):
        r = rows(i)
        o_heads = []
        for hh in range(2):
            acc = acc_scr[hh, r, :]
            o_heads.append(acc / pltpu.roll(acc, HALF, 1))
        o_pair = jnp.where(lo_half, o_heads[0], pltpu.roll(o_heads[1], HALF, 1))
        o_ref[0, r, :] = o_pair.astype(o_ref.dtype)
        return carry

    lax.fori_loop(0, nq, normalize, 0)


def _tile_schedule(nq):
    diag = [(i, i) for i in range(nq)]
    full = [(i, j) for i in range(nq) for j in range(i)]
    n_even = (len(diag) + len(full)) // 2
    assert (len(diag) + len(full)) % 2 == 0 and n_even >= len(diag) and n_even >= 2
    even = diag + full[:n_even - len(diag)]
    odd = full[n_even - len(diag):]
    return np.array([t for pair in zip(even, odd) for t in pair], np.int32)


def _flash(q, k, v, lc, *, q_col0, k_col0, v_col0, n_pairs, head_lanes):
    B, S, _ = q.shape
    T = ATTN_TILE
    qk_w = 2 * head_lanes
    qk_step = qk_w // LANES
    steps = _tile_schedule(S // T)
    smem = pl.BlockSpec(memory_space=pltpu.SMEM)
    in_specs = [
        smem, smem,
        pl.BlockSpec((1, S, qk_w), lambda b, p: (b, 0, q_col0 // qk_step + p)),
        pl.BlockSpec((1, S, qk_w), lambda b, p: (b, 0, k_col0 // qk_step + p)),
        pl.BlockSpec((1, S, 2 * LANES), lambda b, p: (b, 0, v_col0 // 2 + p)),
    ]
    args = [jnp.asarray(steps[:, 0]), jnp.asarray(steps[:, 1]), q, k, v]
    if lc is not None:
        in_specs.append(pl.BlockSpec((1, 2) + lc.shape[2:], lambda b, p: (b, p, 0, 0, 0)))
        args.append(lc)
    per_slot = lambda shape, dtype: [pltpu.VMEM(shape, dtype)] * 2
    scratch = (per_slot((2, T, T), BF16) + per_slot((2, T, LANES), F32)
               + [pltpu.VMEM((2, S, LANES), F32)] * 2
               + [pltpu.VMEM((T, T), F32)])
    return pl.pallas_call(
        functools.partial(_flash_kernel, head_lanes=head_lanes, has_bias=lc is not None),
        grid=(B, n_pairs),
        in_specs=in_specs,
        out_specs=pl.BlockSpec((1, S, LANES), lambda b, p: (b, 0, p)),
        out_shape=jax.ShapeDtypeStruct((B, S, n_pairs * LANES), BF16),
        scratch_shapes=scratch,
        compiler_params=_params(2),
        name="fox" if lc is not None else "mla",
    )(*args)


def _swa_kernel(slopes_ref, sinks_ref, qs_ref, kvs_ref, o_ref):
    W = WINDOW
    group = SWA_HEADS // SWA_KV_HEADS
    step = pl.program_id(1)
    lane = lax.broadcasted_iota(jnp.int32, (W, LANES), 1)
    lo_half = lane < HALF
    a_id = lax.broadcasted_iota(jnp.int32, (W, 2 * W), 0)
    c_id = lax.broadcasted_iota(jnp.int32, (W, 2 * W), 1)
    zero = jnp.zeros((), BF16)

    for blk in range(SWA_ROWS // W):
        t0 = step * SWA_ROWS + blk * W
        k_start = pl.multiple_of(jnp.maximum(t0 - W, 0), W)
        dist_i = (t0 + a_id) - (k_start + c_id)
        valid = (dist_i >= 0) & (dist_i < W)
        dist = dist_i.astype(F32)
        rows = slice(blk * W, (blk + 1) * W)
        for g in range(SWA_KV_HEADS):
            k_t = kvs_ref[0, pl.ds(k_start, 2 * W), g * LANES:(g + 1) * LANES]
            v_t = kvs_ref[0, pl.ds(k_start, 2 * W), (2 + g) * LANES:(3 + g) * LANES]
            q_parts = []
            for pair in range(group // 2):
                q_pair = qs_ref[0, rows, (g * 2 + pair) * LANES:(g * 2 + pair + 1) * LANES]
                q_parts.append(jnp.where(lo_half, q_pair, zero))
                q_parts.append(jnp.where(lo_half, zero, q_pair))
            s_all = _dot_nt(jnp.concatenate(q_parts, axis=0), k_t)
            p_parts, denoms = [], []
            for hq in range(group):
                head = g * group + hq
                sink = sinks_ref[head]
                s = s_all[hq * W:(hq + 1) * W] - slopes_ref[head] * dist
                s = jnp.where(valid, s, -jnp.inf)
                m = jnp.maximum(jnp.max(s, axis=1, keepdims=True), sink)
                p = jnp.exp(s - m)
                denoms.append(jnp.sum(p, axis=1, keepdims=True) + jnp.exp(sink - m))
                p_parts.append(p.astype(BF16))
            o_all = _dot(jnp.concatenate(p_parts, axis=0), v_t)
            for pair in range(group // 2):
                o_even = o_all[(2 * pair) * W:(2 * pair + 1) * W] / denoms[2 * pair]
                o_odd = o_all[(2 * pair + 1) * W:(2 * pair + 2) * W] / denoms[2 * pair + 1]
                o_ref[0, rows, (g * 2 + pair) * LANES:(g * 2 + pair + 1) * LANES] = jnp.where(
                    lo_half, o_even, o_odd).astype(o_ref.dtype)


def _swa(slopes, sinks, qs, kvs):
    B, S, C = qs.shape
    smem = pl.BlockSpec(memory_space=pltpu.SMEM)
    return pl.pallas_call(
        _swa_kernel,
        grid=(B, S // SWA_ROWS),
        in_specs=[smem, smem,
                  pl.BlockSpec((1, SWA_ROWS, C), lambda b, i: (b, i, 0)),
                  pl.BlockSpec((1, S, kvs.shape[2]), lambda b, i: (b, 0, 0))],
        out_specs=pl.BlockSpec((1, SWA_ROWS, C), lambda b, i: (b, i, 0)),
        out_shape=jax.ShapeDtypeStruct((B, S, C), BF16),
        compiler_params=_params(2),
        name="swa",
    )(slopes, sinks, qs, kvs)


def _mid_kernel(x_ref, om_ref, os_ref, gate_ref, w_out_ref, g_in_ref, w_in_ref, w_f_ref,
                x1_ref, qk_ref, v_ref, g1_ref, f_ref):
    half = D_MODEL // 2
    gate = _silu(gate_ref[0].astype(F32))
    a_m = (om_ref[0].astype(F32) * gate[:, :half]).astype(BF16)
    a_s = (os_ref[0].astype(F32) * gate[:, half:]).astype(BF16)
    x1 = x_ref[0] + (_dot(a_m, w_out_ref[:half, :]) + _dot(a_s, w_out_ref[half:, :]))
    x1_ref[0] = x1
    h = _rms(x1, g_in_ref[...]).astype(BF16)
    w = FOX_WIDTH
    qk_ref[0, :, :w] = (_dot(h, w_in_ref[:, :w]) * LOG2E).astype(BF16)
    qk_ref[0, :, w:] = _dot(h, w_in_ref[:, w:2 * w]).astype(BF16)
    g1_ref[0] = _dot(h, w_in_ref[:, 3 * w:]).astype(BF16)
    f_ref[0] = _dot(h, w_f_ref[...])
    v = _dot(h, w_in_ref[:, 2 * w:3 * w])
    lo_half = lax.broadcasted_iota(jnp.int32, (v.shape[0], LANES), 1) < HALF
    for pair in range(FOX_HEADS // 2):
        v_pair = v[:, pair * LANES:(pair + 1) * LANES]
        v_ref[0, :, (2 * pair) * LANES:(2 * pair + 1) * LANES] = jnp.where(lo_half, v_pair, 1.0).astype(BF16)
        v_ref[0, :, (2 * pair + 1) * LANES:(2 * pair + 2) * LANES] = jnp.where(
            lo_half, pltpu.roll(v_pair, HALF, 1), 1.0).astype(BF16)


def _mid(x, o_mla, o_swa, gate, w_out, g_in, w_in, w_f):
    B, S, D = x.shape
    tm = PROJ_ROWS
    row = lambda width: pl.BlockSpec((1, tm, width), lambda b, i: (b, i, 0))
    outs = [(D, F32), (2 * FOX_WIDTH, BF16), (FOX_HEADS * LANES, BF16), (FOX_WIDTH, BF16), (LANES, F32)]
    return pl.pallas_call(
        _mid_kernel,
        grid=(B, S // tm),
        in_specs=[row(D), row(D // 2), row(D // 2), row(D)] + [_resident(a.shape) for a in (w_out, g_in, w_in, w_f)],
        out_specs=[row(width) for width, _ in outs],
        out_shape=[jax.ShapeDtypeStruct((B, S, width), dtype) for width, dtype in outs],
        compiler_params=_params(2),
        name="mid",
    )(x, o_mla, o_swa, gate, w_out, g_in, w_in, w_f)


def _split3(x):
    hi = x.astype(BF16)
    r1 = x - hi.astype(F32)
    mid = r1.astype(BF16)
    lo = (r1 - mid.astype(F32)).astype(BF16)
    return hi, mid, lo


def _lcum_kernel(f_ref, b_ref, lc_ref):
    S = f_ref.shape[1]
    log_f = jax.nn.log_sigmoid(f_ref[0] + b_ref[...])
    log_f_t = log_f.T[:FOX_HEADS]
    n = CUM_CHUNK
    tri = (lax.broadcasted_iota(jnp.int32, (n, n), 0) <= lax.broadcasted_iota(jnp.int32, (n, n), 1)).astype(BF16)
    carry = jnp.zeros((FOX_HEADS, 1), F32)
    for c in range(S // n):
        hi, mid, lo = _split3(log_f_t[:, c * n:(c + 1) * n])
        cs = (_dot(hi, tri) + _dot(mid, tri)) + _dot(lo, tri) + carry
        lc_ref[0, :, c * n:(c + 1) * n] = cs * LOG2E
        carry = cs[:, n - 1:n]


def _lcum(f, b_pad):
    B, S, _ = f.shape
    return pl.pallas_call(
        _lcum_kernel,
        grid=(B,),
        in_specs=[pl.BlockSpec((1, S, LANES), lambda b: (b, 0, 0)), _resident(b_pad.shape)],
        out_specs=pl.BlockSpec((1, FOX_HEADS, S), lambda b: (b, 0, 0)),
        out_shape=jax.ShapeDtypeStruct((B, FOX_HEADS, S), F32),
        compiler_params=_params(1),
        name="lcum",
    )(f, b_pad)


def _final_kernel(x1_ref, o_ref, gate_ref, w_out_ref, g_ref, y_ref):
    a = (o_ref[0].astype(F32) * _silu(gate_ref[0].astype(F32))).astype(BF16)
    x2 = x1_ref[0] + _dot(a, w_out_ref[...])
    y_ref[0] = _rms(x2, g_ref[...])


def _final(x1, o_fox, gate, w_out, g_final):
    B, S, D = x1.shape
    tm = PROJ_ROWS
    row = pl.BlockSpec((1, tm, D), lambda b, i: (b, i, 0))
    return pl.pallas_call(
        _final_kernel,
        grid=(B, S // tm),
        in_specs=[row, row, row, _resident(w_out.shape), _resident(g_final.shape)],
        out_specs=row,
        out_shape=jax.ShapeDtypeStruct((B, S, D), F32),
        compiler_params=_params(2),
        name="final",
    )(x1, o_fox, gate, w_out, g_final)


def _pad_cols(w, left, total):
    return jnp.pad(w, ((0, 0), (left, total - left - w.shape[1])))


def _layer0_weights(w_in, w_q_up, w_kv_up):
    sizes = (MLA_Q_RANK, MLA_KV_RANK, MLA_ROPE, SWA_HEADS * SWA_HEAD_DIM,
             SWA_KV_HEADS * SWA_HEAD_DIM, SWA_KV_HEADS * SWA_HEAD_DIM, D_MODEL)
    w_cq, w_ckv, w_kpe, w_qs, w_ks, w_vs, w_gate = jnp.split(w_in, np.cumsum(sizes)[:-1].tolist(), axis=1)
    half = MLA_ROPE // 2
    w_kpe_sw = jnp.concatenate([w_kpe[:, half:], w_kpe[:, :half]], axis=1)
    w_a = jnp.concatenate([w_cq, w_ckv, _pad_cols(w_kpe, MLA_NOPE, LANES), _pad_cols(w_kpe_sw, MLA_NOPE, LANES)], axis=1)
    d = SWA_HEAD_DIM
    w_kvs = jnp.concatenate([w_ks[:, :d], w_ks[:, :d], w_ks[:, d:], w_ks[:, d:],
                             w_vs[:, :d], w_vs[:, :d], w_vs[:, d:], w_vs[:, d:]], axis=1)
    qk = MLA_NOPE + MLA_ROPE
    wq3 = w_q_up.reshape(MLA_Q_RANK, MLA_HEADS, qk)
    w_q = jnp.pad(wq3, ((0, 0), (0, 0), (0, LANES - qk))).reshape(MLA_Q_RANK, MLA_HEADS * LANES)
    wq_sw = jnp.concatenate([wq3[:, :, MLA_NOPE + half:], wq3[:, :, MLA_NOPE:MLA_NOPE + half]], axis=2)
    w_qsw = jnp.pad(wq_sw, ((0, 0), (0, 0), (MLA_NOPE, LANES - qk))).reshape(MLA_Q_RANK, MLA_HEADS * LANES)
    wkv3 = w_kv_up.reshape(MLA_KV_RANK, MLA_HEADS, MLA_NOPE + MLA_V)
    w_k = jnp.pad(wkv3[:, :, :MLA_NOPE], ((0, 0), (0, 0), (0, LANES - MLA_NOPE))).reshape(MLA_KV_RANK, MLA_HEADS * LANES)
    w_v = jnp.pad(wkv3[:, :, MLA_NOPE:], ((0, 0), (0, 0), (0, LANES - MLA_V))).reshape(MLA_KV_RANK, MLA_HEADS * LANES)
    swa_scale = SWA_HEAD_DIM ** -0.5
    ws = [w_a, w_qs * swa_scale, w_kvs, w_gate, w_q, w_qsw, w_k, w_v]
    return [w.astype(BF16) for w in ws]


def _rope_table():
    half = MLA_ROPE // 2
    inv_freq = 1.0 / (ROPE_THETA ** (jnp.arange(0, MLA_ROPE, 2, dtype=F32) / MLA_ROPE))
    t = jnp.zeros((8, LANES), F32)
    t = t.at[0, MLA_NOPE:MLA_NOPE + half].set(inv_freq).at[0, MLA_NOPE + half:MLA_NOPE + MLA_ROPE].set(inv_freq)
    t = t.at[1, MLA_NOPE:MLA_NOPE + MLA_ROPE].set(1.0)
    t = t.at[2, MLA_NOPE:MLA_NOPE + half].set(-1.0).at[2, MLA_NOPE + half:MLA_NOPE + MLA_ROPE].set(1.0)
    t = t.at[3, :MLA_NOPE].set(1.0)
    return t


def kernel(x, positions, e_g_in, e_w_in, e_g_q_a, e_w_q_up, e_g_kv_a, e_w_kv_up, e_sinks,
           e_w_out, o_g_in, o_w_in, o_b_f, o_w_out, g_final):
    B, S, D = x.shape
    assert D == D_MODEL and S % PROJ_ROWS == 0 and S % ATTN_TILE == 0 and S % SWA_ROWS == 0
    row = lambda g: g.reshape(1, -1).astype(F32)

    w0 = _layer0_weights(e_w_in[0], e_w_q_up[0], e_w_kv_up[0])
    w_a, w_qs, w_kvs, w_gate, w_q, w_qsw, w_k, w_v = w0
    qm, km, vm, qs, kvs, gate0 = _proj0(
        x, positions.reshape(B, S, 1), row(e_g_in[0]), w_a, w_qs, w_kvs, w_gate,
        row(e_g_q_a[0]), w_q, w_qsw, row(e_g_kv_a[0]), w_k, w_v, _rope_table())
    o_mla = _flash(qm, km, vm, None, q_col0=0, k_col0=0, v_col0=0,
                   n_pairs=MLA_HEADS // 2, head_lanes=LANES)
    slopes = 2.0 ** (-8.0 * (jnp.arange(SWA_HEADS, dtype=F32) + 1.0) / SWA_HEADS)
    o_swa = _swa(slopes, e_sinks[0].astype(F32), qs, kvs)

    sizes = (FOX_WIDTH, FOX_WIDTH, FOX_WIDTH, FOX_HEADS, FOX_WIDTH)
    w1_q, w1_k, w1_v, w1_f, w1_gate = jnp.split(o_w_in[0], np.cumsum(sizes)[:-1].tolist(), axis=1)
    fox_scale = FOX_HEAD_DIM ** -0.5
    w1 = jnp.concatenate([w1_q * fox_scale, w1_k, w1_v, w1_gate], axis=1).astype(BF16)
    w1_f_pad = _pad_cols(w1_f, 0, LANES).astype(BF16)
    x1, qk1, v1, gate1, f_logit = _mid(x, o_mla, o_swa, gate0, e_w_out[0].astype(BF16), row(o_g_in[0]), w1, w1_f_pad)

    lc = _lcum(f_logit, _pad_cols(row(o_b_f[0]), 0, LANES))
    lc4 = lc.reshape(B, FOX_HEADS, S // ATTN_TILE, 1, ATTN_TILE)
    o_fox = _flash(qk1, qk1, v1, lc4, q_col0=0, k_col0=FOX_WIDTH // LANES, v_col0=0,
                   n_pairs=FOX_HEADS // 2, head_lanes=HALF)
    return _final(x1, o_fox, gate1, o_w_out[0].astype(BF16), row(g_final))
```

```python
import functools

import jax
import jax.numpy as jnp
import numpy as np
from jax import lax
from jax.experimental import pallas as pl
from jax.experimental.pallas import tpu as pltpu

F32 = jnp.float32
BF16 = jnp.bfloat16

D_MODEL = 1024
EPS = 1e-6
ROPE_THETA = 10000.0
WINDOW = 128

MLA_HEADS = 8
MLA_Q_RANK = 256
MLA_KV_RANK = 128
MLA_NOPE = 64
MLA_ROPE = 32
MLA_V = 64
SWA_HEADS = 8
SWA_KV_HEADS = 2
SWA_HEAD_DIM = 64
FOX_HEADS = 16
FOX_HEAD_DIM = 64
FOX_WIDTH = FOX_HEADS * FOX_HEAD_DIM

LANES = 128
HALF = 64
V7X_VMEM_LIMIT_BYTES = 56 * 1024 * 1024

PROJ_ROWS = 512
ATTN_TILE = 512
SWA_ROWS = 512
CUM_CHUNK = 256
LOG2E = 1.4426950408889634


def _dot(a, b):
    return jnp.dot(a, b, preferred_element_type=F32)


def _dot_nt(a, b):
    return lax.dot_general(a, b, (((1,), (1,)), ((), ())), preferred_element_type=F32)


def _rms(xf, g):
    ms = jnp.mean(xf * xf, axis=-1, keepdims=True)
    return xf * lax.rsqrt(ms + EPS) * g


def _silu(x):
    return x * (1.0 / (1.0 + jnp.exp(-x)))


def _params(n_parallel):
    return pltpu.CompilerParams(
        dimension_semantics=("parallel",) * n_parallel,
        vmem_limit_bytes=V7X_VMEM_LIMIT_BYTES,
    )


def _resident(shape):
    nd = len(shape)
    return pl.BlockSpec(shape, lambda *_: (0,) * nd, pipeline_mode=pl.Buffered(1))


def _proj0_kernel(x_ref, pos_ref, g_in_ref, w_a_ref, w_qs_ref, w_kvs_ref, w_gate_ref,
                  g_q_ref, w_q_ref, w_qsw_ref, g_kv_ref, w_k_ref, w_v_ref, rope_ref,
                  qm_ref, km_ref, vm_ref, qs_ref, kvs_ref, gate_ref):
    h = _rms(x_ref[0], g_in_ref[...]).astype(BF16)

    za = _dot(h, w_a_ref[...])
    cqn = _rms(za[:, :MLA_Q_RANK], g_q_ref[...]).astype(BF16)
    ckvn = _rms(za[:, MLA_Q_RANK:MLA_Q_RANK + MLA_KV_RANK], g_kv_ref[...]).astype(BF16)
    kpe = za[:, 384:512]
    kpe_sw = za[:, 512:640]

    ang = pos_ref[0].astype(F32) * rope_ref[0:1, :]
    cos_t = jnp.cos(ang) * rope_ref[1:2, :]
    sin_t = jnp.sin(ang) * rope_ref[2:3, :]
    scale = (MLA_NOPE + MLA_ROPE) ** -0.5 * LOG2E
    q_cos = (cos_t + rope_ref[3:4, :]) * scale
    q_sin = sin_t * scale
    kpe_rot = kpe * cos_t + kpe_sw * sin_t

    for c in range(MLA_HEADS // 2):
        cols = slice(c * 2 * LANES, (c + 1) * 2 * LANES)
        q = _dot(cqn, w_q_ref[:, cols])
        q_sw = _dot(cqn, w_qsw_ref[:, cols])
        kn = _dot(ckvn, w_k_ref[:, cols])
        for hh in range(2):
            lo = hh * LANES
            qm_ref[0, :, c * 2 * LANES + lo:c * 2 * LANES + lo + LANES] = (
                q[:, lo:lo + LANES] * q_cos + q_sw[:, lo:lo + LANES] * q_sin).astype(BF16)
            km_ref[0, :, c * 2 * LANES + lo:c * 2 * LANES + lo + LANES] = (
                kn[:, lo:lo + LANES] + kpe_rot).astype(BF16)
    v = _dot(ckvn, w_v_ref[...])
    lane_v = lax.broadcasted_iota(jnp.int32, v.shape, 1)
    ones_half = (((lane_v >> 6) ^ (lane_v >> 7)) & 1) == 1
    vm_ref[0] = jnp.where(ones_half, 1.0, v).astype(BF16)
    qs_ref[0] = _dot(h, w_qs_ref[...]).astype(BF16)
    kvs_ref[0] = _dot(h, w_kvs_ref[...]).astype(BF16)
    gate_ref[0] = _dot(h, w_gate_ref[...]).astype(BF16)


def _proj0(x, pos3, g_in, w_a, w_qs, w_kvs, w_gate, g_q, w_q, w_qsw, g_kv, w_k, w_v, rope):
    B, S, D = x.shape
    tm = PROJ_ROWS
    row = lambda width: pl.BlockSpec((1, tm, width), lambda b, i: (b, i, 0))
    out_widths = (MLA_HEADS * LANES, MLA_HEADS * LANES, MLA_HEADS * LANES,
                  SWA_HEADS * SWA_HEAD_DIM, 4 * LANES, D)
    return pl.pallas_call(
        _proj0_kernel,
        grid=(B, S // tm),
        in_specs=[row(D), row(1)] + [_resident(a.shape) for a in
                                     (g_in, w_a, w_qs, w_kvs, w_gate, g_q, w_q, w_qsw, g_kv, w_k, w_v, rope)],
        out_specs=[row(w) for w in out_widths],
        out_shape=[jax.ShapeDtypeStruct((B, S, w), BF16) for w in out_widths],
        compiler_params=_params(2),
        name="proj0",
    )(x, pos3, g_in, w_a, w_qs, w_kvs, w_gate, g_q, w_q, w_qsw, g_kv, w_k, w_v, rope)


def _flash_kernel(*refs, head_lanes, has_bias):
    ti_ref, tj_ref, q_ref, k_ref, v_ref = refs[:5]
    lc_ref = refs[5] if has_bias else None
    o_ref = refs[5 + has_bias]
    scr = refs[6 + has_bias:]
    p_scr, alpha_scr = scr[0:2], scr[2:4]
    m_scr, acc_scr, mask_scr = scr[4:]
    T = ATTN_TILE
    nq = q_ref.shape[1] // T
    n_steps = ti_ref.shape[0]
    n_chunks = T // LANES
    lane = lax.broadcasted_iota(jnp.int32, (T, LANES), 1)
    lo_half = lane < HALF
    zero = jnp.zeros((), BF16)

    def rows(i):
        return pl.ds(pl.multiple_of(i * T, T), T)

    def softmax(t, slot, diag):
        i, j = ti_ref[t], tj_ref[t]
        r = rows(i)
        if head_lanes == HALF:
            q_pair = q_ref[0, rows(i), :]
            q_heads = [jnp.where(lo_half, q_pair, zero), jnp.where(lo_half, zero, q_pair)]
        for hh in range(2):
            if head_lanes == LANES:
                q_h = q_ref[0, rows(i), hh * LANES:(hh + 1) * LANES]
                k_t = k_ref[0, rows(j), hh * LANES:(hh + 1) * LANES]
            else:
                q_h = q_heads[hh]
                k_t = k_ref[0, rows(j), :]
            s = _dot_nt(q_h, k_t)
            if has_bias:
                s = s + (lc_ref[0, hh, i][:, 0:1] - lc_ref[0, hh, j])
            if diag:
                s = s + mask_scr[...]
            m_prev = m_scr[hh, r, :]
            m_next = jnp.maximum(m_prev, jnp.max(s, axis=1, keepdims=True))
            alpha_scr[slot][hh] = jnp.exp2(m_prev - m_next)
            p = jnp.exp2(s - jnp.concatenate([m_next] * n_chunks, axis=1))
            p_scr[slot][hh] = p.astype(BF16)
            m_scr[hh, r, :] = m_next

    def accumulate(t, slot):
        r = rows(ti_ref[t])
        kr = rows(tj_ref[t])
        for hh in range(2):
            v_t = v_ref[0, kr, hh * LANES:(hh + 1) * LANES]
            acc_scr[hh, r, :] = alpha_scr[slot][hh] * acc_scr[hh, r, :] + _dot(p_scr[slot][hh], v_t)

    m_scr[...] = jnp.full(m_scr.shape, -jnp.inf, F32)
    acc_scr[...] = jnp.zeros(acc_scr.shape, F32)
    row_id = lax.broadcasted_iota(jnp.int32, (T, T), 0)
    col_id = lax.broadcasted_iota(jnp.int32, (T, T), 1)
    mask_scr[...] = jnp.where(col_id <= row_id, 0.0, -jnp.inf).astype(F32)

    def step_pair(t, diag):
        softmax(t, 1, False)
        accumulate(t - 1, 0)
        softmax(t + 1, 0, diag)
        accumulate(t, 1)

    def make_body(diag):
        def body(u, carry):
            step_pair(4 * u + 3, diag)
            step_pair(4 * u + 5, diag)
            return carry
        return body

    last_diag = 2 * (nq - 1)
    assert (last_diag - 6) % 4 == 0 and (n_steps - 4) % 4 == 0
    n_diag_bodies = (last_diag - 6) // 4 + 1
    softmax(0, 0, True)
    step_pair(1, True)
    lax.fori_loop(0, n_diag_bodies, make_body(True), 0)
    lax.fori_loop(n_diag_bodies, (n_steps - 4) // 4, make_body(False), 0)
    softmax(n_steps - 1, 1, False)
    accumulate(n_steps - 2, 0)
    accumulate(n_steps - 1, 1)

    def normalize(i, carry):
        r = rows(i)
        acc_even, acc_odd = acc_scr[0, r, :], acc_scr[1, r, :]
        num = jnp.where(lo_half, acc_even, acc_odd)
        den = pltpu.roll(jnp.where(lo_half, acc_odd, acc_even), HALF, 1)
        o_ref[0, r, :] = (num / den).astype(o_ref.dtype)
        return carry

    lax.fori_loop(0, nq, normalize, 0)


def _tile_schedule(nq):
    diag = [(i, i) for i in range(nq)]
    full = [(i, j) for i in range(nq) for j in range(i)]
    n_even = (len(diag) + len(full)) // 2
    assert (len(diag) + len(full)) % 2 == 0 and n_even >= len(diag) and n_even >= 2
    even = diag + full[:n_even - len(diag)]
    odd = full[n_even - len(diag):]
    return np.array([t for pair in zip(even, odd) for t in pair], np.int32)


def _flash(q, k, v, lc, *, q_col0, k_col0, v_col0, n_pairs, head_lanes):
    B, S, _ = q.shape
    T = ATTN_TILE
    qk_w = 2 * head_lanes
    qk_step = qk_w // LANES
    steps = _tile_schedule(S // T)
    smem = pl.BlockSpec(memory_space=pltpu.SMEM)
    in_specs = [
        smem, smem,
        pl.BlockSpec((1, S, qk_w), lambda b, p: (b, 0, q_col0 // qk_step + p)),
        pl.BlockSpec((1, S, qk_w), lambda b, p: (b, 0, k_col0 // qk_step + p)),
        pl.BlockSpec((1, S, 2 * LANES), lambda b, p: (b, 0, v_col0 // 2 + p)),
    ]
    args = [jnp.asarray(steps[:, 0]), jnp.asarray(steps[:, 1]), q, k, v]
    if lc is not None:
        in_specs.append(pl.BlockSpec((1, 2) + lc.shape[2:], lambda b, p: (b, p, 0, 0, 0)))
        args.append(lc)
    per_slot = lambda shape, dtype: [pltpu.VMEM(shape, dtype)] * 2
    scratch = (per_slot((2, T, T), BF16) + per_slot((2, T, LANES), F32)
               + [pltpu.VMEM((2, S, LANES), F32)] * 2
               + [pltpu.VMEM((T, T), F32)])
    return pl.pallas_call(
        functools.partial(_flash_kernel, head_lanes=head_lanes, has_bias=lc is not None),
        grid=(B, n_pairs),
        in_specs=in_specs,
        out_specs=pl.BlockSpec((1, S, LANES), lambda b, p: (b, 0, p)),
        out_shape=jax.ShapeDtypeStruct((B, S, n_pairs * LANES), BF16),
        scratch_shapes=scratch,
        compiler_params=_params(2),
        name="fox" if lc is not None else "mla",
    )(*args)


def _swa_kernel(sinks_ref, bias_ref, qs_ref, kvs_ref, o_ref):
    W = WINDOW
    group = SWA_HEADS // SWA_KV_HEADS
    step = pl.program_id(1)
    lane = lax.broadcasted_iota(jnp.int32, (W, LANES), 1)
    lo_half = lane < HALF
    zero = jnp.zeros((), BF16)

    for blk in range(SWA_ROWS // W):
        t0 = step * SWA_ROWS + blk * W
        k_start = pl.multiple_of(jnp.maximum(t0 - W, 0), W)
        bias_sel = jnp.where(step == 0, 0, 1) if blk == 0 else 1
        rows = slice(blk * W, (blk + 1) * W)
        for g in range(SWA_KV_HEADS):
            k_t = kvs_ref[0, pl.ds(k_start, 2 * W), g * LANES:(g + 1) * LANES]
            v_t = kvs_ref[0, pl.ds(k_start, 2 * W), (2 + g) * LANES:(3 + g) * LANES]
            q_parts = []
            for pair in range(group // 2):
                q_pair = qs_ref[0, rows, (g * 2 + pair) * LANES:(g * 2 + pair + 1) * LANES]
                q_parts.append(jnp.where(lo_half, q_pair, zero))
                q_parts.append(jnp.where(lo_half, zero, q_pair))
            s_all = _dot_nt(jnp.concatenate(q_parts, axis=0), k_t)
            p_parts, denoms = [], []
            for hq in range(group):
                head = g * group + hq
                sink = sinks_ref[head]
                s = s_all[hq * W:(hq + 1) * W] + bias_ref[bias_sel, head]
                m = jnp.maximum(jnp.max(s, axis=1, keepdims=True), sink)
                p = jnp.exp(s - m)
                denoms.append(jnp.sum(p, axis=1, keepdims=True) + jnp.exp(sink - m))
                p_parts.append(p.astype(BF16))
            o_all = _dot(jnp.concatenate(p_parts, axis=0), v_t)
            for pair in range(group // 2):
                o_even = o_all[(2 * pair) * W:(2 * pair + 1) * W] / denoms[2 * pair]
                o_odd = o_all[(2 * pair + 1) * W:(2 * pair + 2) * W] / denoms[2 * pair + 1]
                o_ref[0, rows, (g * 2 + pair) * LANES:(g * 2 + pair + 1) * LANES] = jnp.where(
                    lo_half, o_even, o_odd).astype(o_ref.dtype)


def _swa_bias():
    W = WINDOW
    slopes = 2.0 ** (-8.0 * (np.arange(SWA_HEADS, dtype=np.float32) + 1.0) / SWA_HEADS)
    a = np.arange(W, dtype=np.float32)[:, None]
    c = np.arange(2 * W, dtype=np.float32)[None, :]
    tables = []
    for k_offset in (0.0, float(W)):
        dist = k_offset + a - c
        bias = -slopes.astype(np.float32)[:, None, None] * dist[None]
        tables.append(np.where((dist >= 0) & (dist < W), bias, -np.inf).astype(np.float32))
    return jnp.asarray(np.stack(tables))


def _swa(sinks, qs, kvs):
    B, S, C = qs.shape
    bias = _swa_bias()
    return pl.pallas_call(
        _swa_kernel,
        grid=(B, S // SWA_ROWS),
        in_specs=[pl.BlockSpec(memory_space=pltpu.SMEM), _resident(bias.shape),
                  pl.BlockSpec((1, SWA_ROWS, C), lambda b, i: (b, i, 0)),
                  pl.BlockSpec((1, S, kvs.shape[2]), lambda b, i: (b, 0, 0))],
        out_specs=pl.BlockSpec((1, SWA_ROWS, C), lambda b, i: (b, i, 0)),
        out_shape=jax.ShapeDtypeStruct((B, S, C), BF16),
        compiler_params=_params(2),
        name="swa",
    )(sinks, bias, qs, kvs)


def _mid_kernel(x_ref, om_ref, os_ref, gate_ref, w_out_ref, g_in_ref, w_in_ref, w_f_ref,
                x1_ref, qk_ref, v_ref, g1_ref, f_ref):
    half = D_MODEL // 2
    gate = _silu(gate_ref[0].astype(F32))
    a_m = (om_ref[0].astype(F32) * gate[:, :half]).astype(BF16)
    a_s = (os_ref[0].astype(F32) * gate[:, half:]).astype(BF16)
    x1 = x_ref[0] + (_dot(a_m, w_out_ref[:half, :]) + _dot(a_s, w_out_ref[half:, :]))
    x1_ref[0] = x1
    h = _rms(x1, g_in_ref[...]).astype(BF16)
    w = FOX_WIDTH
    qk_ref[0, :, :w] = (_dot(h, w_in_ref[:, :w]) * LOG2E).astype(BF16)
    qk_ref[0, :, w:] = _dot(h, w_in_ref[:, w:2 * w]).astype(BF16)
    g1_ref[0] = _dot(h, w_in_ref[:, 3 * w:]).astype(BF16)
    f_ref[0] = _dot(h, w_f_ref[...])
    v = _dot(h, w_in_ref[:, 2 * w:3 * w])
    lo_half = lax.broadcasted_iota(jnp.int32, (v.shape[0], LANES), 1) < HALF
    for pair in range(FOX_HEADS // 2):
        v_pair = v[:, pair * LANES:(pair + 1) * LANES]
        v_ref[0, :, (2 * pair) * LANES:(2 * pair + 1) * LANES] = jnp.where(lo_half, v_pair, 1.0).astype(BF16)
        v_ref[0, :, (2 * pair + 1) * LANES:(2 * pair + 2) * LANES] = jnp.where(lo_half, 1.0, v_pair).astype(BF16)


def _mid(x, o_mla, o_swa, gate, w_out, g_in, w_in, w_f):
    B, S, D = x.shape
    tm = PROJ_ROWS
    row = lambda width: pl.BlockSpec((1, tm, width), lambda b, i: (b, i, 0))
    outs = [(D, F32), (2 * FOX_WIDTH, BF16), (FOX_HEADS * LANES, BF16), (FOX_WIDTH, BF16), (LANES, F32)]
    return pl.pallas_call(
        _mid_kernel,
        grid=(B, S // tm),
        in_specs=[row(D), row(D // 2), row(D // 2), row(D)] + [_resident(a.shape) for a in (w_out, g_in, w_in, w_f)],
        out_specs=[row(width) for width, _ in outs],
        out_shape=[jax.ShapeDtypeStruct((B, S, width), dtype) for width, dtype in outs],
        compiler_params=_params(2),
        name="mid",
    )(x, o_mla, o_swa, gate, w_out, g_in, w_in, w_f)


def _split3(x):
    hi = x.astype(BF16)
    r1 = x - hi.astype(F32)
    mid = r1.astype(BF16)
    lo = (r1 - mid.astype(F32)).astype(BF16)
    return hi, mid, lo


def _lcum_kernel(f_ref, b_ref, lc_ref):
    S = f_ref.shape[1]
    log_f = jax.nn.log_sigmoid(f_ref[0] + b_ref[...])
    log_f_t = log_f.T[:FOX_HEADS]
    n = CUM_CHUNK
    tri = (lax.broadcasted_iota(jnp.int32, (n, n), 0) <= lax.broadcasted_iota(jnp.int32, (n, n), 1)).astype(BF16)
    carry = jnp.zeros((FOX_HEADS, 1), F32)
    for c in range(S // n):
        hi, mid, lo = _split3(log_f_t[:, c * n:(c + 1) * n])
        cs = (_dot(hi, tri) + _dot(mid, tri)) + _dot(lo, tri) + carry
        lc_ref[0, :, c * n:(c + 1) * n] = cs * LOG2E
        carry = cs[:, n - 1:n]


def _lcum(f, b_pad):
    B, S, _ = f.shape
    return pl.pallas_call(
        _lcum_kernel,
        grid=(B,),
        in_specs=[pl.BlockSpec((1, S, LANES), lambda b: (b, 0, 0)), _resident(b_pad.shape)],
        out_specs=pl.BlockSpec((1, FOX_HEADS, S), lambda b: (b, 0, 0)),
        out_shape=jax.ShapeDtypeStruct((B, FOX_HEADS, S), F32),
        compiler_params=_params(1),
        name="lcum",
    )(f, b_pad)


def _final_kernel(x1_ref, o_ref, gate_ref, w_out_ref, g_ref, y_ref):
    a = (o_ref[0].astype(F32) * _silu(gate_ref[0].astype(F32))).astype(BF16)
    x2 = x1_ref[0] + _dot(a, w_out_ref[...])
    y_ref[0] = _rms(x2, g_ref[...])


def _final(x1, o_fox, gate, w_out, g_final):
    B, S, D = x1.shape
    tm = PROJ_ROWS
    row = pl.BlockSpec((1, tm, D), lambda b, i: (b, i, 0))
    return pl.pallas_call(
        _final_kernel,
        grid=(B, S // tm),
        in_specs=[row, row, row, _resident(w_out.shape), _resident(g_final.shape)],
        out_specs=row,
        out_shape=jax.ShapeDtypeStruct((B, S, D), F32),
        compiler_params=_params(2),
        name="final",
    )(x1, o_fox, gate, w_out, g_final)


def _pad_cols(w, left, total):
    return jnp.pad(w, ((0, 0), (left, total - left - w.shape[1])))


def _layer0_weights(w_in, w_q_up, w_kv_up):
    sizes = (MLA_Q_RANK, MLA_KV_RANK, MLA_ROPE, SWA_HEADS * SWA_HEAD_DIM,
             SWA_KV_HEADS * SWA_HEAD_DIM, SWA_KV_HEADS * SWA_HEAD_DIM, D_MODEL)
    w_cq, w_ckv, w_kpe, w_qs, w_ks, w_vs, w_gate = jnp.split(w_in, np.cumsum(sizes)[:-1].tolist(), axis=1)
    half = MLA_ROPE // 2
    w_kpe_sw = jnp.concatenate([w_kpe[:, half:], w_kpe[:, :half]], axis=1)
    w_a = jnp.concatenate([w_cq, w_ckv, _pad_cols(w_kpe, MLA_NOPE, LANES), _pad_cols(w_kpe_sw, MLA_NOPE, LANES)], axis=1)
    d = SWA_HEAD_DIM
    w_kvs = jnp.concatenate([w_ks[:, :d], w_ks[:, :d], w_ks[:, d:], w_ks[:, d:],
                             w_vs[:, :d], w_vs[:, :d], w_vs[:, d:], w_vs[:, d:]], axis=1)
    qk = MLA_NOPE + MLA_ROPE
    wq3 = w_q_up.reshape(MLA_Q_RANK, MLA_HEADS, qk)
    w_q = jnp.pad(wq3, ((0, 0), (0, 0), (0, LANES - qk))).reshape(MLA_Q_RANK, MLA_HEADS * LANES)
    wq_sw = jnp.concatenate([wq3[:, :, MLA_NOPE + half:], wq3[:, :, MLA_NOPE:MLA_NOPE + half]], axis=2)
    w_qsw = jnp.pad(wq_sw, ((0, 0), (0, 0), (MLA_NOPE, LANES - qk))).reshape(MLA_Q_RANK, MLA_HEADS * LANES)
    wkv3 = w_kv_up.reshape(MLA_KV_RANK, MLA_HEADS, MLA_NOPE + MLA_V)
    w_k = jnp.pad(wkv3[:, :, :MLA_NOPE], ((0, 0), (0, 0), (0, LANES - MLA_NOPE))).reshape(MLA_KV_RANK, MLA_HEADS * LANES)
    wv4 = wkv3[:, :, MLA_NOPE:].reshape(MLA_KV_RANK, MLA_HEADS // 2, 2, MLA_V)
    zeros_v = jnp.zeros_like(wv4[:, :, 0])
    w_v = jnp.stack([wv4[:, :, 0], zeros_v, zeros_v, wv4[:, :, 1]], axis=2).reshape(MLA_KV_RANK, MLA_HEADS * LANES)
    swa_scale = SWA_HEAD_DIM ** -0.5
    ws = [w_a, w_qs * swa_scale, w_kvs, w_gate, w_q, w_qsw, w_k, w_v]
    return [w.astype(BF16) for w in ws]


def _rope_table():
    half = MLA_ROPE // 2
    inv_freq = 1.0 / (ROPE_THETA ** (jnp.arange(0, MLA_ROPE, 2, dtype=F32) / MLA_ROPE))
    t = jnp.zeros((8, LANES), F32)
    t = t.at[0, MLA_NOPE:MLA_NOPE + half].set(inv_freq).at[0, MLA_NOPE + half:MLA_NOPE + MLA_ROPE].set(inv_freq)
    t = t.at[1, MLA_NOPE:MLA_NOPE + MLA_ROPE].set(1.0)
    t = t.at[2, MLA_NOPE:MLA_NOPE + half].set(-1.0).at[2, MLA_NOPE + half:MLA_NOPE + MLA_ROPE].set(1.0)
    t = t.at[3, :MLA_NOPE].set(1.0)
    return t


def kernel(x, positions, e_g_in, e_w_in, e_g_q_a, e_w_q_up, e_g_kv_a, e_w_kv_up, e_sinks,
           e_w_out, o_g_in, o_w_in, o_b_f, o_w_out, g_final):
    B, S, D = x.shape
    assert D == D_MODEL and S % PROJ_ROWS == 0 and S % ATTN_TILE == 0 and S % SWA_ROWS == 0
    row = lambda g: g.reshape(1, -1).astype(F32)

    w0 = _layer0_weights(e_w_in[0], e_w_q_up[0], e_w_kv_up[0])
    w_a, w_qs, w_kvs, w_gate, w_q, w_qsw, w_k, w_v = w0
    qm, km, vm, qs, kvs, gate0 = _proj0(
        x, positions.reshape(B, S, 1), row(e_g_in[0]), w_a, w_qs, w_kvs, w_gate,
        row(e_g_q_a[0]), w_q, w_qsw, row(e_g_kv_a[0]), w_k, w_v, _rope_table())
    o_mla = _flash(qm, km, vm, None, q_col0=0, k_col0=0, v_col0=0,
                   n_pairs=MLA_HEADS // 2, head_lanes=LANES)
    o_swa = _swa(e_sinks[0].astype(F32), qs, kvs)

    sizes = (FOX_WIDTH, FOX_WIDTH, FOX_WIDTH, FOX_HEADS, FOX_WIDTH)
    w1_q, w1_k, w1_v, w1_f, w1_gate = jnp.split(o_w_in[0], np.cumsum(sizes)[:-1].tolist(), axis=1)
    fox_scale = FOX_HEAD_DIM ** -0.5
    w1 = jnp.concatenate([w1_q * fox_scale, w1_k, w1_v, w1_gate], axis=1).astype(BF16)
    w1_f_pad = _pad_cols(w1_f, 0, LANES).astype(BF16)
    x1, qk1, v1, gate1, f_logit = _mid(x, o_mla, o_swa, gate0, e_w_out[0].astype(BF16), row(o_g_in[0]), w1, w1_f_pad)

    lc = _lcum(f_logit, _pad_cols(row(o_b_f[0]), 0, LANES))
    lc4 = lc.reshape(B, FOX_HEADS, S // ATTN_TILE, 1, ATTN_TILE)
    o_fox = _flash(qk1, qk1, v1, lc4, q_col0=0, k_col0=FOX_WIDTH // LANES, v_col0=0,
                   n_pairs=FOX_HEADS // 2, head_lanes=HALF)
    return _final(x1, o_fox, gate1, o_w_out[0].astype(BF16), row(g_final))
```

```python
import functools

import jax
import jax.numpy as jnp
import numpy as np
from jax import lax
from jax.experimental import pallas as pl
from jax.experimental.pallas import tpu as pltpu

F32 = jnp.float32
BF16 = jnp.bfloat16

D_MODEL = 1024
EPS = 1e-6
ROPE_THETA = 10000.0
WINDOW = 128

MLA_HEADS = 8
MLA_Q_RANK = 256
MLA_KV_RANK = 128
MLA_NOPE = 64
MLA_ROPE = 32
MLA_V = 64
SWA_HEADS = 8
SWA_KV_HEADS = 2
SWA_HEAD_DIM = 64
FOX_HEADS = 16
FOX_HEAD_DIM = 64
FOX_WIDTH = FOX_HEADS * FOX_HEAD_DIM

LANES = 128
HALF = 64
V7X_VMEM_LIMIT_BYTES = 56 * 1024 * 1024

PROJ_ROWS = 512
FINAL_ROWS = 1024
ATTN_TILE = 512
SWA_ROWS = 512
CUM_CHUNK = 256
LOG2E = 1.4426950408889634


def _dot(a, b):
    return jnp.dot(a, b, preferred_element_type=F32)


def _dot_nt(a, b):
    return lax.dot_general(a, b, (((1,), (1,)), ((), ())), preferred_element_type=F32)


def _rms(xf, g):
    ms = jnp.mean(xf * xf, axis=-1, keepdims=True)
    return xf * lax.rsqrt(ms + EPS) * g


def _silu(x):
    return x * (1.0 / (1.0 + jnp.exp(-x)))


def _params(n_parallel):
    return pltpu.CompilerParams(
        dimension_semantics=("parallel",) * n_parallel,
        vmem_limit_bytes=V7X_VMEM_LIMIT_BYTES,
    )


def _resident(shape):
    nd = len(shape)
    return pl.BlockSpec(shape, lambda *_: (0,) * nd, pipeline_mode=pl.Buffered(1))


def _proj0_kernel(x_ref, pos_ref, g_in_ref, w_a_ref, w_qs_ref, w_kvs_ref, w_gate_ref,
                  g_q_ref, w_q_ref, w_qsw_ref, g_kv_ref, w_k_ref, w_v_ref, rope_ref,
                  qm_ref, km_ref, vm_ref, qs_ref, kvs_ref, gate_ref):
    h = _rms(x_ref[0], g_in_ref[...]).astype(BF16)

    za = _dot(h, w_a_ref[...])
    cqn = _rms(za[:, :MLA_Q_RANK], g_q_ref[...]).astype(BF16)
    ckvn = _rms(za[:, MLA_Q_RANK:MLA_Q_RANK + MLA_KV_RANK], g_kv_ref[...]).astype(BF16)
    kpe = za[:, 384:512]
    kpe_sw = za[:, 512:640]

    ang = pos_ref[0].astype(F32) * rope_ref[0:1, :]
    cos_t = jnp.cos(ang) * rope_ref[1:2, :]
    sin_t = jnp.sin(ang) * rope_ref[2:3, :]
    scale = (MLA_NOPE + MLA_ROPE) ** -0.5 * LOG2E
    q_cos = (cos_t + rope_ref[3:4, :]) * scale
    q_sin = sin_t * scale
    kpe_rot = kpe * cos_t + kpe_sw * sin_t

    for c in range(MLA_HEADS // 2):
        cols = slice(c * 2 * LANES, (c + 1) * 2 * LANES)
        q = _dot(cqn, w_q_ref[:, cols])
        q_sw = _dot(cqn, w_qsw_ref[:, cols])
        kn = _dot(ckvn, w_k_ref[:, cols])
        for hh in range(2):
            lo = hh * LANES
            qm_ref[0, :, c * 2 * LANES + lo:c * 2 * LANES + lo + LANES] = (
                q[:, lo:lo + LANES] * q_cos + q_sw[:, lo:lo + LANES] * q_sin).astype(BF16)
            km_ref[0, :, c * 2 * LANES + lo:c * 2 * LANES + lo + LANES] = (
                kn[:, lo:lo + LANES] + kpe_rot).astype(BF16)
    v = _dot(ckvn, w_v_ref[...])
    lane_v = lax.broadcasted_iota(jnp.int32, v.shape, 1)
    ones_half = (((lane_v >> 6) ^ (lane_v >> 7)) & 1) == 1
    vm_ref[0] = jnp.where(ones_half, 1.0, v).astype(BF16)
    qs_ref[0] = _dot(h, w_qs_ref[...]).astype(BF16)
    kvs_ref[0] = _dot(h, w_kvs_ref[...]).astype(BF16)
    gate_ref[0] = _dot(h, w_gate_ref[...]).astype(BF16)


def _proj0(x, pos3, g_in, w_a, w_qs, w_kvs, w_gate, g_q, w_q, w_qsw, g_kv, w_k, w_v, rope):
    B, S, D = x.shape
    tm = PROJ_ROWS
    row = lambda width: pl.BlockSpec((1, tm, width), lambda b, i: (b, i, 0))
    out_widths = (MLA_HEADS * LANES, MLA_HEADS * LANES, MLA_HEADS * LANES,
                  SWA_HEADS * SWA_HEAD_DIM, 4 * LANES, D)
    return pl.pallas_call(
        _proj0_kernel,
        grid=(B, S // tm),
        in_specs=[row(D), row(1)] + [_resident(a.shape) for a in
                                     (g_in, w_a, w_qs, w_kvs, w_gate, g_q, w_q, w_qsw, g_kv, w_k, w_v, rope)],
        out_specs=[row(w) for w in out_widths],
        out_shape=[jax.ShapeDtypeStruct((B, S, w), BF16) for w in out_widths],
        compiler_params=_params(2),
        name="proj0",
    )(x, pos3, g_in, w_a, w_qs, w_kvs, w_gate, g_q, w_q, w_qsw, g_kv, w_k, w_v, rope)


def _flash_kernel(*refs, head_lanes, has_bias):
    ti_ref, tj_ref, q_ref, k_ref, v_ref = refs[:5]
    lc_ref = refs[5] if has_bias else None
    o_ref = refs[5 + has_bias]
    scr = refs[6 + has_bias:]
    p_scr, alpha_scr = scr[0:2], scr[2:4]
    m_scr, acc_scr, mask_scr = scr[4:]
    T = ATTN_TILE
    nq = q_ref.shape[1] // T
    n_steps = ti_ref.shape[0]
    n_chunks = T // LANES
    lane = lax.broadcasted_iota(jnp.int32, (T, LANES), 1)
    lo_half = lane < HALF
    zero = jnp.zeros((), BF16)

    def rows(i):
        return pl.ds(pl.multiple_of(i * T, T), T)

    H = T // 2
    full_blocks = ((0, T, T),)
    diag_blocks = ((0, H, H), (H, H, T))

    def softmax(t, slot, diag):
        i, j = ti_ref[t], tj_ref[t]
        if head_lanes == HALF:
            q_pair = q_ref[0, rows(i), :]
            q_heads = [jnp.where(lo_half, q_pair, zero), jnp.where(lo_half, zero, q_pair)]
        for hh in range(2):
            for r0, nr, nc in (diag_blocks if diag else full_blocks):
                kr = pl.ds(pl.multiple_of(j * T, T), nc)
                if head_lanes == LANES:
                    q_h = q_ref[0, pl.ds(pl.multiple_of(i * T + r0, H), nr), hh * LANES:(hh + 1) * LANES]
                    k_t = k_ref[0, kr, hh * LANES:(hh + 1) * LANES]
                else:
                    q_h = q_heads[hh][r0:r0 + nr]
                    k_t = k_ref[0, kr, :]
                s = _dot_nt(q_h, k_t)
                if has_bias:
                    s = s + (lc_ref[0, hh, i][:, 0:1] - lc_ref[0, hh, j][:, :nc])
                if diag:
                    tri = s[:, nc - H:] + mask_scr[...]
                    s = tri if nc == H else jnp.concatenate([s[:, :nc - H], tri], axis=1)
                state_rows = pl.ds(pl.multiple_of(i * T + r0, H), nr)
                m_prev = m_scr[hh, state_rows, :]
                m_next = jnp.maximum(m_prev, jnp.max(s, axis=1, keepdims=True))
                alpha_scr[slot][hh, r0:r0 + nr, :] = jnp.exp2(m_prev - m_next)
                p = jnp.exp2(s - jnp.concatenate([m_next] * (nc // LANES), axis=1))
                p_scr[slot][hh, r0:r0 + nr, :nc] = p.astype(BF16)
                m_scr[hh, state_rows, :] = m_next

    def accumulate(t, slot, diag):
        i, j = ti_ref[t], tj_ref[t]
        for hh in range(2):
            for r0, nr, nc in (diag_blocks if diag else full_blocks):
                state_rows = pl.ds(pl.multiple_of(i * T + r0, H), nr)
                v_t = v_ref[0, pl.ds(pl.multiple_of(j * T, T), nc), hh * LANES:(hh + 1) * LANES]
                pv = _dot(p_scr[slot][hh, r0:r0 + nr, :nc], v_t)
                acc_scr[hh, state_rows, :] = alpha_scr[slot][hh, r0:r0 + nr, :] * acc_scr[hh, state_rows, :] + pv

    m_scr[...] = jnp.full(m_scr.shape, -jnp.inf, F32)
    acc_scr[...] = jnp.zeros(acc_scr.shape, F32)
    row_id = lax.broadcasted_iota(jnp.int32, (H, H), 0)
    col_id = lax.broadcasted_iota(jnp.int32, (H, H), 1)
    mask_scr[...] = jnp.where(col_id <= row_id, 0.0, -jnp.inf).astype(F32)

    last_diag = 2 * (nq - 1)

    def is_diag(step):
        return step % 2 == 0 and step <= last_diag

    def step_pair(t, t_static):
        softmax(t, 1, False)
        accumulate(t - 1, 0, is_diag(t_static - 1))
        softmax(t + 1, 0, is_diag(t_static + 1))
        accumulate(t, 1, False)

    def make_body(u_static):
        def body(u, carry):
            step_pair(4 * u + 3, 4 * u_static + 3)
            step_pair(4 * u + 5, 4 * u_static + 5)
            return carry
        return body

    assert (last_diag - 6) % 4 == 0 and (n_steps - 4) % 4 == 0
    n_diag_bodies = (last_diag - 6) // 4 + 1
    softmax(0, 0, True)
    step_pair(1, 1)
    lax.fori_loop(0, n_diag_bodies, make_body(0), 0)
    make_body(n_diag_bodies)(n_diag_bodies, 0)
    lax.fori_loop(n_diag_bodies + 1, (n_steps - 4) // 4, make_body(n_diag_bodies + 1), 0)
    softmax(n_steps - 1, 1, False)
    accumulate(n_steps - 2, 0, False)
    accumulate(n_steps - 1, 1, False)

    def normalize(i, carry):
        r = rows(i)
        acc_even, acc_odd = acc_scr[0, r, :], acc_scr[1, r, :]
        num = jnp.where(lo_half, acc_even, acc_odd)
        den = pltpu.roll(jnp.where(lo_half, acc_odd, acc_even), HALF, 1)
        o_ref[0, r, :] = (num / den).astype(o_ref.dtype)
        return carry

    lax.fori_loop(0, nq, normalize, 0)


def _tile_schedule(nq):
    diag = [(i, i) for i in range(nq)]
    full = [(i, j) for i in range(nq) for j in range(i)]
    n_even = (len(diag) + len(full)) // 2
    assert (len(diag) + len(full)) % 2 == 0 and n_even >= len(diag) and n_even >= 2
    even = diag + full[:n_even - len(diag)]
    odd = full[n_even - len(diag):]
    return np.array([t for pair in zip(even, odd) for t in pair], np.int32)


def _flash(q, k, v, lc, *, q_col0, k_col0, v_col0, n_pairs, head_lanes):
    B, S, _ = q.shape
    T = ATTN_TILE
    qk_w = 2 * head_lanes
    qk_step = qk_w // LANES
    steps = _tile_schedule(S // T)
    smem = pl.BlockSpec(memory_space=pltpu.SMEM)
    in_specs = [
        smem, smem,
        pl.BlockSpec((1, S, qk_w), lambda b, p: (b, 0, q_col0 // qk_step + p)),
        pl.BlockSpec((1, S, qk_w), lambda b, p: (b, 0, k_col0 // qk_step + p)),
        pl.BlockSpec((1, S, 2 * LANES), lambda b, p: (b, 0, v_col0 // 2 + p)),
    ]
    args = [jnp.asarray(steps[:, 0]), jnp.asarray(steps[:, 1]), q, k, v]
    if lc is not None:
        in_specs.append(pl.BlockSpec((1, 2) + lc.shape[2:], lambda b, p: (b, p, 0, 0, 0)))
        args.append(lc)
    per_slot = lambda shape, dtype: [pltpu.VMEM(shape, dtype)] * 2
    scratch = (per_slot((2, T, T), BF16) + per_slot((2, T, LANES), F32)
               + [pltpu.VMEM((2, S, LANES), F32)] * 2
               + [pltpu.VMEM((T // 2, T // 2), F32)])
    return pl.pallas_call(
        functools.partial(_flash_kernel, head_lanes=head_lanes, has_bias=lc is not None),
        grid=(B, n_pairs),
        in_specs=in_specs,
        out_specs=pl.BlockSpec((1, S, LANES), lambda b, p: (b, 0, p)),
        out_shape=jax.ShapeDtypeStruct((B, S, n_pairs * LANES), BF16),
        scratch_shapes=scratch,
        compiler_params=_params(2),
        name="fox" if lc is not None else "mla",
    )(*args)


def _swa_kernel(sinks_ref, bias_ref, qs_ref, kvs_ref, o_ref):
    W = WINDOW
    group = SWA_HEADS // SWA_KV_HEADS
    step = pl.program_id(1)
    lane = lax.broadcasted_iota(jnp.int32, (W, LANES), 1)
    lo_half = lane < HALF
    zero = jnp.zeros((), BF16)

    for blk in range(SWA_ROWS // W):
        t0 = step * SWA_ROWS + blk * W
        k_start = pl.multiple_of(jnp.maximum(t0 - W, 0), W)
        bias_sel = jnp.where(step == 0, 0, 1) if blk == 0 else 1
        rows = slice(blk * W, (blk + 1) * W)
        for g in range(SWA_KV_HEADS):
            k_t = kvs_ref[0, pl.ds(k_start, 2 * W), g * LANES:(g + 1) * LANES]
            v_t = kvs_ref[0, pl.ds(k_start, 2 * W), (2 + g) * LANES:(3 + g) * LANES]
            q_parts = []
            for pair in range(group // 2):
                q_pair = qs_ref[0, rows, (g * 2 + pair) * LANES:(g * 2 + pair + 1) * LANES]
                q_parts.append(jnp.where(lo_half, q_pair, zero))
                q_parts.append(jnp.where(lo_half, zero, q_pair))
            s_all = _dot_nt(jnp.concatenate(q_parts, axis=0), k_t)
            p_parts, denoms = [], []
            for hq in range(group):
                head = g * group + hq
                sink = sinks_ref[head]
                s = s_all[hq * W:(hq + 1) * W] + bias_ref[bias_sel, head]
                m = jnp.maximum(jnp.max(s, axis=1, keepdims=True), sink)
                p = jnp.exp(s - m)
                denoms.append(jnp.sum(p, axis=1, keepdims=True) + jnp.exp(sink - m))
                p_parts.append(p.astype(BF16))
            o_all = _dot(jnp.concatenate(p_parts, axis=0), v_t)
            for pair in range(group // 2):
                o_even = o_all[(2 * pair) * W:(2 * pair + 1) * W] / denoms[2 * pair]
                o_odd = o_all[(2 * pair + 1) * W:(2 * pair + 2) * W] / denoms[2 * pair + 1]
                o_ref[0, rows, (g * 2 + pair) * LANES:(g * 2 + pair + 1) * LANES] = jnp.where(
                    lo_half, o_even, o_odd).astype(o_ref.dtype)


def _swa_bias():
    W = WINDOW
    slopes = 2.0 ** (-8.0 * (np.arange(SWA_HEADS, dtype=np.float32) + 1.0) / SWA_HEADS)
    a = np.arange(W, dtype=np.float32)[:, None]
    c = np.arange(2 * W, dtype=np.float32)[None, :]
    tables = []
    for k_offset in (0.0, float(W)):
        dist = k_offset + a - c
        bias = -slopes.astype(np.float32)[:, None, None] * dist[None]
        tables.append(np.where((dist >= 0) & (dist < W), bias, -np.inf).astype(np.float32))
    return jnp.asarray(np.stack(tables))


def _swa(sinks, qs, kvs):
    B, S, C = qs.shape
    bias = _swa_bias()
    return pl.pallas_call(
        _swa_kernel,
        grid=(B, S // SWA_ROWS),
        in_specs=[pl.BlockSpec(memory_space=pltpu.SMEM), _resident(bias.shape),
                  pl.BlockSpec((1, SWA_ROWS, C), lambda b, i: (b, i, 0)),
                  pl.BlockSpec((1, S, kvs.shape[2]), lambda b, i: (b, 0, 0))],
        out_specs=pl.BlockSpec((1, SWA_ROWS, C), lambda b, i: (b, i, 0)),
        out_shape=jax.ShapeDtypeStruct((B, S, C), BF16),
        compiler_params=_params(2),
        name="swa",
    )(sinks, bias, qs, kvs)


def _mid_kernel(x_ref, om_ref, os_ref, gate_ref, w_out_ref, g_in_ref, w_in_ref, w_f_ref,
                x1_ref, qk_ref, v_ref, g1_ref, f_ref):
    half = D_MODEL // 2
    gate = _silu(gate_ref[0].astype(F32))
    a_m = (om_ref[0].astype(F32) * gate[:, :half]).astype(BF16)
    a_s = (os_ref[0].astype(F32) * gate[:, half:]).astype(BF16)
    x1 = x_ref[0] + (_dot(a_m, w_out_ref[:half, :]) + _dot(a_s, w_out_ref[half:, :]))
    x1_ref[0] = x1
    h = _rms(x1, g_in_ref[...]).astype(BF16)
    w = FOX_WIDTH
    qk_ref[0, :, :w] = (_dot(h, w_in_ref[:, :w]) * LOG2E).astype(BF16)
    qk_ref[0, :, w:] = _dot(h, w_in_ref[:, w:2 * w]).astype(BF16)
    g1_ref[0] = _dot(h, w_in_ref[:, 3 * w:]).astype(BF16)
    f_ref[0] = _dot(h, w_f_ref[...])
    v = _dot(h, w_in_ref[:, 2 * w:3 * w])
    lo_half = lax.broadcasted_iota(jnp.int32, (v.shape[0], LANES), 1) < HALF
    for pair in range(FOX_HEADS // 2):
        v_pair = v[:, pair * LANES:(pair + 1) * LANES]
        v_ref[0, :, (2 * pair) * LANES:(2 * pair + 1) * LANES] = jnp.where(lo_half, v_pair, 1.0).astype(BF16)
        v_ref[0, :, (2 * pair + 1) * LANES:(2 * pair + 2) * LANES] = jnp.where(lo_half, 1.0, v_pair).astype(BF16)


def _mid(x, o_mla, o_swa, gate, w_out, g_in, w_in, w_f):
    B, S, D = x.shape
    tm = PROJ_ROWS
    row = lambda width: pl.BlockSpec((1, tm, width), lambda b, i: (b, i, 0))
    outs = [(D, F32), (2 * FOX_WIDTH, BF16), (FOX_HEADS * LANES, BF16), (FOX_WIDTH, BF16), (LANES, F32)]
    return pl.pallas_call(
        _mid_kernel,
        grid=(B, S // tm),
        in_specs=[row(D), row(D // 2), row(D // 2), row(D)] + [_resident(a.shape) for a in (w_out, g_in, w_in, w_f)],
        out_specs=[row(width) for width, _ in outs],
        out_shape=[jax.ShapeDtypeStruct((B, S, width), dtype) for width, dtype in outs],
        compiler_params=_params(2),
        name="mid",
    )(x, o_mla, o_swa, gate, w_out, g_in, w_in, w_f)


def _split3(x):
    hi = x.astype(BF16)
    r1 = x - hi.astype(F32)
    mid = r1.astype(BF16)
    lo = (r1 - mid.astype(F32)).astype(BF16)
    return hi, mid, lo


def _lcum_kernel(f_ref, b_ref, lc_ref):
    S = f_ref.shape[1]
    log_f = jax.nn.log_sigmoid(f_ref[0] + b_ref[...])
    log_f_t = log_f.T[:FOX_HEADS]
    n = CUM_CHUNK
    tri = (lax.broadcasted_iota(jnp.int32, (n, n), 0) <= lax.broadcasted_iota(jnp.int32, (n, n), 1)).astype(BF16)
    carry = jnp.zeros((FOX_HEADS, 1), F32)
    for c in range(S // n):
        hi, mid, lo = _split3(log_f_t[:, c * n:(c + 1) * n])
        cs = (_dot(hi, tri) + _dot(mid, tri)) + _dot(lo, tri) + carry
        lc_ref[0, :, c * n:(c + 1) * n] = cs * LOG2E
        carry = cs[:, n - 1:n]


def _lcum(f, b_pad):
    B, S, _ = f.shape
    return pl.pallas_call(
        _lcum_kernel,
        grid=(B,),
        in_specs=[pl.BlockSpec((1, S, LANES), lambda b: (b, 0, 0)), _resident(b_pad.shape)],
        out_specs=pl.BlockSpec((1, FOX_HEADS, S), lambda b: (b, 0, 0)),
        out_shape=jax.ShapeDtypeStruct((B, FOX_HEADS, S), F32),
        compiler_params=_params(1),
        name="lcum",
    )(f, b_pad)


def _final_kernel(x1_ref, o_ref, gate_ref, w_out_ref, g_ref, y_ref):
    a = (o_ref[0].astype(F32) * _silu(gate_ref[0].astype(F32))).astype(BF16)
    x2 = x1_ref[0] + _dot(a, w_out_ref[...])
    y_ref[0] = _rms(x2, g_ref[...])


def _final(x1, o_fox, gate, w_out, g_final):
    B, S, D = x1.shape
    tm = FINAL_ROWS
    row = pl.BlockSpec((1, tm, D), lambda b, i: (b, i, 0))
    return pl.pallas_call(
        _final_kernel,
        grid=(B, S // tm),
        in_specs=[row, row, row, _resident(w_out.shape), _resident(g_final.shape)],
        out_specs=row,
        out_shape=jax.ShapeDtypeStruct((B, S, D), F32),
        compiler_params=_params(2),
        name="final",
    )(x1, o_fox, gate, w_out, g_final)


def _pad_cols(w, left, total):
    return jnp.pad(w, ((0, 0), (left, total - left - w.shape[1])))


def _pad_rows(w, top, total):
    return jnp.pad(w, ((top, total - top - w.shape[0]), (0, 0)))


def _layer0_weights(w_in, w_q_up, w_kv_up):
    sizes = (MLA_Q_RANK, MLA_KV_RANK, MLA_ROPE, SWA_HEADS * SWA_HEAD_DIM,
             SWA_KV_HEADS * SWA_HEAD_DIM, SWA_KV_HEADS * SWA_HEAD_DIM, D_MODEL)
    cq, ckv, kpe, qs, ks, vs, gate = jnp.split(w_in.T, np.cumsum(sizes)[:-1].tolist(), axis=0)
    half = MLA_ROPE // 2
    kpe_sw = jnp.concatenate([kpe[half:], kpe[:half]], axis=0)
    a_t = jnp.concatenate([cq, ckv, _pad_rows(kpe, MLA_NOPE, LANES), _pad_rows(kpe_sw, MLA_NOPE, LANES)], axis=0)
    d = SWA_HEAD_DIM
    kvs_t = jnp.concatenate([ks[:d], ks[:d], ks[d:], ks[d:], vs[:d], vs[:d], vs[d:], vs[d:]], axis=0)
    swa_scale = SWA_HEAD_DIM ** -0.5
    w_a, w_qs, w_kvs, w_gate = [w_t.astype(BF16).T for w_t in (a_t, qs * swa_scale, kvs_t, gate)]
    qk = MLA_NOPE + MLA_ROPE
    wq3 = w_q_up.reshape(MLA_Q_RANK, MLA_HEADS, qk)
    w_q = jnp.pad(wq3, ((0, 0), (0, 0), (0, LANES - qk))).reshape(MLA_Q_RANK, MLA_HEADS * LANES)
    wq_sw = jnp.concatenate([wq3[:, :, MLA_NOPE + half:], wq3[:, :, MLA_NOPE:MLA_NOPE + half]], axis=2)
    w_qsw = jnp.pad(wq_sw, ((0, 0), (0, 0), (MLA_NOPE, LANES - qk))).reshape(MLA_Q_RANK, MLA_HEADS * LANES)
    wkv3 = w_kv_up.reshape(MLA_KV_RANK, MLA_HEADS, MLA_NOPE + MLA_V)
    w_k = jnp.pad(wkv3[:, :, :MLA_NOPE], ((0, 0), (0, 0), (0, LANES - MLA_NOPE))).reshape(MLA_KV_RANK, MLA_HEADS * LANES)
    wv4 = wkv3[:, :, MLA_NOPE:].reshape(MLA_KV_RANK, MLA_HEADS // 2, 2, MLA_V)
    zeros_v = jnp.zeros_like(wv4[:, :, 0])
    w_v = jnp.stack([wv4[:, :, 0], zeros_v, zeros_v, wv4[:, :, 1]], axis=2).reshape(MLA_KV_RANK, MLA_HEADS * LANES)
    return [w_a, w_qs, w_kvs, w_gate] + [w.astype(BF16) for w in (w_q, w_qsw, w_k, w_v)]


def _rope_table():
    half = MLA_ROPE // 2
    inv_freq = 1.0 / (ROPE_THETA ** (jnp.arange(0, MLA_ROPE, 2, dtype=F32) / MLA_ROPE))
    t = jnp.zeros((8, LANES), F32)
    t = t.at[0, MLA_NOPE:MLA_NOPE + half].set(inv_freq).at[0, MLA_NOPE + half:MLA_NOPE + MLA_ROPE].set(inv_freq)
    t = t.at[1, MLA_NOPE:MLA_NOPE + MLA_ROPE].set(1.0)
    t = t.at[2, MLA_NOPE:MLA_NOPE + half].set(-1.0).at[2, MLA_NOPE + half:MLA_NOPE + MLA_ROPE].set(1.0)
    t = t.at[3, :MLA_NOPE].set(1.0)
    return t


def kernel(x, positions, e_g_in, e_w_in, e_g_q_a, e_w_q_up, e_g_kv_a, e_w_kv_up, e_sinks,
           e_w_out, o_g_in, o_w_in, o_b_f, o_w_out, g_final):
    B, S, D = x.shape
    assert D == D_MODEL and all(S % rows == 0 for rows in (PROJ_ROWS, FINAL_ROWS, ATTN_TILE, SWA_ROWS))
    row = lambda g: g.reshape(1, -1).astype(F32)

    w0 = _layer0_weights(e_w_in[0], e_w_q_up[0], e_w_kv_up[0])
    w_a, w_qs, w_kvs, w_gate, w_q, w_qsw, w_k, w_v = w0
    qm, km, vm, qs, kvs, gate0 = _proj0(
        x, positions.reshape(B, S, 1), row(e_g_in[0]), w_a, w_qs, w_kvs, w_gate,
        row(e_g_q_a[0]), w_q, w_qsw, row(e_g_kv_a[0]), w_k, w_v, _rope_table())
    o_mla = _flash(qm, km, vm, None, q_col0=0, k_col0=0, v_col0=0,
                   n_pairs=MLA_HEADS // 2, head_lanes=LANES)
    o_swa = _swa(e_sinks[0].astype(F32), qs, kvs)

    sizes = (FOX_WIDTH, FOX_WIDTH, FOX_WIDTH, FOX_HEADS, FOX_WIDTH)
    w1_q, w1_k, w1_v, w1_f, w1_gate = jnp.split(o_w_in[0].T, np.cumsum(sizes)[:-1].tolist(), axis=0)
    fox_scale = FOX_HEAD_DIM ** -0.5
    w1 = jnp.concatenate([w1_q * fox_scale, w1_k, w1_v, w1_gate], axis=0).astype(BF16).T
    w1_f_pad = _pad_rows(w1_f, 0, LANES).astype(BF16).T
    x1, qk1, v1, gate1, f_logit = _mid(x, o_mla, o_swa, gate0, e_w_out[0].astype(BF16), row(o_g_in[0]), w1, w1_f_pad)

    lc = _lcum(f_logit, _pad_cols(row(o_b_f[0]), 0, LANES))
    lc4 = lc.reshape(B, FOX_HEADS, S // ATTN_TILE, 1, ATTN_TILE)
    o_fox = _flash(qk1, qk1, v1, lc4, q_col0=0, k_col0=FOX_WIDTH // LANES, v_col0=0,
                   n_pairs=FOX_HEADS // 2, head_lanes=HALF)
    return _final(x1, o_fox, gate1, o_w_out[0].astype(BF16), row(g_final))
```

```python
import functools

import jax
import jax.numpy as jnp
import numpy as np
from jax import lax
from jax.experimental import pallas as pl
from jax.experimental.pallas import tpu as pltpu

F32 = jnp.float32
BF16 = jnp.bfloat16

D_MODEL = 1024
EPS = 1e-6
ROPE_THETA = 10000.0
WINDOW = 128

MLA_HEADS = 8
MLA_Q_RANK = 256
MLA_KV_RANK = 128
MLA_NOPE = 64
MLA_ROPE = 32
MLA_V = 64
SWA_HEADS = 8
SWA_KV_HEADS = 2
SWA_HEAD_DIM = 64
FOX_HEADS = 16
FOX_HEAD_DIM = 64
FOX_WIDTH = FOX_HEADS * FOX_HEAD_DIM

LANES = 128
HALF = 64
V7X_VMEM_LIMIT_BYTES = 56 * 1024 * 1024

PROJ_ROWS = 512
FINAL_ROWS = 1024
ATTN_TILE = 512
SWA_ROWS = 512
CUM_CHUNK = 256
LOG2E = 1.4426950408889634
EXP2_UNDERFLOW = 152.0
NORM_SLACK = 1.01
BOUND_SLACK = 2.0


def _dot(a, b):
    return jnp.dot(a, b, preferred_element_type=F32)


def _dot_nt(a, b):
    return lax.dot_general(a, b, (((1,), (1,)), ((), ())), preferred_element_type=F32)


def _rms(xf, g):
    ms = jnp.mean(xf * xf, axis=-1, keepdims=True)
    return xf * lax.rsqrt(ms + EPS) * g


def _silu(x):
    return x * (1.0 / (1.0 + jnp.exp(-x)))


def _params(n_parallel):
    return pltpu.CompilerParams(
        dimension_semantics=("parallel",) * n_parallel,
        vmem_limit_bytes=V7X_VMEM_LIMIT_BYTES,
    )


def _resident(shape):
    nd = len(shape)
    return pl.BlockSpec(shape, lambda *_: (0,) * nd, pipeline_mode=pl.Buffered(1))


def _proj0_kernel(x_ref, pos_ref, g_in_ref, w_a_ref, w_qs_ref, w_kvs_ref, w_gate_ref,
                  g_q_ref, w_q_ref, w_qsw_ref, g_kv_ref, w_k_ref, w_v_ref, rope_ref,
                  qm_ref, km_ref, vm_ref, qs_ref, kvs_ref, gate_ref):
    h = _rms(x_ref[0], g_in_ref[...]).astype(BF16)

    za = _dot(h, w_a_ref[...])
    cqn = _rms(za[:, :MLA_Q_RANK], g_q_ref[...]).astype(BF16)
    ckvn = _rms(za[:, MLA_Q_RANK:MLA_Q_RANK + MLA_KV_RANK], g_kv_ref[...]).astype(BF16)
    kpe = za[:, 384:512]
    kpe_sw = za[:, 512:640]

    ang = pos_ref[0].astype(F32) * rope_ref[0:1, :]
    cos_t = jnp.cos(ang) * rope_ref[1:2, :]
    sin_t = jnp.sin(ang) * rope_ref[2:3, :]
    scale = (MLA_NOPE + MLA_ROPE) ** -0.5 * LOG2E
    q_cos = (cos_t + rope_ref[3:4, :]) * scale
    q_sin = sin_t * scale
    kpe_rot = kpe * cos_t + kpe_sw * sin_t

    for c in range(MLA_HEADS // 2):
        cols = slice(c * 2 * LANES, (c + 1) * 2 * LANES)
        q = _dot(cqn, w_q_ref[:, cols])
        q_sw = _dot(cqn, w_qsw_ref[:, cols])
        kn = _dot(ckvn, w_k_ref[:, cols])
        for hh in range(2):
            lo = hh * LANES
            qm_ref[0, :, c * 2 * LANES + lo:c * 2 * LANES + lo + LANES] = (
                q[:, lo:lo + LANES] * q_cos + q_sw[:, lo:lo + LANES] * q_sin).astype(BF16)
            km_ref[0, :, c * 2 * LANES + lo:c * 2 * LANES + lo + LANES] = (
                kn[:, lo:lo + LANES] + kpe_rot).astype(BF16)
    v = _dot(ckvn, w_v_ref[...])
    lane_v = lax.broadcasted_iota(jnp.int32, v.shape, 1)
    ones_half = (((lane_v >> 6) ^ (lane_v >> 7)) & 1) == 1
    vm_ref[0] = jnp.where(ones_half, 1.0, v).astype(BF16)
    qs_ref[0] = _dot(h, w_qs_ref[...]).astype(BF16)
    kvs_ref[0] = _dot(h, w_kvs_ref[...]).astype(BF16)
    gate_ref[0] = _dot(h, w_gate_ref[...]).astype(BF16)


def _proj0(x, pos3, g_in, w_a, w_qs, w_kvs, w_gate, g_q, w_q, w_qsw, g_kv, w_k, w_v, rope):
    B, S, D = x.shape
    tm = PROJ_ROWS
    row = lambda width: pl.BlockSpec((1, tm, width), lambda b, i: (b, i, 0))
    out_widths = (MLA_HEADS * LANES, MLA_HEADS * LANES, MLA_HEADS * LANES,
                  SWA_HEADS * SWA_HEAD_DIM, 4 * LANES, D)
    return pl.pallas_call(
        _proj0_kernel,
        grid=(B, S // tm),
        in_specs=[row(D), row(1)] + [_resident(a.shape) for a in
                                     (g_in, w_a, w_qs, w_kvs, w_gate, g_q, w_q, w_qsw, g_kv, w_k, w_v, rope)],
        out_specs=[row(w) for w in out_widths],
        out_shape=[jax.ShapeDtypeStruct((B, S, w), BF16) for w in out_widths],
        compiler_params=_params(2),
        name="proj0",
    )(x, pos3, g_in, w_a, w_qs, w_kvs, w_gate, g_q, w_q, w_qsw, g_kv, w_k, w_v, rope)


def _fox_schedule(q_ref, k_ref, lc_ref, ti_ref, tj_ref, nq):
    T = ATTN_TILE
    d_id = lax.broadcasted_iota(jnp.int32, (LANES, LANES), 0)
    c_id = lax.broadcasted_iota(jnp.int32, (LANES, LANES), 1)
    head_cols = (((c_id == 0) & (d_id < HALF)) | ((c_id == 1) & (d_id >= HALF))).astype(BF16)
    q_all, k_all = q_ref[0], k_ref[0]
    qq = _dot(q_all * q_all, head_cols)
    kk = _dot(k_all * k_all, head_cols)
    qk = _dot(q_all * k_all, head_cols)
    k_max = jnp.max(kk, axis=0, keepdims=True)
    j_lo = [0]
    for i in range(1, nq):
        r = slice(i * T, (i + 1) * T)
        u = (NORM_SLACK * jnp.sqrt(jnp.max(qq[r], axis=0, keepdims=True) * k_max)
             - jnp.min(qk[r], axis=0, keepdims=True) + BOUND_SLACK)
        count = jnp.zeros((1, 1), jnp.int32)
        for j in range(i):
            skip = None
            for hh in range(2):
                bound = u[:, hh:hh + 1] + lc_ref[0, hh, i][:, 0:1] - lc_ref[0, hh, j][:, T - 1:T]
                skip_h = bound < -EXP2_UNDERFLOW
                skip = skip_h if skip is None else skip & skip_h
            count = count + skip.astype(jnp.int32)
        j_lo.append(count[0, 0])

    for i in range(nq):
        ti_ref[2 * i] = jnp.int32(i)
        tj_ref[2 * i] = jnp.int32(i)

    def slot_of(pos):
        return jnp.where(pos < nq, 2 * pos + 1, pos + nq)

    pos = jnp.int32(0)
    pad_i = pad_j = jnp.int32(0)
    for i in range(1, nq):
        for j in range(i):
            keep = j >= j_lo[i]

            @pl.when(keep)
            def _(i=i, j=j, pos=pos):
                ti_ref[slot_of(pos)] = jnp.int32(i)
                tj_ref[slot_of(pos)] = jnp.int32(j)

            pos = pos + keep.astype(jnp.int32)
            pad_i = jnp.where(keep, pad_i, i)
            pad_j = jnp.where(keep, pad_j, j)
    assert nq % 4 == 0
    min_quads = nq // 4 + 1
    quads = jnp.maximum((pos + 3) >> 2, min_quads)
    for extra in range(4 * min_quads):
        @pl.when(pos + extra < 4 * quads)
        def _(extra=extra):
            ti_ref[slot_of(pos + extra)] = pad_i
            tj_ref[slot_of(pos + extra)] = pad_j
    return quads + 1


def _flash_kernel(*refs, head_lanes, has_bias):
    T = ATTN_TILE
    if has_bias:
        q_ref, k_ref, v_ref, lc_ref, o_ref = refs[:5]
        scr = refs[5:-2]
        ti_ref, tj_ref = refs[-2:]
        nq = q_ref.shape[1] // T
        n_bodies = _fox_schedule(q_ref, k_ref, lc_ref, ti_ref, tj_ref, nq)
        last_step = nq + 4 * (n_bodies - 1) - 1
    else:
        ti_ref, tj_ref, q_ref, k_ref, v_ref, o_ref = refs[:6]
        lc_ref = None
        scr = refs[6:]
        nq = q_ref.shape[1] // T
        assert (ti_ref.shape[0] - 4) % 4 == 0
        n_bodies = (ti_ref.shape[0] - 4) // 4
        last_step = ti_ref.shape[0] - 1
    p_scr, alpha_scr = scr[0:2], scr[2:4]
    m_scr, acc_scr, mask_scr = scr[4:]
    lane = lax.broadcasted_iota(jnp.int32, (T, LANES), 1)
    lo_half = lane < HALF
    zero = jnp.zeros((), BF16)

    def rows(i):
        return pl.ds(pl.multiple_of(i * T, T), T)

    H = T // 2
    full_blocks = ((0, T, T),)
    diag_blocks = ((0, H, H), (H, H, T))

    def softmax(t, slot, diag):
        i, j = ti_ref[t], tj_ref[t]
        if head_lanes == HALF:
            q_pair = q_ref[0, rows(i), :]
            q_heads = [jnp.where(lo_half, q_pair, zero), jnp.where(lo_half, zero, q_pair)]
        for hh in range(2):
            for r0, nr, nc in (diag_blocks if diag else full_blocks):
                kr = pl.ds(pl.multiple_of(j * T, T), nc)
                if head_lanes == LANES:
                    q_h = q_ref[0, pl.ds(pl.multiple_of(i * T + r0, H), nr), hh * LANES:(hh + 1) * LANES]
                    k_t = k_ref[0, kr, hh * LANES:(hh + 1) * LANES]
                else:
                    q_h = q_heads[hh][r0:r0 + nr]
                    k_t = k_ref[0, kr, :]
                s = _dot_nt(q_h, k_t)
                if has_bias:
                    s = s + (lc_ref[0, hh, i][:, 0:1] - lc_ref[0, hh, j][:, :nc])
                if diag:
                    tri = s[:, nc - H:] + mask_scr[...]
                    s = tri if nc == H else jnp.concatenate([s[:, :nc - H], tri], axis=1)
                state_rows = pl.ds(pl.multiple_of(i * T + r0, H), nr)
                m_prev = m_scr[hh, state_rows, :]
                m_next = jnp.maximum(m_prev, jnp.max(s, axis=1, keepdims=True))
                alpha_scr[slot][hh, r0:r0 + nr, :] = jnp.exp2(m_prev - m_next)
                p = jnp.exp2(s - jnp.concatenate([m_next] * (nc // LANES), axis=1))
                p_scr[slot][hh, r0:r0 + nr, :nc] = p.astype(BF16)
                m_scr[hh, state_rows, :] = m_next

    def accumulate(t, slot, diag):
        i, j = ti_ref[t], tj_ref[t]
        for hh in range(2):
            for r0, nr, nc in (diag_blocks if diag else full_blocks):
                state_rows = pl.ds(pl.multiple_of(i * T + r0, H), nr)
                v_t = v_ref[0, pl.ds(pl.multiple_of(j * T, T), nc), hh * LANES:(hh + 1) * LANES]
                pv = _dot(p_scr[slot][hh, r0:r0 + nr, :nc], v_t)
                acc_scr[hh, state_rows, :] = alpha_scr[slot][hh, r0:r0 + nr, :] * acc_scr[hh, state_rows, :] + pv

    m_scr[...] = jnp.full(m_scr.shape, -jnp.inf, F32)
    acc_scr[...] = jnp.zeros(acc_scr.shape, F32)
    row_id = lax.broadcasted_iota(jnp.int32, (H, H), 0)
    col_id = lax.broadcasted_iota(jnp.int32, (H, H), 1)
    mask_scr[...] = jnp.where(col_id <= row_id, 0.0, -jnp.inf).astype(F32)

    last_diag = 2 * (nq - 1)

    def is_diag(step):
        return step % 2 == 0 and step <= last_diag

    def step_pair(t, t_static):
        softmax(t, 1, False)
        accumulate(t - 1, 0, is_diag(t_static - 1))
        softmax(t + 1, 0, is_diag(t_static + 1))
        accumulate(t, 1, False)

    def make_body(u_static):
        def body(u, carry):
            step_pair(4 * u + 3, 4 * u_static + 3)
            step_pair(4 * u + 5, 4 * u_static + 5)
            return carry
        return body

    assert (last_diag - 6) % 4 == 0
    n_diag_bodies = (last_diag - 6) // 4 + 1
    softmax(0, 0, True)
    step_pair(1, 1)
    lax.fori_loop(0, n_diag_bodies, make_body(0), 0)
    make_body(n_diag_bodies)(n_diag_bodies, 0)
    lax.fori_loop(n_diag_bodies + 1, n_bodies, make_body(n_diag_bodies + 1), 0)
    softmax(last_step, 1, False)
    accumulate(last_step - 1, 0, False)
    accumulate(last_step, 1, False)

    def normalize(i, carry):
        r = rows(i)
        acc_even, acc_odd = acc_scr[0, r, :], acc_scr[1, r, :]
        num = jnp.where(lo_half, acc_even, acc_odd)
        den = pltpu.roll(jnp.where(lo_half, acc_odd, acc_even), HALF, 1)
        o_ref[0, r, :] = (num / den).astype(o_ref.dtype)
        return carry

    lax.fori_loop(0, nq, normalize, 0)


def _tile_schedule(nq):
    diag = [(i, i) for i in range(nq)]
    full = [(i, j) for i in range(nq) for j in range(i)]
    n_even = (len(diag) + len(full)) // 2
    assert (len(diag) + len(full)) % 2 == 0 and n_even >= len(diag) and n_even >= 2
    even = diag + full[:n_even - len(diag)]
    odd = full[n_even - len(diag):]
    return np.array([t for pair in zip(even, odd) for t in pair], np.int32)


def _flash(q, k, v, lc, *, q_col0, k_col0, v_col0, n_pairs, head_lanes):
    B, S, _ = q.shape
    T = ATTN_TILE
    qk_w = 2 * head_lanes
    qk_step = qk_w // LANES
    steps = _tile_schedule(S // T)
    in_specs = [
        pl.BlockSpec((1, S, qk_w), lambda b, p: (b, 0, q_col0 // qk_step + p)),
        pl.BlockSpec((1, S, qk_w), lambda b, p: (b, 0, k_col0 // qk_step + p)),
        pl.BlockSpec((1, S, 2 * LANES), lambda b, p: (b, 0, v_col0 // 2 + p)),
    ]
    args = [q, k, v]
    per_slot = lambda shape, dtype: [pltpu.VMEM(shape, dtype)] * 2
    scratch = (per_slot((2, T, T), BF16) + per_slot((2, T, LANES), F32)
               + [pltpu.VMEM((2, S, LANES), F32)] * 2
               + [pltpu.VMEM((T // 2, T // 2), F32)])
    if lc is not None:
        in_specs.append(pl.BlockSpec((1, 2) + lc.shape[2:], lambda b, p: (b, p, 0, 0, 0)))
        args.append(lc)
        scratch += [pltpu.SMEM((len(steps),), jnp.int32)] * 2
    else:
        smem = pl.BlockSpec(memory_space=pltpu.SMEM)
        in_specs = [smem, smem] + in_specs
        args = [jnp.asarray(steps[:, 0]), jnp.asarray(steps[:, 1])] + args
    return pl.pallas_call(
        functools.partial(_flash_kernel, head_lanes=head_lanes, has_bias=lc is not None),
        grid=(B, n_pairs),
        in_specs=in_specs,
        out_specs=pl.BlockSpec((1, S, LANES), lambda b, p: (b, 0, p)),
        out_shape=jax.ShapeDtypeStruct((B, S, n_pairs * LANES), BF16),
        scratch_shapes=scratch,
        compiler_params=_params(2),
        name="fox" if lc is not None else "mla",
    )(*args)


def _swa_kernel(sinks_ref, bias_ref, qs_ref, kvs_ref, o_ref):
    W = WINDOW
    group = SWA_HEADS // SWA_KV_HEADS
    step = pl.program_id(1)
    lane = lax.broadcasted_iota(jnp.int32, (W, LANES), 1)
    lo_half = lane < HALF
    zero = jnp.zeros((), BF16)

    for blk in range(SWA_ROWS // W):
        t0 = step * SWA_ROWS + blk * W
        k_start = pl.multiple_of(jnp.maximum(t0 - W, 0), W)
        bias_sel = jnp.where(step == 0, 0, 1) if blk == 0 else 1
        rows = slice(blk * W, (blk + 1) * W)
        for g in range(SWA_KV_HEADS):
            k_t = kvs_ref[0, pl.ds(k_start, 2 * W), g * LANES:(g + 1) * LANES]
            v_t = kvs_ref[0, pl.ds(k_start, 2 * W), (2 + g) * LANES:(3 + g) * LANES]
            q_parts = []
            for pair in range(group // 2):
                q_pair = qs_ref[0, rows, (g * 2 + pair) * LANES:(g * 2 + pair + 1) * LANES]
                q_parts.append(jnp.where(lo_half, q_pair, zero))
                q_parts.append(jnp.where(lo_half, zero, q_pair))
            s_all = _dot_nt(jnp.concatenate(q_parts, axis=0), k_t)
            p_parts, denoms = [], []
            for hq in range(group):
                head = g * group + hq
                sink = sinks_ref[head]
                s = s_all[hq * W:(hq + 1) * W] + bias_ref[bias_sel, head]
                m = jnp.maximum(jnp.max(s, axis=1, keepdims=True), sink)
                p = jnp.exp(s - m)
                denoms.append(jnp.sum(p, axis=1, keepdims=True) + jnp.exp(sink - m))
                p_parts.append(p.astype(BF16))
            o_all = _dot(jnp.concatenate(p_parts, axis=0), v_t)
            for pair in range(group // 2):
                o_even = o_all[(2 * pair) * W:(2 * pair + 1) * W] / denoms[2 * pair]
                o_odd = o_all[(2 * pair + 1) * W:(2 * pair + 2) * W] / denoms[2 * pair + 1]
                o_ref[0, rows, (g * 2 + pair) * LANES:(g * 2 + pair + 1) * LANES] = jnp.where(
                    lo_half, o_even, o_odd).astype(o_ref.dtype)


def _swa_bias():
    W = WINDOW
    slopes = 2.0 ** (-8.0 * (np.arange(SWA_HEADS, dtype=np.float32) + 1.0) / SWA_HEADS)
    a = np.arange(W, dtype=np.float32)[:, None]
    c = np.arange(2 * W, dtype=np.float32)[None, :]
    tables = []
    for k_offset in (0.0, float(W)):
        dist = k_offset + a - c
        bias = -slopes.astype(np.float32)[:, None, None] * dist[None]
        tables.append(np.where((dist >= 0) & (dist < W), bias, -np.inf).astype(np.float32))
    return jnp.asarray(np.stack(tables))


def _swa(sinks, qs, kvs):
    B, S, C = qs.shape
    bias = _swa_bias()
    return pl.pallas_call(
        _swa_kernel,
        grid=(B, S // SWA_ROWS),
        in_specs=[pl.BlockSpec(memory_space=pltpu.SMEM), _resident(bias.shape),
                  pl.BlockSpec((1, SWA_ROWS, C), lambda b, i: (b, i, 0)),
                  pl.BlockSpec((1, S, kvs.shape[2]), lambda b, i: (b, 0, 0))],
        out_specs=pl.BlockSpec((1, SWA_ROWS, C), lambda b, i: (b, i, 0)),
        out_shape=jax.ShapeDtypeStruct((B, S, C), BF16),
        compiler_params=_params(2),
        name="swa",
    )(sinks, bias, qs, kvs)


def _mid_kernel(x_ref, om_ref, os_ref, gate_ref, w_out_ref, g_in_ref, w_in_ref, w_f_ref,
                x1_ref, qk_ref, v_ref, g1_ref, f_ref):
    half = D_MODEL // 2
    gate = _silu(gate_ref[0].astype(F32))
    a_m = (om_ref[0].astype(F32) * gate[:, :half]).astype(BF16)
    a_s = (os_ref[0].astype(F32) * gate[:, half:]).astype(BF16)
    x1 = x_ref[0] + (_dot(a_m, w_out_ref[:half, :]) + _dot(a_s, w_out_ref[half:, :]))
    x1_ref[0] = x1
    h = _rms(x1, g_in_ref[...]).astype(BF16)
    w = FOX_WIDTH
    qk_ref[0, :, :w] = (_dot(h, w_in_ref[:, :w]) * LOG2E).astype(BF16)
    qk_ref[0, :, w:] = _dot(h, w_in_ref[:, w:2 * w]).astype(BF16)
    g1_ref[0] = _dot(h, w_in_ref[:, 3 * w:]).astype(BF16)
    f_ref[0] = _dot(h, w_f_ref[...])
    v = _dot(h, w_in_ref[:, 2 * w:3 * w])
    lo_half = lax.broadcasted_iota(jnp.int32, (v.shape[0], LANES), 1) < HALF
    for pair in range(FOX_HEADS // 2):
        v_pair = v[:, pair * LANES:(pair + 1) * LANES]
        v_ref[0, :, (2 * pair) * LANES:(2 * pair + 1) * LANES] = jnp.where(lo_half, v_pair, 1.0).astype(BF16)
        v_ref[0, :, (2 * pair + 1) * LANES:(2 * pair + 2) * LANES] = jnp.where(lo_half, 1.0, v_pair).astype(BF16)


def _mid(x, o_mla, o_swa, gate, w_out, g_in, w_in, w_f):
    B, S, D = x.shape
    tm = PROJ_ROWS
    row = lambda width: pl.BlockSpec((1, tm, width), lambda b, i: (b, i, 0))
    outs = [(D, F32), (2 * FOX_WIDTH, BF16), (FOX_HEADS * LANES, BF16), (FOX_WIDTH, BF16), (LANES, F32)]
    return pl.pallas_call(
        _mid_kernel,
        grid=(B, S // tm),
        in_specs=[row(D), row(D // 2), row(D // 2), row(D)] + [_resident(a.shape) for a in (w_out, g_in, w_in, w_f)],
        out_specs=[row(width) for width, _ in outs],
        out_shape=[jax.ShapeDtypeStruct((B, S, width), dtype) for width, dtype in outs],
        compiler_params=_params(2),
        name="mid",
    )(x, o_mla, o_swa, gate, w_out, g_in, w_in, w_f)


def _split3(x):
    hi = x.astype(BF16)
    r1 = x - hi.astype(F32)
    mid = r1.astype(BF16)
    lo = (r1 - mid.astype(F32)).astype(BF16)
    return hi, mid, lo


def _lcum_kernel(f_ref, b_ref, lc_ref):
    S = f_ref.shape[1]
    log_f = jax.nn.log_sigmoid(f_ref[0] + b_ref[...])
    log_f_t = log_f.T[:FOX_HEADS]
    n = CUM_CHUNK
    tri = (lax.broadcasted_iota(jnp.int32, (n, n), 0) <= lax.broadcasted_iota(jnp.int32, (n, n), 1)).astype(BF16)
    carry = jnp.zeros((FOX_HEADS, 1), F32)
    for c in range(S // n):
        hi, mid, lo = _split3(log_f_t[:, c * n:(c + 1) * n])
        cs = (_dot(hi, tri) + _dot(mid, tri)) + _dot(lo, tri) + carry
        lc_ref[0, :, c * n:(c + 1) * n] = cs * LOG2E
        carry = cs[:, n - 1:n]


def _lcum(f, b_pad):
    B, S, _ = f.shape
    return pl.pallas_call(
        _lcum_kernel,
        grid=(B,),
        in_specs=[pl.BlockSpec((1, S, LANES), lambda b: (b, 0, 0)), _resident(b_pad.shape)],
        out_specs=pl.BlockSpec((1, FOX_HEADS, S), lambda b: (b, 0, 0)),
        out_shape=jax.ShapeDtypeStruct((B, FOX_HEADS, S), F32),
        compiler_params=_params(1),
        name="lcum",
    )(f, b_pad)


def _final_kernel(x1_ref, o_ref, gate_ref, w_out_ref, g_ref, y_ref):
    a = (o_ref[0].astype(F32) * _silu(gate_ref[0].astype(F32))).astype(BF16)
    x2 = x1_ref[0] + _dot(a, w_out_ref[...])
    y_ref[0] = _rms(x2, g_ref[...])


def _final(x1, o_fox, gate, w_out, g_final):
    B, S, D = x1.shape
    tm = FINAL_ROWS
    row = pl.BlockSpec((1, tm, D), lambda b, i: (b, i, 0))
    return pl.pallas_call(
        _final_kernel,
        grid=(B, S // tm),
        in_specs=[row, row, row, _resident(w_out.shape), _resident(g_final.shape)],
        out_specs=row,
        out_shape=jax.ShapeDtypeStruct((B, S, D), F32),
        compiler_params=_params(2),
        name="final",
    )(x1, o_fox, gate, w_out, g_final)


def _pad_cols(w, left, total):
    return jnp.pad(w, ((0, 0), (left, total - left - w.shape[1])))


def _pad_rows(w, top, total):
    return jnp.pad(w, ((top, total - top - w.shape[0]), (0, 0)))


def _layer0_weights(w_in, w_q_up, w_kv_up):
    sizes = (MLA_Q_RANK, MLA_KV_RANK, MLA_ROPE, SWA_HEADS * SWA_HEAD_DIM,
             SWA_KV_HEADS * SWA_HEAD_DIM, SWA_KV_HEADS * SWA_HEAD_DIM, D_MODEL)
    cq, ckv, kpe, qs, ks, vs, gate = jnp.split(w_in.T, np.cumsum(sizes)[:-1].tolist(), axis=0)
    half = MLA_ROPE // 2
    kpe_sw = jnp.concatenate([kpe[half:], kpe[:half]], axis=0)
    a_t = jnp.concatenate([cq, ckv, _pad_rows(kpe, MLA_NOPE, LANES), _pad_rows(kpe_sw, MLA_NOPE, LANES)], axis=0)
    d = SWA_HEAD_DIM
    kvs_t = jnp.concatenate([ks[:d], ks[:d], ks[d:], ks[d:], vs[:d], vs[:d], vs[d:], vs[d:]], axis=0)
    swa_scale = SWA_HEAD_DIM ** -0.5
    w_a, w_qs, w_kvs, w_gate = [w_t.astype(BF16).T for w_t in (a_t, qs * swa_scale, kvs_t, gate)]
    qk = MLA_NOPE + MLA_ROPE
    wq3 = w_q_up.reshape(MLA_Q_RANK, MLA_HEADS, qk)
    w_q = jnp.pad(wq3, ((0, 0), (0, 0), (0, LANES - qk))).reshape(MLA_Q_RANK, MLA_HEADS * LANES)
    wq_sw = jnp.concatenate([wq3[:, :, MLA_NOPE + half:], wq3[:, :, MLA_NOPE:MLA_NOPE + half]], axis=2)
    w_qsw = jnp.pad(wq_sw, ((0, 0), (0, 0), (MLA_NOPE, LANES - qk))).reshape(MLA_Q_RANK, MLA_HEADS * LANES)
    wkv3 = w_kv_up.reshape(MLA_KV_RANK, MLA_HEADS, MLA_NOPE + MLA_V)
    w_k = jnp.pad(wkv3[:, :, :MLA_NOPE], ((0, 0), (0, 0), (0, LANES - MLA_NOPE))).reshape(MLA_KV_RANK, MLA_HEADS * LANES)
    wv4 = wkv3[:, :, MLA_NOPE:].reshape(MLA_KV_RANK, MLA_HEADS // 2, 2, MLA_V)
    zeros_v = jnp.zeros_like(wv4[:, :, 0])
    w_v = jnp.stack([wv4[:, :, 0], zeros_v, zeros_v, wv4[:, :, 1]], axis=2).reshape(MLA_KV_RANK, MLA_HEADS * LANES)
    return [w_a, w_qs, w_kvs, w_gate] + [w.astype(BF16) for w in (w_q, w_qsw, w_k, w_v)]


def _rope_table():
    half = MLA_ROPE // 2
    inv_freq = 1.0 / (ROPE_THETA ** (jnp.arange(0, MLA_ROPE, 2, dtype=F32) / MLA_ROPE))
    t = jnp.zeros((8, LANES), F32)
    t = t.at[0, MLA_NOPE:MLA_NOPE + half].set(inv_freq).at[0, MLA_NOPE + half:MLA_NOPE + MLA_ROPE].set(inv_freq)
    t = t.at[1, MLA_NOPE:MLA_NOPE + MLA_ROPE].set(1.0)
    t = t.at[2, MLA_NOPE:MLA_NOPE + half].set(-1.0).at[2, MLA_NOPE + half:MLA_NOPE + MLA_ROPE].set(1.0)
    t = t.at[3, :MLA_NOPE].set(1.0)
    return t


def kernel(x, positions, e_g_in, e_w_in, e_g_q_a, e_w_q_up, e_g_kv_a, e_w_kv_up, e_sinks,
           e_w_out, o_g_in, o_w_in, o_b_f, o_w_out, g_final):
    B, S, D = x.shape
    assert D == D_MODEL and all(S % rows == 0 for rows in (PROJ_ROWS, FINAL_ROWS, ATTN_TILE, SWA_ROWS))
    row = lambda g: g.reshape(1, -1).astype(F32)

    w0 = _layer0_weights(e_w_in[0], e_w_q_up[0], e_w_kv_up[0])
    w_a, w_qs, w_kvs, w_gate, w_q, w_qsw, w_k, w_v = w0
    qm, km, vm, qs, kvs, gate0 = _proj0(
        x, positions.reshape(B, S, 1), row(e_g_in[0]), w_a, w_qs, w_kvs, w_gate,
        row(e_g_q_a[0]), w_q, w_qsw, row(e_g_kv_a[0]), w_k, w_v, _rope_table())
    o_mla = _flash(qm, km, vm, None, q_col0=0, k_col0=0, v_col0=0,
                   n_pairs=MLA_HEADS // 2, head_lanes=LANES)
    o_swa = _swa(e_sinks[0].astype(F32), qs, kvs)

    sizes = (FOX_WIDTH, FOX_WIDTH, FOX_WIDTH, FOX_HEADS, FOX_WIDTH)
    w1_q, w1_k, w1_v, w1_f, w1_gate = jnp.split(o_w_in[0].T, np.cumsum(sizes)[:-1].tolist(), axis=0)
    fox_scale = FOX_HEAD_DIM ** -0.5
    w1 = jnp.concatenate([w1_q * fox_scale, w1_k, w1_v, w1_gate], axis=0).astype(BF16).T
    w1_f_pad = _pad_rows(w1_f, 0, LANES).astype(BF16).T
    x1, qk1, v1, gate1, f_logit = _mid(x, o_mla, o_swa, gate0, e_w_out[0].astype(BF16), row(o_g_in[0]), w1, w1_f_pad)

    lc = _lcum(f_logit, _pad_cols(row(o_b_f[0]), 0, LANES))
    lc4 = lc.reshape(B, FOX_HEADS, S // ATTN_TILE, 1, ATTN_TILE)
    o_fox = _flash(qk1, qk1, v1, lc4, q_col0=0, k_col0=FOX_WIDTH // LANES, v_col0=0,
                   n_pairs=FOX_HEADS // 2, head_lanes=HALF)
    return _final(x1, o_fox, gate1, o_w_out[0].astype(BF16), row(g_final))
```

```python
import functools

import jax
import jax.numpy as jnp
import numpy as np
from jax import lax
from jax.experimental import pallas as pl
from jax.experimental.pallas import tpu as pltpu

F32 = jnp.float32
BF16 = jnp.bfloat16

D_MODEL = 1024
EPS = 1e-6
ROPE_THETA = 10000.0
WINDOW = 128

MLA_HEADS = 8
MLA_Q_RANK = 256
MLA_KV_RANK = 128
MLA_NOPE = 64
MLA_ROPE = 32
MLA_V = 64
SWA_HEADS = 8
SWA_KV_HEADS = 2
SWA_HEAD_DIM = 64
FOX_HEADS = 16
FOX_HEAD_DIM = 64
FOX_WIDTH = FOX_HEADS * FOX_HEAD_DIM

LANES = 128
HALF = 64
V7X_VMEM_LIMIT_BYTES = 56 * 1024 * 1024

PROJ_ROWS = 512
FINAL_ROWS = 1024
ATTN_TILE = 512
SWA_ROWS = 512
CUM_CHUNK = 256
LOG2E = 1.4426950408889634
EXP2_UNDERFLOW = 152.0
NORM_SLACK = 1.01
BOUND_SLACK = 2.0


def _dot(a, b):
    return jnp.dot(a, b, preferred_element_type=F32)


def _dot_nt(a, b):
    return lax.dot_general(a, b, (((1,), (1,)), ((), ())), preferred_element_type=F32)


def _rms(xf, g):
    ms = jnp.mean(xf * xf, axis=-1, keepdims=True)
    return xf * lax.rsqrt(ms + EPS) * g


def _silu(x):
    return x * (1.0 / (1.0 + jnp.exp(-x)))


def _params(n_parallel):
    return pltpu.CompilerParams(
        dimension_semantics=("parallel",) * n_parallel,
        vmem_limit_bytes=V7X_VMEM_LIMIT_BYTES,
    )


def _resident(shape):
    nd = len(shape)
    return pl.BlockSpec(shape, lambda *_: (0,) * nd, pipeline_mode=pl.Buffered(1))


def _proj0_kernel(x_ref, pos_ref, g_in_ref, w_a_ref, w_qs_ref, w_kvs_ref, w_gate_ref,
                  g_q_ref, w_q_ref, w_qsw_ref, g_kv_ref, w_k_ref, w_v_ref, rope_ref,
                  qm_ref, km_ref, vm_ref, qs_ref, kvs_ref, gate_ref):
    h = _rms(x_ref[0], g_in_ref[...]).astype(BF16)

    za = _dot(h, w_a_ref[...])
    cqn = _rms(za[:, :MLA_Q_RANK], g_q_ref[...]).astype(BF16)
    ckvn = _rms(za[:, MLA_Q_RANK:MLA_Q_RANK + MLA_KV_RANK], g_kv_ref[...]).astype(BF16)
    kpe = za[:, 384:512]
    kpe_sw = za[:, 512:640]

    pos_row = pos_ref[0, 0].astype(F32)
    pos_col = jnp.broadcast_to(pos_row, (LANES, pos_row.shape[1])).T
    ang = pos_col * rope_ref[0:1, :]
    cos_t = jnp.cos(ang) * rope_ref[1:2, :]
    sin_t = jnp.sin(ang) * rope_ref[2:3, :]
    scale = (MLA_NOPE + MLA_ROPE) ** -0.5 * LOG2E
    q_cos = (cos_t + rope_ref[3:4, :]) * scale
    q_sin = sin_t * scale
    kpe_rot = kpe * cos_t + kpe_sw * sin_t

    for c in range(MLA_HEADS // 2):
        cols = slice(c * 2 * LANES, (c + 1) * 2 * LANES)
        q = _dot(cqn, w_q_ref[:, cols])
        q_sw = _dot(cqn, w_qsw_ref[:, cols])
        kn = _dot(ckvn, w_k_ref[:, cols])
        for hh in range(2):
            lo = hh * LANES
            qm_ref[0, :, c * 2 * LANES + lo:c * 2 * LANES + lo + LANES] = (
                q[:, lo:lo + LANES] * q_cos + q_sw[:, lo:lo + LANES] * q_sin).astype(BF16)
            km_ref[0, :, c * 2 * LANES + lo:c * 2 * LANES + lo + LANES] = (
                kn[:, lo:lo + LANES] + kpe_rot).astype(BF16)
    v = _dot(ckvn, w_v_ref[...])
    lane_v = lax.broadcasted_iota(jnp.int32, v.shape, 1)
    ones_half = (((lane_v >> 6) ^ (lane_v >> 7)) & 1) == 1
    vm_ref[0] = jnp.where(ones_half, 1.0, v).astype(BF16)
    qs_ref[0] = _dot(h, w_qs_ref[...]).astype(BF16)
    kvs_ref[0] = _dot(h, w_kvs_ref[...]).astype(BF16)
    gate_ref[0] = _dot(h, w_gate_ref[...]).astype(BF16)


def _proj0(x, pos3, g_in, w_a, w_qs, w_kvs, w_gate, g_q, w_q, w_qsw, g_kv, w_k, w_v, rope):
    B, S, D = x.shape
    tm = PROJ_ROWS
    row = lambda width: pl.BlockSpec((1, tm, width), lambda b, i: (b, i, 0))
    out_widths = (MLA_HEADS * LANES, MLA_HEADS * LANES, MLA_HEADS * LANES,
                  SWA_HEADS * SWA_HEAD_DIM, 4 * LANES, D)
    return pl.pallas_call(
        _proj0_kernel,
        grid=(B, S // tm),
        in_specs=[row(D), pl.BlockSpec((1, 1, 1, tm), lambda b, i: (b, i, 0, 0))] + [_resident(a.shape) for a in
                                     (g_in, w_a, w_qs, w_kvs, w_gate, g_q, w_q, w_qsw, g_kv, w_k, w_v, rope)],
        out_specs=[row(w) for w in out_widths],
        out_shape=[jax.ShapeDtypeStruct((B, S, w), BF16) for w in out_widths],
        compiler_params=_params(2),
        name="proj0",
    )(x, pos3, g_in, w_a, w_qs, w_kvs, w_gate, g_q, w_q, w_qsw, g_kv, w_k, w_v, rope)


def _fox_schedule(q_ref, k_ref, lc_ref, ti_ref, tj_ref, nq):
    T = ATTN_TILE
    q_all, k_all = q_ref[0], k_ref[0]
    prods = jnp.concatenate([q_all * q_all, k_all * k_all, q_all * k_all], axis=1)
    d_id = lax.broadcasted_iota(jnp.int32, (3 * LANES, LANES), 0)
    c_id = lax.broadcasted_iota(jnp.int32, (3 * LANES, LANES), 1)
    head_cols = (c_id == 2 * (d_id >> 7) + ((d_id >> 6) & 1)).astype(BF16)
    sums = _dot(prods, head_cols)
    k_max = jnp.max(sums, axis=0, keepdims=True)[:, 2:4]
    j_lo = [0]
    for i in range(1, nq):
        r = slice(i * T, (i + 1) * T)
        u = (NORM_SLACK * jnp.sqrt(jnp.max(sums[r], axis=0, keepdims=True)[:, 0:2] * k_max)
             - jnp.min(sums[r], axis=0, keepdims=True)[:, 4:6] + BOUND_SLACK)
        count = jnp.zeros((1, 1), jnp.int32)
        for j in range(i):
            skip = None
            for hh in range(2):
                bound = u[:, hh:hh + 1] + lc_ref[0, hh, i][:, 0:1] - lc_ref[0, hh, j][:, T - 1:T]
                skip_h = bound < -EXP2_UNDERFLOW
                skip = skip_h if skip is None else skip & skip_h
            count = count + skip.astype(jnp.int32)
        j_lo.append(count[0, 0])

    for i in range(nq):
        ti_ref[2 * i] = jnp.int32(i)
        tj_ref[2 * i] = jnp.int32(i)

    def slot_of(pos):
        return jnp.where(pos < nq, 2 * pos + 1, pos + nq)

    pos = jnp.int32(0)
    pad_i = pad_j = jnp.int32(0)
    for i in range(1, nq):
        for j in range(i):
            keep = j >= j_lo[i]

            @pl.when(keep)
            def _(i=i, j=j, pos=pos):
                ti_ref[slot_of(pos)] = jnp.int32(i)
                tj_ref[slot_of(pos)] = jnp.int32(j)

            pos = pos + keep.astype(jnp.int32)
            pad_i = jnp.where(keep, pad_i, i)
            pad_j = jnp.where(keep, pad_j, j)
    assert nq % 4 == 0
    min_full = nq + 4
    n_full = jnp.maximum(((pos + 1) >> 1) << 1, min_full)
    for extra in range(min_full):
        @pl.when(pos + extra < n_full)
        def _(extra=extra):
            ti_ref[slot_of(pos + extra)] = pad_i
            tj_ref[slot_of(pos + extra)] = pad_j
    return n_full


def _flash_kernel(*refs, head_lanes, has_bias):
    T = ATTN_TILE
    if has_bias:
        q_ref, k_ref, v_ref, lc_ref, o_ref = refs[:5]
        scr = refs[5:-2]
        ti_ref, tj_ref = refs[-2:]
        nq = q_ref.shape[1] // T
        n_full = _fox_schedule(q_ref, k_ref, lc_ref, ti_ref, tj_ref, nq)
        n_bodies = (n_full >> 2) + 1
        odd_pair = (n_full & 2) != 0
        last_step = nq + n_full - 1
    else:
        ti_ref, tj_ref, q_ref, k_ref, v_ref, o_ref = refs[:6]
        lc_ref = None
        scr = refs[6:]
        nq = q_ref.shape[1] // T
        assert (ti_ref.shape[0] - 4) % 4 == 0
        n_bodies = (ti_ref.shape[0] - 4) // 4
        odd_pair = None
        last_step = ti_ref.shape[0] - 1
    p_scr, alpha_scr = scr[0:2], scr[2:4]
    m_scr, acc_scr, mask_scr = scr[4:]
    lane = lax.broadcasted_iota(jnp.int32, (T, LANES), 1)
    lo_half = lane < HALF
    zero = jnp.zeros((), BF16)

    def rows(i):
        return pl.ds(pl.multiple_of(i * T, T), T)

    H = T // 2
    full_blocks = ((0, T, T),)
    diag_blocks = ((0, H, H), (H, H, T))

    def softmax(t, slot, diag):
        i, j = ti_ref[t], tj_ref[t]
        if head_lanes == HALF:
            q_pair = q_ref[0, rows(i), :]
            q_heads = [jnp.where(lo_half, q_pair, zero), jnp.where(lo_half, zero, q_pair)]
        for hh in range(2):
            for r0, nr, nc in (diag_blocks if diag else full_blocks):
                kr = pl.ds(pl.multiple_of(j * T, T), nc)
                if head_lanes == LANES:
                    q_h = q_ref[0, pl.ds(pl.multiple_of(i * T + r0, H), nr), hh * LANES:(hh + 1) * LANES]
                    k_t = k_ref[0, kr, hh * LANES:(hh + 1) * LANES]
                else:
                    q_h = q_heads[hh][r0:r0 + nr]
                    k_t = k_ref[0, kr, :]
                s = _dot_nt(q_h, k_t)
                if has_bias:
                    s = s + (lc_ref[0, hh, i][:, 0:1] - lc_ref[0, hh, j][:, :nc])
                if diag:
                    tri = s[:, nc - H:] + mask_scr[...]
                    s = tri if nc == H else jnp.concatenate([s[:, :nc - H], tri], axis=1)
                state_rows = pl.ds(pl.multiple_of(i * T + r0, H), nr)
                m_prev = m_scr[hh, state_rows, :]
                m_next = jnp.maximum(m_prev, jnp.max(s, axis=1, keepdims=True))
                alpha_scr[slot][hh, r0:r0 + nr, :] = jnp.exp2(m_prev - m_next)
                p = jnp.exp2(s - jnp.concatenate([m_next] * (nc // LANES), axis=1))
                p_scr[slot][hh, r0:r0 + nr, :nc] = p.astype(BF16)
                m_scr[hh, state_rows, :] = m_next

    def accumulate(t, slot, diag):
        i, j = ti_ref[t], tj_ref[t]
        for hh in range(2):
            for r0, nr, nc in (diag_blocks if diag else full_blocks):
                state_rows = pl.ds(pl.multiple_of(i * T + r0, H), nr)
                v_t = v_ref[0, pl.ds(pl.multiple_of(j * T, T), nc), hh * LANES:(hh + 1) * LANES]
                pv = _dot(p_scr[slot][hh, r0:r0 + nr, :nc], v_t)
                acc_scr[hh, state_rows, :] = alpha_scr[slot][hh, r0:r0 + nr, :] * acc_scr[hh, state_rows, :] + pv

    m_scr[...] = jnp.full(m_scr.shape, -jnp.inf, F32)
    acc_scr[...] = jnp.zeros(acc_scr.shape, F32)
    row_id = lax.broadcasted_iota(jnp.int32, (H, H), 0)
    col_id = lax.broadcasted_iota(jnp.int32, (H, H), 1)
    mask_scr[...] = jnp.where(col_id <= row_id, 0.0, -jnp.inf).astype(F32)

    last_diag = 2 * (nq - 1)

    def is_diag(step):
        return step % 2 == 0 and step <= last_diag

    def step_pair(t, t_static):
        softmax(t, 1, False)
        accumulate(t - 1, 0, is_diag(t_static - 1))
        softmax(t + 1, 0, is_diag(t_static + 1))
        accumulate(t, 1, False)

    def make_body(u_static):
        def body(u, carry):
            step_pair(4 * u + 3, 4 * u_static + 3)
            step_pair(4 * u + 5, 4 * u_static + 5)
            return carry
        return body

    assert (last_diag - 6) % 4 == 0
    n_diag_bodies = (last_diag - 6) // 4 + 1
    softmax(0, 0, True)
    step_pair(1, 1)
    lax.fori_loop(0, n_diag_bodies, make_body(0), 0)
    make_body(n_diag_bodies)(n_diag_bodies, 0)
    lax.fori_loop(n_diag_bodies + 1, n_bodies, make_body(n_diag_bodies + 1), 0)
    if odd_pair is not None:
        @pl.when(odd_pair)
        def _():
            step_pair(4 * n_bodies + 3, 4 * (n_diag_bodies + 1) + 3)
    softmax(last_step, 1, False)
    accumulate(last_step - 1, 0, False)
    accumulate(last_step, 1, False)

    def normalize(i, carry):
        r = rows(i)
        acc_even, acc_odd = acc_scr[0, r, :], acc_scr[1, r, :]
        num = jnp.where(lo_half, acc_even, acc_odd)
        den = pltpu.roll(jnp.where(lo_half, acc_odd, acc_even), HALF, 1)
        o_ref[0, r, :] = (num / den).astype(o_ref.dtype)
        return carry

    lax.fori_loop(0, nq, normalize, 0)


def _tile_schedule(nq):
    diag = [(i, i) for i in range(nq)]
    full = [(i, j) for i in range(nq) for j in range(i)]
    n_even = (len(diag) + len(full)) // 2
    assert (len(diag) + len(full)) % 2 == 0 and n_even >= len(diag) and n_even >= 2
    even = diag + full[:n_even - len(diag)]
    odd = full[n_even - len(diag):]
    return np.array([t for pair in zip(even, odd) for t in pair], np.int32)


def _flash(q, k, v, lc, *, q_col0, k_col0, v_col0, n_pairs, head_lanes):
    B, S, _ = q.shape
    T = ATTN_TILE
    qk_w = 2 * head_lanes
    qk_step = qk_w // LANES
    steps = _tile_schedule(S // T)
    in_specs = [
        pl.BlockSpec((1, S, qk_w), lambda b, p: (b, 0, q_col0 // qk_step + p)),
        pl.BlockSpec((1, S, qk_w), lambda b, p: (b, 0, k_col0 // qk_step + p)),
        pl.BlockSpec((1, S, 2 * LANES), lambda b, p: (b, 0, v_col0 // 2 + p)),
    ]
    args = [q, k, v]
    per_slot = lambda shape, dtype: [pltpu.VMEM(shape, dtype)] * 2
    scratch = (per_slot((2, T, T), BF16) + per_slot((2, T, LANES), F32)
               + [pltpu.VMEM((2, S, LANES), F32)] * 2
               + [pltpu.VMEM((T // 2, T // 2), F32)])
    if lc is not None:
        in_specs.append(pl.BlockSpec((1, 2) + lc.shape[2:], lambda b, p: (b, p, 0, 0, 0)))
        args.append(lc)
        scratch += [pltpu.SMEM((len(steps),), jnp.int32)] * 2
    else:
        smem = pl.BlockSpec(memory_space=pltpu.SMEM)
        in_specs = [smem, smem] + in_specs
        args = [jnp.asarray(steps[:, 0]), jnp.asarray(steps[:, 1])] + args
    return pl.pallas_call(
        functools.partial(_flash_kernel, head_lanes=head_lanes, has_bias=lc is not None),
        grid=(B, n_pairs),
        in_specs=in_specs,
        out_specs=pl.BlockSpec((1, S, LANES), lambda b, p: (b, 0, p)),
        out_shape=jax.ShapeDtypeStruct((B, S, n_pairs * LANES), BF16),
        scratch_shapes=scratch,
        compiler_params=_params(2),
        name="fox" if lc is not None else "mla",
    )(*args)


def _swa_kernel(sinks_ref, bias_ref, qs_ref, kvs_ref, o_ref):
    W = WINDOW
    group = SWA_HEADS // SWA_KV_HEADS
    step = pl.program_id(1)
    lane = lax.broadcasted_iota(jnp.int32, (W, LANES), 1)
    lo_half = lane < HALF
    zero = jnp.zeros((), BF16)

    for blk in range(SWA_ROWS // W):
        t0 = step * SWA_ROWS + blk * W
        k_start = pl.multiple_of(jnp.maximum(t0 - W, 0), W)
        bias_sel = jnp.where(step == 0, 0, 1) if blk == 0 else 1
        rows = slice(blk * W, (blk + 1) * W)
        for g in range(SWA_KV_HEADS):
            k_t = kvs_ref[0, pl.ds(k_start, 2 * W), g * LANES:(g + 1) * LANES]
            v_t = kvs_ref[0, pl.ds(k_start, 2 * W), (2 + g) * LANES:(3 + g) * LANES]
            q_parts = []
            for pair in range(group // 2):
                q_pair = qs_ref[0, rows, (g * 2 + pair) * LANES:(g * 2 + pair + 1) * LANES]
                q_parts.append(jnp.where(lo_half, q_pair, zero))
                q_parts.append(jnp.where(lo_half, zero, q_pair))
            s_all = _dot_nt(jnp.concatenate(q_parts, axis=0), k_t)
            p_parts, denoms = [], []
            for hq in range(group):
                head = g * group + hq
                sink = sinks_ref[head]
                s = s_all[hq * W:(hq + 1) * W] + bias_ref[bias_sel, head]
                m = jnp.maximum(jnp.max(s, axis=1, keepdims=True), sink)
                p = jnp.exp(s - m)
                denoms.append(jnp.sum(p, axis=1, keepdims=True) + jnp.exp(sink - m))
                p_parts.append(p.astype(BF16))
            o_all = _dot(jnp.concatenate(p_parts, axis=0), v_t)
            for pair in range(group // 2):
                o_even = o_all[(2 * pair) * W:(2 * pair + 1) * W] / denoms[2 * pair]
                o_odd = o_all[(2 * pair + 1) * W:(2 * pair + 2) * W] / denoms[2 * pair + 1]
                o_ref[0, rows, (g * 2 + pair) * LANES:(g * 2 + pair + 1) * LANES] = jnp.where(
                    lo_half, o_even, o_odd).astype(o_ref.dtype)


def _swa_bias():
    W = WINDOW
    slopes = 2.0 ** (-8.0 * (np.arange(SWA_HEADS, dtype=np.float32) + 1.0) / SWA_HEADS)
    a = np.arange(W, dtype=np.float32)[:, None]
    c = np.arange(2 * W, dtype=np.float32)[None, :]
    tables = []
    for k_offset in (0.0, float(W)):
        dist = k_offset + a - c
        bias = -slopes.astype(np.float32)[:, None, None] * dist[None]
        tables.append(np.where((dist >= 0) & (dist < W), bias, -np.inf).astype(np.float32))
    return jnp.asarray(np.stack(tables))


def _swa(sinks, qs, kvs):
    B, S, C = qs.shape
    bias = _swa_bias()
    return pl.pallas_call(
        _swa_kernel,
        grid=(B, S // SWA_ROWS),
        in_specs=[pl.BlockSpec(memory_space=pltpu.SMEM), _resident(bias.shape),
                  pl.BlockSpec((1, SWA_ROWS, C), lambda b, i: (b, i, 0)),
                  pl.BlockSpec((1, S, kvs.shape[2]), lambda b, i: (b, 0, 0))],
        out_specs=pl.BlockSpec((1, SWA_ROWS, C), lambda b, i: (b, i, 0)),
        out_shape=jax.ShapeDtypeStruct((B, S, C), BF16),
        compiler_params=_params(2),
        name="swa",
    )(sinks, bias, qs, kvs)


def _mid_kernel(x_ref, om_ref, os_ref, gate_ref, w_out_ref, g_in_ref, w_in_ref, w_f_ref,
                x1_ref, qk_ref, v_ref, g1_ref, f_ref):
    half = D_MODEL // 2
    gate = _silu(gate_ref[0].astype(F32))
    a_m = (om_ref[0].astype(F32) * gate[:, :half]).astype(BF16)
    a_s = (os_ref[0].astype(F32) * gate[:, half:]).astype(BF16)
    x1 = x_ref[0] + (_dot(a_m, w_out_ref[:half, :]) + _dot(a_s, w_out_ref[half:, :]))
    x1_ref[0] = x1
    h = _rms(x1, g_in_ref[...]).astype(BF16)
    w = FOX_WIDTH
    qk_ref[0, :, :w] = (_dot(h, w_in_ref[:, :w]) * LOG2E).astype(BF16)
    qk_ref[0, :, w:] = _dot(h, w_in_ref[:, w:2 * w]).astype(BF16)
    g1_ref[0] = _dot(h, w_in_ref[:, 3 * w:]).astype(BF16)
    f_ref[0] = _dot(h, w_f_ref[...])
    v = _dot(h, w_in_ref[:, 2 * w:3 * w])
    lo_half = lax.broadcasted_iota(jnp.int32, (v.shape[0], LANES), 1) < HALF
    for pair in range(FOX_HEADS // 2):
        v_pair = v[:, pair * LANES:(pair + 1) * LANES]
        v_ref[0, :, (2 * pair) * LANES:(2 * pair + 1) * LANES] = jnp.where(lo_half, v_pair, 1.0).astype(BF16)
        v_ref[0, :, (2 * pair + 1) * LANES:(2 * pair + 2) * LANES] = jnp.where(lo_half, 1.0, v_pair).astype(BF16)


def _mid(x, o_mla, o_swa, gate, w_out, g_in, w_in, w_f):
    B, S, D = x.shape
    tm = PROJ_ROWS
    row = lambda width: pl.BlockSpec((1, tm, width), lambda b, i: (b, i, 0))
    outs = [(D, F32), (2 * FOX_WIDTH, BF16), (FOX_HEADS * LANES, BF16), (FOX_WIDTH, BF16), (LANES, F32)]
    return pl.pallas_call(
        _mid_kernel,
        grid=(B, S // tm),
        in_specs=[row(D), row(D // 2), row(D // 2), row(D)] + [_resident(a.shape) for a in (w_out, g_in, w_in, w_f)],
        out_specs=[row(width) for width, _ in outs],
        out_shape=[jax.ShapeDtypeStruct((B, S, width), dtype) for width, dtype in outs],
        compiler_params=_params(2),
        name="mid",
    )(x, o_mla, o_swa, gate, w_out, g_in, w_in, w_f)


def _split3(x):
    hi = x.astype(BF16)
    r1 = x - hi.astype(F32)
    mid = r1.astype(BF16)
    lo = (r1 - mid.astype(F32)).astype(BF16)
    return hi, mid, lo


def _lcum_kernel(f_ref, b_ref, lc_ref):
    S = f_ref.shape[1]
    log_f = jax.nn.log_sigmoid(f_ref[0] + b_ref[...])
    log_f_t = log_f.T[:FOX_HEADS]
    n = CUM_CHUNK
    tri = (lax.broadcasted_iota(jnp.int32, (n, n), 0) <= lax.broadcasted_iota(jnp.int32, (n, n), 1)).astype(BF16)
    carry = jnp.zeros((FOX_HEADS, 1), F32)
    for c in range(S // n):
        hi, mid, lo = _split3(log_f_t[:, c * n:(c + 1) * n])
        cs = (_dot(hi, tri) + _dot(mid, tri)) + _dot(lo, tri) + carry
        cs2 = cs * LOG2E
        tile, lane0 = divmod(c * n, ATTN_TILE)
        for head in range(FOX_HEADS):
            lc_ref[0, head, tile, :, lane0:lane0 + n] = cs2[head:head + 1]
        carry = cs[:, n - 1:n]


def _lcum(f, b_pad):
    B, S, _ = f.shape
    out_block = (1, FOX_HEADS, S // ATTN_TILE, 1, ATTN_TILE)
    return pl.pallas_call(
        _lcum_kernel,
        grid=(B,),
        in_specs=[pl.BlockSpec((1, S, LANES), lambda b: (b, 0, 0)), _resident(b_pad.shape)],
        out_specs=pl.BlockSpec(out_block, lambda b: (b, 0, 0, 0, 0)),
        out_shape=jax.ShapeDtypeStruct((B,) + out_block[1:], F32),
        compiler_params=_params(1),
        name="lcum",
    )(f, b_pad)


def _final_kernel(x1_ref, o_ref, gate_ref, w_out_ref, g_ref, y_ref):
    a = (o_ref[0].astype(F32) * _silu(gate_ref[0].astype(F32))).astype(BF16)
    x2 = x1_ref[0] + _dot(a, w_out_ref[...])
    y_ref[0] = _rms(x2, g_ref[...])


def _final(x1, o_fox, gate, w_out, g_final):
    B, S, D = x1.shape
    tm = FINAL_ROWS
    row = pl.BlockSpec((1, tm, D), lambda b, i: (b, i, 0))
    return pl.pallas_call(
        _final_kernel,
        grid=(B, S // tm),
        in_specs=[row, row, row, _resident(w_out.shape), _resident(g_final.shape)],
        out_specs=row,
        out_shape=jax.ShapeDtypeStruct((B, S, D), F32),
        compiler_params=_params(2),
        name="final",
    )(x1, o_fox, gate, w_out, g_final)


def _pad_cols(w, left, total):
    return jnp.pad(w, ((0, 0), (left, total - left - w.shape[1])))


def _pad_rows(w, top, total):
    return jnp.pad(w, ((top, total - top - w.shape[0]), (0, 0)))


def _layer0_weights(w_in, w_q_up, w_kv_up):
    sizes = (MLA_Q_RANK, MLA_KV_RANK, MLA_ROPE, SWA_HEADS * SWA_HEAD_DIM,
             SWA_KV_HEADS * SWA_HEAD_DIM, SWA_KV_HEADS * SWA_HEAD_DIM, D_MODEL)
    cq, ckv, kpe, qs, ks, vs, gate = jnp.split(w_in.T, np.cumsum(sizes)[:-1].tolist(), axis=0)
    half = MLA_ROPE // 2
    kpe_sw = jnp.concatenate([kpe[half:], kpe[:half]], axis=0)
    a_t = jnp.concatenate([cq, ckv, _pad_rows(kpe, MLA_NOPE, LANES), _pad_rows(kpe_sw, MLA_NOPE, LANES)], axis=0)
    d = SWA_HEAD_DIM
    kvs_t = jnp.concatenate([ks[:d], ks[:d], ks[d:], ks[d:], vs[:d], vs[:d], vs[d:], vs[d:]], axis=0)
    swa_scale = SWA_HEAD_DIM ** -0.5
    w_a, w_qs, w_kvs, w_gate = [w_t.astype(BF16).T for w_t in (a_t, qs * swa_scale, kvs_t, gate)]
    qk = MLA_NOPE + MLA_ROPE
    wq3 = w_q_up.reshape(MLA_Q_RANK, MLA_HEADS, qk)
    w_q = jnp.pad(wq3, ((0, 0), (0, 0), (0, LANES - qk))).reshape(MLA_Q_RANK, MLA_HEADS * LANES)
    wq_sw = jnp.concatenate([wq3[:, :, MLA_NOPE + half:], wq3[:, :, MLA_NOPE:MLA_NOPE + half]], axis=2)
    w_qsw = jnp.pad(wq_sw, ((0, 0), (0, 0), (MLA_NOPE, LANES - qk))).reshape(MLA_Q_RANK, MLA_HEADS * LANES)
    wkv3 = w_kv_up.reshape(MLA_KV_RANK, MLA_HEADS, MLA_NOPE + MLA_V)
    w_k = jnp.pad(wkv3[:, :, :MLA_NOPE], ((0, 0), (0, 0), (0, LANES - MLA_NOPE))).reshape(MLA_KV_RANK, MLA_HEADS * LANES)
    wv4 = wkv3[:, :, MLA_NOPE:].reshape(MLA_KV_RANK, MLA_HEADS // 2, 2, MLA_V)
    zeros_v = jnp.zeros_like(wv4[:, :, 0])
    w_v = jnp.stack([wv4[:, :, 0], zeros_v, zeros_v, wv4[:, :, 1]], axis=2).reshape(MLA_KV_RANK, MLA_HEADS * LANES)
    return [w_a, w_qs, w_kvs, w_gate] + [w.astype(BF16) for w in (w_q, w_qsw, w_k, w_v)]


def _rope_table():
    half = MLA_ROPE // 2
    inv_freq = 1.0 / (ROPE_THETA ** (jnp.arange(0, MLA_ROPE, 2, dtype=F32) / MLA_ROPE))
    t = jnp.zeros((8, LANES), F32)
    t = t.at[0, MLA_NOPE:MLA_NOPE + half].set(inv_freq).at[0, MLA_NOPE + half:MLA_NOPE + MLA_ROPE].set(inv_freq)
    t = t.at[1, MLA_NOPE:MLA_NOPE + MLA_ROPE].set(1.0)
    t = t.at[2, MLA_NOPE:MLA_NOPE + half].set(-1.0).at[2, MLA_NOPE + half:MLA_NOPE + MLA_ROPE].set(1.0)
    t = t.at[3, :MLA_NOPE].set(1.0)
    return t


def kernel(x, positions, e_g_in, e_w_in, e_g_q_a, e_w_q_up, e_g_kv_a, e_w_kv_up, e_sinks,
           e_w_out, o_g_in, o_w_in, o_b_f, o_w_out, g_final):
    B, S, D = x.shape
    assert D == D_MODEL and all(S % rows == 0 for rows in (PROJ_ROWS, FINAL_ROWS, ATTN_TILE, SWA_ROWS))
    row = lambda g: g.reshape(1, -1).astype(F32)

    w0 = _layer0_weights(e_w_in[0], e_w_q_up[0], e_w_kv_up[0])
    w_a, w_qs, w_kvs, w_gate, w_q, w_qsw, w_k, w_v = w0
    qm, km, vm, qs, kvs, gate0 = _proj0(
        x, positions.reshape(B, S // PROJ_ROWS, 1, PROJ_ROWS), row(e_g_in[0]), w_a, w_qs, w_kvs, w_gate,
        row(e_g_q_a[0]), w_q, w_qsw, row(e_g_kv_a[0]), w_k, w_v, _rope_table())
    o_mla = _flash(qm, km, vm, None, q_col0=0, k_col0=0, v_col0=0,
                   n_pairs=MLA_HEADS // 2, head_lanes=LANES)
    o_swa = _swa(e_sinks[0].astype(F32), qs, kvs)

    sizes = (FOX_WIDTH, FOX_WIDTH, FOX_WIDTH, FOX_HEADS, FOX_WIDTH)
    w1_q, w1_k, w1_v, w1_f, w1_gate = jnp.split(o_w_in[0].T, np.cumsum(sizes)[:-1].tolist(), axis=0)
    fox_scale = FOX_HEAD_DIM ** -0.5
    w1 = jnp.concatenate([w1_q * fox_scale, w1_k, w1_v, w1_gate], axis=0).astype(BF16).T
    w1_f_pad = _pad_rows(w1_f, 0, LANES).astype(BF16).T
    x1, qk1, v1, gate1, f_logit = _mid(x, o_mla, o_swa, gate0, e_w_out[0].astype(BF16), row(o_g_in[0]), w1, w1_f_pad)

    lc = _lcum(f_logit, _pad_cols(row(o_b_f[0]), 0, LANES))
    o_fox = _flash(qk1, qk1, v1, lc, q_col0=0, k_col0=FOX_WIDTH // LANES, v_col0=0,
                   n_pairs=FOX_HEADS // 2, head_lanes=HALF)
    return _final(x1, o_fox, gate1, o_w_out[0].astype(BF16), row(g_final))
```

```python
import functools

import jax
import jax.numpy as jnp
import numpy as np
from jax import lax
from jax.experimental import pallas as pl
from jax.experimental.pallas import tpu as pltpu

F32 = jnp.float32
BF16 = jnp.bfloat16

D_MODEL = 1024
EPS = 1e-6
ROPE_THETA = 10000.0
WINDOW = 128

MLA_HEADS = 8
MLA_Q_RANK = 256
MLA_KV_RANK = 128
MLA_NOPE = 64
MLA_ROPE = 32
MLA_V = 64
SWA_HEADS = 8
SWA_KV_HEADS = 2
SWA_HEAD_DIM = 64
FOX_HEADS = 16
FOX_HEAD_DIM = 64
FOX_WIDTH = FOX_HEADS * FOX_HEAD_DIM

LANES = 128
HALF = 64
V7X_VMEM_LIMIT_BYTES = 56 * 1024 * 1024

PROJ_ROWS = 512
FINAL_ROWS = 1024
ATTN_TILE = 512
SWA_ROWS = 512
CUM_CHUNK = 256
LOG2E = 1.4426950408889634
EXP2_UNDERFLOW = 152.0
NORM_SLACK = 1.01
BOUND_SLACK = 2.0


def _dot(a, b):
    return jnp.dot(a, b, preferred_element_type=F32)


def _dot_nt(a, b):
    return lax.dot_general(a, b, (((1,), (1,)), ((), ())), preferred_element_type=F32)


def _rms(xf, g):
    ms = jnp.mean(xf * xf, axis=-1, keepdims=True)
    return xf * lax.rsqrt(ms + EPS) * g


def _silu(x):
    return x * (1.0 / (1.0 + jnp.exp(-x)))


def _params(n_parallel):
    return pltpu.CompilerParams(
        dimension_semantics=("parallel",) * n_parallel,
        vmem_limit_bytes=V7X_VMEM_LIMIT_BYTES,
    )


def _resident(shape):
    nd = len(shape)
    return pl.BlockSpec(shape, lambda *_: (0,) * nd, pipeline_mode=pl.Buffered(1))


def _proj0_kernel(x_ref, pos_ref, g_in_ref, w_a_ref, w_qs_ref, w_kvs_ref, w_gate_ref,
                  g_q_ref, w_q_ref, w_qsw_ref, g_kv_ref, w_k_ref, w_v_ref, rope_ref,
                  qm_ref, km_ref, vm_ref, qs_ref, kvs_ref, gate_ref):
    h = _rms(x_ref[0], g_in_ref[...]).astype(BF16)

    za = _dot(h, w_a_ref[...])
    cqn = _rms(za[:, :MLA_Q_RANK], g_q_ref[...]).astype(BF16)
    ckvn = _rms(za[:, MLA_Q_RANK:MLA_Q_RANK + MLA_KV_RANK], g_kv_ref[...]).astype(BF16)
    kpe = za[:, 384:512]
    kpe_sw = za[:, 512:640]

    pos_row = pos_ref[0, 0].astype(F32)
    pos_col = jnp.broadcast_to(pos_row, (LANES, pos_row.shape[1])).T
    ang = pos_col * rope_ref[0:1, :]
    cos_t = jnp.cos(ang) * rope_ref[1:2, :]
    sin_t = jnp.sin(ang) * rope_ref[2:3, :]
    scale = (MLA_NOPE + MLA_ROPE) ** -0.5 * LOG2E
    q_cos = (cos_t + rope_ref[3:4, :]) * scale
    q_sin = sin_t * scale
    kpe_rot = kpe * cos_t + kpe_sw * sin_t

    for c in range(MLA_HEADS // 2):
        cols = slice(c * 2 * LANES, (c + 1) * 2 * LANES)
        q = _dot(cqn, w_q_ref[:, cols])
        q_sw = _dot(cqn, w_qsw_ref[:, cols])
        kn = _dot(ckvn, w_k_ref[:, cols])
        for hh in range(2):
            lo = hh * LANES
            qm_ref[0, :, c * 2 * LANES + lo:c * 2 * LANES + lo + LANES] = (
                q[:, lo:lo + LANES] * q_cos + q_sw[:, lo:lo + LANES] * q_sin).astype(BF16)
            km_ref[0, :, c * 2 * LANES + lo:c * 2 * LANES + lo + LANES] = (
                kn[:, lo:lo + LANES] + kpe_rot).astype(BF16)
    v = _dot(ckvn, w_v_ref[...])
    lane_v = lax.broadcasted_iota(jnp.int32, v.shape, 1)
    ones_half = (((lane_v >> 6) ^ (lane_v >> 7)) & 1) == 1
    vm_ref[0] = jnp.where(ones_half, 1.0, v).astype(BF16)
    qs_ref[0] = (_dot(h, w_qs_ref[...]) * LOG2E).astype(BF16)
    kvs_ref[0] = _dot(h, w_kvs_ref[...]).astype(BF16)
    gate_ref[0] = _dot(h, w_gate_ref[...]).astype(BF16)


def _proj0(x, pos3, g_in, w_a, w_qs, w_kvs, w_gate, g_q, w_q, w_qsw, g_kv, w_k, w_v, rope):
    B, S, D = x.shape
    tm = PROJ_ROWS
    row = lambda width: pl.BlockSpec((1, tm, width), lambda b, i: (b, i, 0))
    out_widths = (MLA_HEADS * LANES, MLA_HEADS * LANES, MLA_HEADS * LANES,
                  SWA_HEADS * SWA_HEAD_DIM, 4 * LANES, D)
    return pl.pallas_call(
        _proj0_kernel,
        grid=(B, S // tm),
        in_specs=[row(D), pl.BlockSpec((1, 1, 1, tm), lambda b, i: (b, i, 0, 0))] + [_resident(a.shape) for a in
                                     (g_in, w_a, w_qs, w_kvs, w_gate, g_q, w_q, w_qsw, g_kv, w_k, w_v, rope)],
        out_specs=[row(w) for w in out_widths],
        out_shape=[jax.ShapeDtypeStruct((B, S, w), BF16) for w in out_widths],
        compiler_params=_params(2),
        name="proj0",
    )(x, pos3, g_in, w_a, w_qs, w_kvs, w_gate, g_q, w_q, w_qsw, g_kv, w_k, w_v, rope)


def _fox_schedule(q_ref, k_ref, lc_ref, ti_ref, tj_ref, nq):
    T = ATTN_TILE
    q_all, k_all = q_ref[0], k_ref[0]
    prods = jnp.concatenate([q_all * q_all, k_all * k_all, q_all * k_all], axis=1)
    d_id = lax.broadcasted_iota(jnp.int32, (3 * LANES, LANES), 0)
    c_id = lax.broadcasted_iota(jnp.int32, (3 * LANES, LANES), 1)
    head_cols = (c_id == 2 * (d_id >> 7) + ((d_id >> 6) & 1)).astype(BF16)
    sums = _dot(prods, head_cols)
    k_max = jnp.max(sums, axis=0, keepdims=True)[:, 2:4]
    j_lo = [0]
    for i in range(1, nq):
        r = slice(i * T, (i + 1) * T)
        u = (NORM_SLACK * jnp.sqrt(jnp.max(sums[r], axis=0, keepdims=True)[:, 0:2] * k_max)
             - jnp.min(sums[r], axis=0, keepdims=True)[:, 4:6] + BOUND_SLACK)
        count = jnp.zeros((1, 1), jnp.int32)
        for j in range(i):
            skip = None
            for hh in range(2):
                bound = u[:, hh:hh + 1] + lc_ref[0, hh, i][:, 0:1] - lc_ref[0, hh, j][:, T - 1:T]
                skip_h = bound < -EXP2_UNDERFLOW
                skip = skip_h if skip is None else skip & skip_h
            count = count + skip.astype(jnp.int32)
        j_lo.append(count[0, 0])

    for i in range(nq):
        ti_ref[2 * i] = jnp.int32(i)
        tj_ref[2 * i] = jnp.int32(i)

    def slot_of(pos):
        return jnp.where(pos < nq, 2 * pos + 1, pos + nq)

    pos = jnp.int32(0)
    pad_i = pad_j = jnp.int32(0)
    for i in range(1, nq):
        for j in range(i):
            keep = j >= j_lo[i]

            @pl.when(keep)
            def _(i=i, j=j, pos=pos):
                ti_ref[slot_of(pos)] = jnp.int32(i)
                tj_ref[slot_of(pos)] = jnp.int32(j)

            pos = pos + keep.astype(jnp.int32)
            pad_i = jnp.where(keep, pad_i, i)
            pad_j = jnp.where(keep, pad_j, j)
    assert nq % 4 == 0
    min_full = nq + 4
    n_full = jnp.maximum(((pos + 1) >> 1) << 1, min_full)
    for extra in range(min_full):
        @pl.when(pos + extra < n_full)
        def _(extra=extra):
            ti_ref[slot_of(pos + extra)] = pad_i
            tj_ref[slot_of(pos + extra)] = pad_j
    return n_full


def _flash_kernel(*refs, head_lanes, has_bias):
    T = ATTN_TILE
    if has_bias:
        q_ref, k_ref, v_ref, lc_ref, o_ref = refs[:5]
        scr = refs[5:-2]
        ti_ref, tj_ref = refs[-2:]
        nq = q_ref.shape[1] // T
        n_full = _fox_schedule(q_ref, k_ref, lc_ref, ti_ref, tj_ref, nq)
        n_bodies = (n_full >> 2) + 1
        odd_pair = (n_full & 2) != 0
        last_step = nq + n_full - 1
    else:
        ti_ref, tj_ref, q_ref, k_ref, v_ref, o_ref = refs[:6]
        lc_ref = None
        scr = refs[6:]
        nq = q_ref.shape[1] // T
        assert (ti_ref.shape[0] - 4) % 4 == 0
        n_bodies = (ti_ref.shape[0] - 4) // 4
        odd_pair = None
        last_step = ti_ref.shape[0] - 1
    p_scr, alpha_scr = scr[0:2], scr[2:4]
    m_scr, acc_scr, mask_scr = scr[4:]
    lane = lax.broadcasted_iota(jnp.int32, (T, LANES), 1)
    lo_half = lane < HALF
    zero = jnp.zeros((), BF16)

    def rows(i):
        return pl.ds(pl.multiple_of(i * T, T), T)

    H = T // 2
    full_blocks = ((0, T, T),)
    diag_blocks = ((0, H, H), (H, H, T))

    def softmax(t, slot, diag):
        i, j = ti_ref[t], tj_ref[t]
        if head_lanes == HALF:
            q_pair = q_ref[0, rows(i), :]
            q_heads = [jnp.where(lo_half, q_pair, zero), jnp.where(lo_half, zero, q_pair)]
        for hh in range(2):
            for r0, nr, nc in (diag_blocks if diag else full_blocks):
                kr = pl.ds(pl.multiple_of(j * T, T), nc)
                if head_lanes == LANES:
                    q_h = q_ref[0, pl.ds(pl.multiple_of(i * T + r0, H), nr), hh * LANES:(hh + 1) * LANES]
                    k_t = k_ref[0, kr, hh * LANES:(hh + 1) * LANES]
                else:
                    q_h = q_heads[hh][r0:r0 + nr]
                    k_t = k_ref[0, kr, :]
                s = _dot_nt(q_h, k_t)
                if has_bias:
                    s = s + (lc_ref[0, hh, i][:, 0:1] - lc_ref[0, hh, j][:, :nc])
                if diag:
                    tri = s[:, nc - H:] + mask_scr[...]
                    s = tri if nc == H else jnp.concatenate([s[:, :nc - H], tri], axis=1)
                state_rows = pl.ds(pl.multiple_of(i * T + r0, H), nr)
                m_prev = m_scr[hh, state_rows, :]
                m_next = jnp.maximum(m_prev, jnp.max(s, axis=1, keepdims=True))
                alpha_scr[slot][hh, r0:r0 + nr, :] = jnp.exp2(m_prev - m_next)
                p = jnp.exp2(s - jnp.concatenate([m_next] * (nc // LANES), axis=1))
                p_scr[slot][hh, r0:r0 + nr, :nc] = p.astype(BF16)
                m_scr[hh, state_rows, :] = m_next

    def accumulate(t, slot, diag):
        i, j = ti_ref[t], tj_ref[t]
        for hh in range(2):
            for r0, nr, nc in (diag_blocks if diag else full_blocks):
                state_rows = pl.ds(pl.multiple_of(i * T + r0, H), nr)
                v_t = v_ref[0, pl.ds(pl.multiple_of(j * T, T), nc), hh * LANES:(hh + 1) * LANES]
                pv = _dot(p_scr[slot][hh, r0:r0 + nr, :nc], v_t)
                acc_scr[hh, state_rows, :] = alpha_scr[slot][hh, r0:r0 + nr, :] * acc_scr[hh, state_rows, :] + pv

    m_scr[...] = jnp.full(m_scr.shape, -jnp.inf, F32)
    acc_scr[...] = jnp.zeros(acc_scr.shape, F32)
    row_id = lax.broadcasted_iota(jnp.int32, (H, H), 0)
    col_id = lax.broadcasted_iota(jnp.int32, (H, H), 1)
    mask_scr[...] = jnp.where(col_id <= row_id, 0.0, -jnp.inf).astype(F32)

    last_diag = 2 * (nq - 1)

    def is_diag(step):
        return step % 2 == 0 and step <= last_diag

    def step_pair(t, t_static):
        softmax(t, 1, False)
        accumulate(t - 1, 0, is_diag(t_static - 1))
        softmax(t + 1, 0, is_diag(t_static + 1))
        accumulate(t, 1, False)

    def make_body(u_static):
        def body(u, carry):
            step_pair(4 * u + 3, 4 * u_static + 3)
            step_pair(4 * u + 5, 4 * u_static + 5)
            return carry
        return body

    assert (last_diag - 6) % 4 == 0
    n_diag_bodies = (last_diag - 6) // 4 + 1
    softmax(0, 0, True)
    step_pair(1, 1)
    lax.fori_loop(0, n_diag_bodies, make_body(0), 0)
    make_body(n_diag_bodies)(n_diag_bodies, 0)
    lax.fori_loop(n_diag_bodies + 1, n_bodies, make_body(n_diag_bodies + 1), 0)
    if odd_pair is not None:
        @pl.when(odd_pair)
        def _():
            step_pair(4 * n_bodies + 3, 4 * (n_diag_bodies + 1) + 3)
    softmax(last_step, 1, False)
    accumulate(last_step - 1, 0, False)
    accumulate(last_step, 1, False)

    def normalize(i, carry):
        r = rows(i)
        acc_even, acc_odd = acc_scr[0, r, :], acc_scr[1, r, :]
        num = jnp.where(lo_half, acc_even, acc_odd)
        den = pltpu.roll(jnp.where(lo_half, acc_odd, acc_even), HALF, 1)
        o_ref[0, r, :] = (num / den).astype(o_ref.dtype)
        return carry

    lax.fori_loop(0, nq, normalize, 0)


def _tile_schedule(nq):
    diag = [(i, i) for i in range(nq)]
    full = [(i, j) for i in range(nq) for j in range(i)]
    n_even = (len(diag) + len(full)) // 2
    assert (len(diag) + len(full)) % 2 == 0 and n_even >= len(diag) and n_even >= 2
    even = diag + full[:n_even - len(diag)]
    odd = full[n_even - len(diag):]
    return np.array([t for pair in zip(even, odd) for t in pair], np.int32)


def _flash(q, k, v, lc, *, q_col0, k_col0, v_col0, n_pairs, head_lanes):
    B, S, _ = q.shape
    T = ATTN_TILE
    qk_w = 2 * head_lanes
    qk_step = qk_w // LANES
    steps = _tile_schedule(S // T)
    in_specs = [
        pl.BlockSpec((1, S, qk_w), lambda b, p: (b, 0, q_col0 // qk_step + p)),
        pl.BlockSpec((1, S, qk_w), lambda b, p: (b, 0, k_col0 // qk_step + p)),
        pl.BlockSpec((1, S, 2 * LANES), lambda b, p: (b, 0, v_col0 // 2 + p)),
    ]
    args = [q, k, v]
    per_slot = lambda shape, dtype: [pltpu.VMEM(shape, dtype)] * 2
    scratch = (per_slot((2, T, T), BF16) + per_slot((2, T, LANES), F32)
               + [pltpu.VMEM((2, S, LANES), F32)] * 2
               + [pltpu.VMEM((T // 2, T // 2), F32)])
    if lc is not None:
        in_specs.append(pl.BlockSpec((1, 2) + lc.shape[2:], lambda b, p: (b, p, 0, 0, 0)))
        args.append(lc)
        scratch += [pltpu.SMEM((len(steps),), jnp.int32)] * 2
    else:
        smem = pl.BlockSpec(memory_space=pltpu.SMEM)
        in_specs = [smem, smem] + in_specs
        args = [jnp.asarray(steps[:, 0]), jnp.asarray(steps[:, 1])] + args
    return pl.pallas_call(
        functools.partial(_flash_kernel, head_lanes=head_lanes, has_bias=lc is not None),
        grid=(B, n_pairs),
        in_specs=in_specs,
        out_specs=pl.BlockSpec((1, S, LANES), lambda b, p: (b, 0, p)),
        out_shape=jax.ShapeDtypeStruct((B, S, n_pairs * LANES), BF16),
        scratch_shapes=scratch,
        compiler_params=_params(2),
        name="fox" if lc is not None else "mla",
    )(*args)


def _swa_kernel(sinks_ref, bias_ref, qs_ref, kvs_ref, o_ref):
    W = WINDOW
    group = SWA_HEADS // SWA_KV_HEADS
    step = pl.program_id(1)
    lane = lax.broadcasted_iota(jnp.int32, (W, LANES), 1)
    lo_half = lane < HALF
    zero = jnp.zeros((), BF16)

    for blk in range(SWA_ROWS // W):
        t0 = step * SWA_ROWS + blk * W
        k_start = pl.multiple_of(jnp.maximum(t0 - W, 0), W)
        bias_sel = jnp.where(step == 0, 0, 1) if blk == 0 else 1
        rows = slice(blk * W, (blk + 1) * W)
        for g in range(SWA_KV_HEADS):
            k_t = kvs_ref[0, pl.ds(k_start, 2 * W), g * LANES:(g + 1) * LANES]
            v_t = kvs_ref[0, pl.ds(k_start, 2 * W), (2 + g) * LANES:(3 + g) * LANES]
            q_parts = []
            for pair in range(group // 2):
                q_pair = qs_ref[0, rows, (g * 2 + pair) * LANES:(g * 2 + pair + 1) * LANES]
                q_parts.append(jnp.where(lo_half, q_pair, zero))
                q_parts.append(jnp.where(lo_half, zero, q_pair))
            s_all = _dot_nt(jnp.concatenate(q_parts, axis=0), k_t)
            p_parts, denoms = [], []
            for hq in range(group):
                head = g * group + hq
                sink = sinks_ref[head] * LOG2E
                s = s_all[hq * W:(hq + 1) * W] + bias_ref[bias_sel, head]
                m = jnp.maximum(jnp.max(s, axis=1, keepdims=True), sink)
                p = jnp.exp2(s - m)
                denoms.append(jnp.sum(p, axis=1, keepdims=True) + jnp.exp2(sink - m))
                p_parts.append(p.astype(BF16))
            o_all = _dot(jnp.concatenate(p_parts, axis=0), v_t)
            for pair in range(group // 2):
                o_even = o_all[(2 * pair) * W:(2 * pair + 1) * W] / denoms[2 * pair]
                o_odd = o_all[(2 * pair + 1) * W:(2 * pair + 2) * W] / denoms[2 * pair + 1]
                o_ref[0, rows, (g * 2 + pair) * LANES:(g * 2 + pair + 1) * LANES] = jnp.where(
                    lo_half, o_even, o_odd).astype(o_ref.dtype)


def _swa_bias():
    W = WINDOW
    slopes = 2.0 ** (-8.0 * (np.arange(SWA_HEADS, dtype=np.float32) + 1.0) / SWA_HEADS)
    a = np.arange(W, dtype=np.float32)[:, None]
    c = np.arange(2 * W, dtype=np.float32)[None, :]
    tables = []
    for k_offset in (0.0, float(W)):
        dist = k_offset + a - c
        bias = -slopes.astype(np.float32)[:, None, None] * dist[None]
        tables.append(np.where((dist >= 0) & (dist < W), bias * np.float32(LOG2E), -np.inf).astype(np.float32))
    return jnp.asarray(np.stack(tables))


def _swa(sinks, qs, kvs):
    B, S, C = qs.shape
    bias = _swa_bias()
    return pl.pallas_call(
        _swa_kernel,
        grid=(B, S // SWA_ROWS),
        in_specs=[pl.BlockSpec(memory_space=pltpu.SMEM), _resident(bias.shape),
                  pl.BlockSpec((1, SWA_ROWS, C), lambda b, i: (b, i, 0)),
                  pl.BlockSpec((1, S, kvs.shape[2]), lambda b, i: (b, 0, 0))],
        out_specs=pl.BlockSpec((1, SWA_ROWS, C), lambda b, i: (b, i, 0)),
        out_shape=jax.ShapeDtypeStruct((B, S, C), BF16),
        compiler_params=_params(2),
        name="swa",
    )(sinks, bias, qs, kvs)


def _mid_kernel(x_ref, om_ref, os_ref, gate_ref, w_out_ref, g_in_ref, w_in_ref, w_f_ref,
                x1_ref, qk_ref, v_ref, g1_ref, f_ref, h_scr):
    @pl.when(pl.program_id(0) == 0)
    def _():
        h_scr[...] = jnp.zeros(h_scr.shape, BF16)

    h = h_scr[...]
    w = FOX_WIDTH
    qk_ref[0, :, :w] = (_dot(h, w_in_ref[:, :w]) * LOG2E).astype(BF16)
    qk_ref[0, :, w:] = _dot(h, w_in_ref[:, w:2 * w]).astype(BF16)
    g1_ref[0] = _dot(h, w_in_ref[:, 3 * w:]).astype(BF16)
    f_ref[0] = _dot(h, w_f_ref[...])
    v = _dot(h, w_in_ref[:, 2 * w:3 * w])
    lo_half = lax.broadcasted_iota(jnp.int32, (v.shape[0], LANES), 1) < HALF
    for pair in range(FOX_HEADS // 2):
        v_pair = v[:, pair * LANES:(pair + 1) * LANES]
        v_ref[0, :, (2 * pair) * LANES:(2 * pair + 1) * LANES] = jnp.where(lo_half, v_pair, 1.0).astype(BF16)
        v_ref[0, :, (2 * pair + 1) * LANES:(2 * pair + 2) * LANES] = jnp.where(lo_half, 1.0, v_pair).astype(BF16)

    half = D_MODEL // 2
    gate = _silu(gate_ref[0].astype(F32))
    a_m = (om_ref[0].astype(F32) * gate[:, :half]).astype(BF16)
    a_s = (os_ref[0].astype(F32) * gate[:, half:]).astype(BF16)
    x1 = x_ref[0] + (_dot(a_m, w_out_ref[:half, :]) + _dot(a_s, w_out_ref[half:, :]))
    x1_ref[0] = x1
    h_scr[...] = _rms(x1, g_in_ref[...]).astype(BF16)


def _mid(x, o_mla, o_swa, gate, w_out, g_in, w_in, w_f):
    B, S, D = x.shape
    tm = PROJ_ROWS
    nt = S // tm
    n_tiles = B * nt

    def this_tile(g):
        t = jnp.minimum(g, n_tiles - 1)
        return t // nt, t % nt, 0

    def prev_tile(g):
        t = jnp.maximum(g - 1, 0)
        return t // nt, t % nt, 0

    outs = [(D, F32), (2 * FOX_WIDTH, BF16), (FOX_HEADS * LANES, BF16), (FOX_WIDTH, BF16), (LANES, F32)]
    return pl.pallas_call(
        _mid_kernel,
        grid=(n_tiles + 1,),
        in_specs=[pl.BlockSpec((1, tm, width), this_tile) for width in (D, D // 2, D // 2, D)]
        + [_resident(a.shape) for a in (w_out, g_in, w_in, w_f)],
        out_specs=[pl.BlockSpec((1, tm, outs[0][0]), this_tile)]
        + [pl.BlockSpec((1, tm, width), prev_tile) for width, _ in outs[1:]],
        out_shape=[jax.ShapeDtypeStruct((B, S, width), dtype) for width, dtype in outs],
        scratch_shapes=[pltpu.VMEM((tm, D), BF16)],
        compiler_params=pltpu.CompilerParams(dimension_semantics=("arbitrary",),
                                             vmem_limit_bytes=V7X_VMEM_LIMIT_BYTES),
        name="mid",
    )(x, o_mla, o_swa, gate, w_out, g_in, w_in, w_f)


def _split3(x):
    hi = x.astype(BF16)
    r1 = x - hi.astype(F32)
    mid = r1.astype(BF16)
    lo = (r1 - mid.astype(F32)).astype(BF16)
    return hi, mid, lo


def _lcum_kernel(f_ref, b_ref, lc_ref):
    S = f_ref.shape[1]
    log_f = jax.nn.log_sigmoid(f_ref[0] + b_ref[...])
    log_f_t = log_f.T[:FOX_HEADS]
    n = CUM_CHUNK
    tri = (lax.broadcasted_iota(jnp.int32, (n, n), 0) <= lax.broadcasted_iota(jnp.int32, (n, n), 1)).astype(BF16)
    carry = jnp.zeros((FOX_HEADS, 1), F32)
    for c in range(S // n):
        hi, mid, lo = _split3(log_f_t[:, c * n:(c + 1) * n])
        cs = (_dot(hi, tri) + _dot(mid, tri)) + _dot(lo, tri) + carry
        cs2 = cs * LOG2E
        tile, lane0 = divmod(c * n, ATTN_TILE)
        for head in range(FOX_HEADS):
            lc_ref[0, head, tile, :, lane0:lane0 + n] = cs2[head:head + 1]
        carry = cs[:, n - 1:n]


def _lcum(f, b_pad):
    B, S, _ = f.shape
    out_block = (1, FOX_HEADS, S // ATTN_TILE, 1, ATTN_TILE)
    return pl.pallas_call(
        _lcum_kernel,
        grid=(B,),
        in_specs=[pl.BlockSpec((1, S, LANES), lambda b: (b, 0, 0)), _resident(b_pad.shape)],
        out_specs=pl.BlockSpec(out_block, lambda b: (b, 0, 0, 0, 0)),
        out_shape=jax.ShapeDtypeStruct((B,) + out_block[1:], F32),
        compiler_params=_params(1),
        name="lcum",
    )(f, b_pad)


def _final_kernel(x1_ref, o_ref, gate_ref, w_out_ref, g_ref, y_ref):
    a = (o_ref[0].astype(F32) * _silu(gate_ref[0].astype(F32))).astype(BF16)
    x2 = x1_ref[0] + _dot(a, w_out_ref[...])
    y_ref[0] = _rms(x2, g_ref[...])


def _final(x1, o_fox, gate, w_out, g_final):
    B, S, D = x1.shape
    tm = FINAL_ROWS
    row = pl.BlockSpec((1, tm, D), lambda b, i: (b, i, 0))
    return pl.pallas_call(
        _final_kernel,
        grid=(B, S // tm),
        in_specs=[row, row, row, _resident(w_out.shape), _resident(g_final.shape)],
        out_specs=row,
        out_shape=jax.ShapeDtypeStruct((B, S, D), F32),
        compiler_params=_params(2),
        name="final",
    )(x1, o_fox, gate, w_out, g_final)


def _pad_cols(w, left, total):
    return jnp.pad(w, ((0, 0), (left, total - left - w.shape[1])))


def _pad_rows(w, top, total):
    return jnp.pad(w, ((top, total - top - w.shape[0]), (0, 0)))


def _layer0_weights(w_in, w_q_up, w_kv_up):
    sizes = (MLA_Q_RANK, MLA_KV_RANK, MLA_ROPE, SWA_HEADS * SWA_HEAD_DIM,
             SWA_KV_HEADS * SWA_HEAD_DIM, SWA_KV_HEADS * SWA_HEAD_DIM, D_MODEL)
    cq, ckv, kpe, qs, ks, vs, gate = jnp.split(w_in.T, np.cumsum(sizes)[:-1].tolist(), axis=0)
    half = MLA_ROPE // 2
    kpe_sw = jnp.concatenate([kpe[half:], kpe[:half]], axis=0)
    a_t = jnp.concatenate([cq, ckv, _pad_rows(kpe, MLA_NOPE, LANES), _pad_rows(kpe_sw, MLA_NOPE, LANES)], axis=0)
    d = SWA_HEAD_DIM
    kvs_t = jnp.concatenate([ks[:d], ks[:d], ks[d:], ks[d:], vs[:d], vs[:d], vs[d:], vs[d:]], axis=0)
    swa_scale = SWA_HEAD_DIM ** -0.5
    w_a, w_qs, w_kvs, w_gate = [w_t.astype(BF16).T for w_t in (a_t, qs * swa_scale, kvs_t, gate)]
    qk = MLA_NOPE + MLA_ROPE
    wq3 = w_q_up.reshape(MLA_Q_RANK, MLA_HEADS, qk)
    w_q = jnp.pad(wq3, ((0, 0), (0, 0), (0, LANES - qk))).reshape(MLA_Q_RANK, MLA_HEADS * LANES)
    wq_sw = jnp.concatenate([wq3[:, :, MLA_NOPE + half:], wq3[:, :, MLA_NOPE:MLA_NOPE + half]], axis=2)
    w_qsw = jnp.pad(wq_sw, ((0, 0), (0, 0), (MLA_NOPE, LANES - qk))).reshape(MLA_Q_RANK, MLA_HEADS * LANES)
    wkv3 = w_kv_up.reshape(MLA_KV_RANK, MLA_HEADS, MLA_NOPE + MLA_V)
    w_k = jnp.pad(wkv3[:, :, :MLA_NOPE], ((0, 0), (0, 0), (0, LANES - MLA_NOPE))).reshape(MLA_KV_RANK, MLA_HEADS * LANES)
    wv4 = wkv3[:, :, MLA_NOPE:].reshape(MLA_KV_RANK, MLA_HEADS // 2, 2, MLA_V)
    zeros_v = jnp.zeros_like(wv4[:, :, 0])
    w_v = jnp.stack([wv4[:, :, 0], zeros_v, zeros_v, wv4[:, :, 1]], axis=2).reshape(MLA_KV_RANK, MLA_HEADS * LANES)
    return [w_a, w_qs, w_kvs, w_gate] + [w.astype(BF16) for w in (w_q, w_qsw, w_k, w_v)]


def _rope_table():
    half = MLA_ROPE // 2
    inv_freq = 1.0 / (ROPE_THETA ** (jnp.arange(0, MLA_ROPE, 2, dtype=F32) / MLA_ROPE))
    t = jnp.zeros((8, LANES), F32)
    t = t.at[0, MLA_NOPE:MLA_NOPE + half].set(inv_freq).at[0, MLA_NOPE + half:MLA_NOPE + MLA_ROPE].set(inv_freq)
    t = t.at[1, MLA_NOPE:MLA_NOPE + MLA_ROPE].set(1.0)
    t = t.at[2, MLA_NOPE:MLA_NOPE + half].set(-1.0).at[2, MLA_NOPE + half:MLA_NOPE + MLA_ROPE].set(1.0)
    t = t.at[3, :MLA_NOPE].set(1.0)
    return t


def kernel(x, positions, e_g_in, e_w_in, e_g_q_a, e_w_q_up, e_g_kv_a, e_w_kv_up, e_sinks,
           e_w_out, o_g_in, o_w_in, o_b_f, o_w_out, g_final):
    B, S, D = x.shape
    assert D == D_MODEL and all(S % rows == 0 for rows in (PROJ_ROWS, FINAL_ROWS, ATTN_TILE, SWA_ROWS))
    row = lambda g: g.reshape(1, -1).astype(F32)

    w0 = _layer0_weights(e_w_in[0], e_w_q_up[0], e_w_kv_up[0])
    w_a, w_qs, w_kvs, w_gate, w_q, w_qsw, w_k, w_v = w0
    qm, km, vm, qs, kvs, gate0 = _proj0(
        x, positions.reshape(B, S // PROJ_ROWS, 1, PROJ_ROWS), row(e_g_in[0]), w_a, w_qs, w_kvs, w_gate,
        row(e_g_q_a[0]), w_q, w_qsw, row(e_g_kv_a[0]), w_k, w_v, _rope_table())
    o_mla = _flash(qm, km, vm, None, q_col0=0, k_col0=0, v_col0=0,
                   n_pairs=MLA_HEADS // 2, head_lanes=LANES)
    o_swa = _swa(e_sinks[0].astype(F32), qs, kvs)

    sizes = (FOX_WIDTH, FOX_WIDTH, FOX_WIDTH, FOX_HEADS, FOX_WIDTH)
    w1_q, w1_k, w1_v, w1_f, w1_gate = jnp.split(o_w_in[0].T, np.cumsum(sizes)[:-1].tolist(), axis=0)
    fox_scale = FOX_HEAD_DIM ** -0.5
    w1 = jnp.concatenate([w1_q * fox_scale, w1_k, w1_v, w1_gate], axis=0).astype(BF16).T
    w1_f_pad = _pad_rows(w1_f, 0, LANES).astype(BF16).T
    x1, qk1, v1, gate1, f_logit = _mid(x, o_mla, o_swa, gate0, e_w_out[0].astype(BF16), row(o_g_in[0]), w1, w1_f_pad)

    lc = _lcum(f_logit, _pad_cols(row(o_b_f[0]), 0, LANES))
    o_fox = _flash(qk1, qk1, v1, lc, q_col0=0, k_col0=FOX_WIDTH // LANES, v_col0=0,
                   n_pairs=FOX_HEADS // 2, head_lanes=HALF)
    return _final(x1, o_fox, gate1, o_w_out[0].astype(BF16), row(g_final))
```

```python
import functools

import jax
import jax.numpy as jnp
import numpy as np
from jax import lax
from jax.experimental import pallas as pl
from jax.experimental.pallas import tpu as pltpu

F32 = jnp.float32
BF16 = jnp.bfloat16

D_MODEL = 1024
EPS = 1e-6
ROPE_THETA = 10000.0
WINDOW = 128

MLA_HEADS = 8
MLA_Q_RANK = 256
MLA_KV_RANK = 128
MLA_NOPE = 64
MLA_ROPE = 32
MLA_V = 64
SWA_HEADS = 8
SWA_KV_HEADS = 2
SWA_HEAD_DIM = 64
FOX_HEADS = 16
FOX_HEAD_DIM = 64
FOX_WIDTH = FOX_HEADS * FOX_HEAD_DIM

LANES = 128
HALF = 64
V7X_VMEM_LIMIT_BYTES = 56 * 1024 * 1024

PROJ_ROWS = 512
FINAL_ROWS = 1024
ATTN_TILE = 512
SWA_ROWS = 512
CUM_CHUNK = 256
LOG2E = 1.4426950408889634
EXP2_UNDERFLOW = 152.0
NORM_SLACK = 1.01
BOUND_SLACK = 2.0
MIN_FULL_TILES = 4


def _dot(a, b):
    return jnp.dot(a, b, preferred_element_type=F32)


def _dot_nt(a, b):
    return lax.dot_general(a, b, (((1,), (1,)), ((), ())), preferred_element_type=F32)


def _rms(xf, g):
    ms = jnp.mean(xf * xf, axis=-1, keepdims=True)
    return xf * lax.rsqrt(ms + EPS) * g


def _silu(x):
    return x * (1.0 / (1.0 + jnp.exp(-x)))


def _params(n_parallel):
    return pltpu.CompilerParams(
        dimension_semantics=("parallel",) * n_parallel,
        vmem_limit_bytes=V7X_VMEM_LIMIT_BYTES,
    )


def _resident(shape):
    nd = len(shape)
    return pl.BlockSpec(shape, lambda *_: (0,) * nd, pipeline_mode=pl.Buffered(1))


def _proj0_kernel(x_ref, pos_ref, g_in_ref, w_a_ref, w_qs_ref, w_kvs_ref, w_gate_ref,
                  g_q_ref, w_q_ref, w_qsw_ref, g_kv_ref, w_k_ref, w_v_ref, rope_ref,
                  qm_ref, km_ref, vm_ref, qs_ref, kvs_ref, gate_ref):
    h = _rms(x_ref[0], g_in_ref[...]).astype(BF16)

    za = _dot(h, w_a_ref[...])
    cqn = _rms(za[:, :MLA_Q_RANK], g_q_ref[...]).astype(BF16)
    ckvn = _rms(za[:, MLA_Q_RANK:MLA_Q_RANK + MLA_KV_RANK], g_kv_ref[...]).astype(BF16)
    kpe = za[:, 384:512]
    kpe_sw = za[:, 512:640]

    pos_row = pos_ref[0, 0].astype(F32)
    pos_col = jnp.broadcast_to(pos_row, (LANES, pos_row.shape[1])).T
    ang = pos_col * rope_ref[0:1, :]
    cos_t = jnp.cos(ang) * rope_ref[1:2, :]
    sin_t = jnp.sin(ang) * rope_ref[2:3, :]
    scale = (MLA_NOPE + MLA_ROPE) ** -0.5 * LOG2E
    q_cos = (cos_t + rope_ref[3:4, :]) * scale
    q_sin = sin_t * scale
    kpe_rot = kpe * cos_t + kpe_sw * sin_t

    for c in range(MLA_HEADS // 2):
        cols = slice(c * 2 * LANES, (c + 1) * 2 * LANES)
        q = _dot(cqn, w_q_ref[:, cols])
        q_sw = _dot(cqn, w_qsw_ref[:, cols])
        kn = _dot(ckvn, w_k_ref[:, cols])
        for hh in range(2):
            lo = hh * LANES
            qm_ref[0, :, c * 2 * LANES + lo:c * 2 * LANES + lo + LANES] = (
                q[:, lo:lo + LANES] * q_cos + q_sw[:, lo:lo + LANES] * q_sin).astype(BF16)
            km_ref[0, :, c * 2 * LANES + lo:c * 2 * LANES + lo + LANES] = (
                kn[:, lo:lo + LANES] + kpe_rot).astype(BF16)
    v = _dot(ckvn, w_v_ref[...])
    lane_v = lax.broadcasted_iota(jnp.int32, v.shape, 1)
    ones_half = (((lane_v >> 6) ^ (lane_v >> 7)) & 1) == 1
    vm_ref[0] = jnp.where(ones_half, 1.0, v).astype(BF16)
    qs_ref[0] = (_dot(h, w_qs_ref[...]) * LOG2E).astype(BF16)
    kvs_ref[0] = _dot(h, w_kvs_ref[...]).astype(BF16)
    gate_ref[0] = _dot(h, w_gate_ref[...]).astype(BF16)


def _pipelined_tiles(n_batch, tiles_per_batch):
    n_tiles = n_batch * tiles_per_batch

    def this_tile(g):
        t = jnp.minimum(g, n_tiles - 1)
        return t // tiles_per_batch, t % tiles_per_batch

    def prev_tile(g):
        t = jnp.maximum(g - 1, 0)
        return t // tiles_per_batch, t % tiles_per_batch

    params = pltpu.CompilerParams(dimension_semantics=("arbitrary",), vmem_limit_bytes=V7X_VMEM_LIMIT_BYTES)
    return n_tiles + 1, this_tile, prev_tile, params


def _proj0(x, pos4, g_in, w_a, w_qs, w_kvs, w_gate, g_q, w_q, w_qsw, g_kv, w_k, w_v, rope):
    B, S, D = x.shape
    tm = PROJ_ROWS
    row = lambda width: pl.BlockSpec((1, tm, width), lambda b, i: (b, i, 0))
    out_widths = (MLA_HEADS * LANES, MLA_HEADS * LANES, MLA_HEADS * LANES,
                  SWA_HEADS * SWA_HEAD_DIM, 4 * LANES, D)
    return pl.pallas_call(
        _proj0_kernel,
        grid=(B, S // tm),
        in_specs=[row(D), pl.BlockSpec((1, 1, 1, tm), lambda b, i: (b, i, 0, 0))]
        + [_resident(a.shape) for a in (g_in, w_a, w_qs, w_kvs, w_gate, g_q, w_q, w_qsw, g_kv, w_k, w_v, rope)],
        out_specs=[row(w) for w in out_widths],
        out_shape=[jax.ShapeDtypeStruct((B, S, w), BF16) for w in out_widths],
        compiler_params=_params(2),
        name="proj0",
    )(x, pos4, g_in, w_a, w_qs, w_kvs, w_gate, g_q, w_q, w_qsw, g_kv, w_k, w_v, rope)


def _fox_schedule(q_ref, k_ref, lc_ref, ti_ref, tj_ref, nq):
    T = ATTN_TILE
    q_all, k_all = q_ref[0], k_ref[0]
    prods = jnp.concatenate([q_all * q_all, k_all * k_all, q_all * k_all], axis=1)
    d_id = lax.broadcasted_iota(jnp.int32, (3 * LANES, LANES), 0)
    c_id = lax.broadcasted_iota(jnp.int32, (3 * LANES, LANES), 1)
    head_cols = (c_id == 2 * (d_id >> 7) + ((d_id >> 6) & 1)).astype(BF16)
    sums = _dot(prods, head_cols)
    k_max = jnp.max(sums, axis=0, keepdims=True)[:, 2:4]
    j_lo = [0]
    for i in range(1, nq):
        r = slice(i * T, (i + 1) * T)
        u = (NORM_SLACK * jnp.sqrt(jnp.max(sums[r], axis=0, keepdims=True)[:, 0:2] * k_max)
             - jnp.min(sums[r], axis=0, keepdims=True)[:, 4:6] + BOUND_SLACK)
        count = jnp.zeros((1, 1), jnp.int32)
        for j in range(i):
            skip = None
            for hh in range(2):
                bound = u[:, hh:hh + 1] + lc_ref[0, hh, i][:, 0:1] - lc_ref[0, hh, j][:, T - 1:T]
                skip_h = bound < -EXP2_UNDERFLOW
                skip = skip_h if skip is None else skip & skip_h
            count = count + skip.astype(jnp.int32)
        j_lo.append(count[0, 0])

    for i in range(nq):
        ti_ref[i] = jnp.int32(i)
        tj_ref[i] = jnp.int32(i)
    pos = jnp.int32(0)
    pad_i = pad_j = jnp.int32(0)
    for i in range(1, nq):
        for j in range(i):
            keep = j >= j_lo[i]

            @pl.when(keep)
            def _(i=i, j=j, pos=pos):
                ti_ref[nq + pos] = jnp.int32(i)
                tj_ref[nq + pos] = jnp.int32(j)

            pos = pos + keep.astype(jnp.int32)
            pad_i = jnp.where(keep, pad_i, i)
            pad_j = jnp.where(keep, pad_j, j)
    n_full = jnp.maximum(((pos + 1) >> 1) << 1, MIN_FULL_TILES)
    for extra in range(MIN_FULL_TILES):
        @pl.when(pos + extra < n_full)
        def _(extra=extra):
            ti_ref[nq + pos + extra] = pad_i
            tj_ref[nq + pos + extra] = pad_j
    return n_full


def _flash_kernel(*refs, head_lanes, has_bias):
    T = ATTN_TILE
    if has_bias:
        q_ref, k_ref, v_ref, lc_ref, o_ref = refs[:5]
        scr = refs[5:-2]
        ti_ref, tj_ref = refs[-2:]
        nq = q_ref.shape[1] // T
        n_full = _fox_schedule(q_ref, k_ref, lc_ref, ti_ref, tj_ref, nq)
        n_bodies = (n_full >> 2) + 1
        odd_pair = (n_full & 2) != 0
        last_step = nq + n_full - 1
    else:
        ti_ref, tj_ref, q_ref, k_ref, v_ref, o_ref = refs[:6]
        lc_ref = None
        scr = refs[6:]
        nq = q_ref.shape[1] // T
        assert (ti_ref.shape[0] - 4) % 4 == 0
        n_bodies = (ti_ref.shape[0] - 4) // 4
        odd_pair = None
        last_step = ti_ref.shape[0] - 1
    p_scr, alpha_scr = scr[0:2], scr[2:4]
    m_scr, acc_scr, mask_scr = scr[4:]
    lane = lax.broadcasted_iota(jnp.int32, (T, LANES), 1)
    lo_half = lane < HALF
    zero = jnp.zeros((), BF16)

    def rows(i):
        return pl.ds(pl.multiple_of(i * T, T), T)

    H = T // 2
    full_blocks = ((0, T, T),)
    diag_blocks = ((0, H, H), (H, H, T))

    def softmax(t, slot, diag):
        i, j = ti_ref[t], tj_ref[t]
        if head_lanes == HALF:
            q_pair = q_ref[0, rows(i), :]
            q_heads = [jnp.where(lo_half, q_pair, zero), jnp.where(lo_half, zero, q_pair)]
        for hh in range(2):
            for r0, nr, nc in (diag_blocks if diag else full_blocks):
                kr = pl.ds(pl.multiple_of(j * T, T), nc)
                if head_lanes == LANES:
                    q_h = q_ref[0, pl.ds(pl.multiple_of(i * T + r0, H), nr), hh * LANES:(hh + 1) * LANES]
                    k_t = k_ref[0, kr, hh * LANES:(hh + 1) * LANES]
                else:
                    q_h = q_heads[hh][r0:r0 + nr]
                    k_t = k_ref[0, kr, :]
                s = _dot_nt(q_h, k_t)
                if has_bias:
                    s = s + (lc_ref[0, hh, i][:, 0:1] - lc_ref[0, hh, j][:, :nc])
                if diag:
                    tri = s[:, nc - H:] + mask_scr[...]
                    s = tri if nc == H else jnp.concatenate([s[:, :nc - H], tri], axis=1)
                state_rows = pl.ds(pl.multiple_of(i * T + r0, H), nr)
                m_cur = jnp.max(s, axis=1, keepdims=True)
                if diag:
                    m_next = jnp.broadcast_to(m_cur, (nr, LANES))
                else:
                    m_prev = m_scr[hh, state_rows, :]
                    m_next = jnp.maximum(m_prev, m_cur)
                    alpha_scr[slot][hh, r0:r0 + nr, :] = jnp.exp2(m_prev - m_next)
                p = jnp.exp2(s - jnp.concatenate([m_next] * (nc // LANES), axis=1))
                p_scr[slot][hh, r0:r0 + nr, :nc] = p.astype(BF16)
                m_scr[hh, state_rows, :] = m_next

    def accumulate(t, slot, diag):
        i, j = ti_ref[t], tj_ref[t]
        for hh in range(2):
            for r0, nr, nc in (diag_blocks if diag else full_blocks):
                state_rows = pl.ds(pl.multiple_of(i * T + r0, H), nr)
                v_t = v_ref[0, pl.ds(pl.multiple_of(j * T, T), nc), hh * LANES:(hh + 1) * LANES]
                pv = _dot(p_scr[slot][hh, r0:r0 + nr, :nc], v_t)
                if diag:
                    acc_scr[hh, state_rows, :] = pv
                else:
                    acc_scr[hh, state_rows, :] = (
                        alpha_scr[slot][hh, r0:r0 + nr, :] * acc_scr[hh, state_rows, :] + pv)

    row_id = lax.broadcasted_iota(jnp.int32, (H, H), 0)
    col_id = lax.broadcasted_iota(jnp.int32, (H, H), 1)
    mask_scr[...] = jnp.where(col_id <= row_id, 0.0, -jnp.inf).astype(F32)

    def is_diag(step):
        return step < nq

    def step_pair(t, t_static):
        softmax(t, 1, is_diag(t_static))
        accumulate(t - 1, 0, is_diag(t_static - 1))
        softmax(t + 1, 0, is_diag(t_static + 1))
        accumulate(t, 1, is_diag(t_static))

    def make_body(u_static):
        def body(u, carry):
            step_pair(4 * u + 3, 4 * u_static + 3)
            step_pair(4 * u + 5, 4 * u_static + 5)
            return carry
        return body

    assert nq % 4 == 0 and MIN_FULL_TILES == 4
    n_diag_bodies = (nq - 4) // 4
    softmax(0, 0, True)
    step_pair(1, 1)
    lax.fori_loop(0, n_diag_bodies, make_body(0), 0)
    make_body(n_diag_bodies)(n_diag_bodies, 0)
    lax.fori_loop(n_diag_bodies + 1, n_bodies, make_body(n_diag_bodies + 1), 0)
    if odd_pair is not None:
        @pl.when(odd_pair)
        def _():
            step_pair(4 * n_bodies + 3, 4 * (n_diag_bodies + 1) + 3)
    softmax(last_step, 1, False)
    accumulate(last_step - 1, 0, False)
    accumulate(last_step, 1, False)

    for i in range(nq):
        r = pl.ds(i * T, T)
        acc_even, acc_odd = acc_scr[0, r, :], acc_scr[1, r, :]
        num = jnp.where(lo_half, acc_even, acc_odd)
        den = pltpu.roll(jnp.where(lo_half, acc_odd, acc_even), HALF, 1)
        o_ref[0, r, :] = (num / den).astype(o_ref.dtype)


def _tile_schedule(nq):
    steps = [(i, i) for i in range(nq)] + [(i, j) for i in range(nq) for j in range(i)]
    assert (len(steps) - 4) % 4 == 0 and len(steps) - nq >= MIN_FULL_TILES
    return np.array(steps, np.int32)


def _flash(q, k, v, lc, *, q_col0, k_col0, v_col0, n_pairs, head_lanes):
    B, S, _ = q.shape
    T = ATTN_TILE
    qk_w = 2 * head_lanes
    qk_step = qk_w // LANES
    steps = _tile_schedule(S // T)
    in_specs = [
        pl.BlockSpec((1, S, qk_w), lambda b, p: (b, 0, q_col0 // qk_step + p)),
        pl.BlockSpec((1, S, qk_w), lambda b, p: (b, 0, k_col0 // qk_step + p)),
        pl.BlockSpec((1, S, 2 * LANES), lambda b, p: (b, 0, v_col0 // 2 + p)),
    ]
    args = [q, k, v]
    per_slot = lambda shape, dtype: [pltpu.VMEM(shape, dtype)] * 2
    scratch = (per_slot((2, T, T), BF16) + per_slot((2, T, LANES), F32)
               + [pltpu.VMEM((2, S, LANES), F32)] * 2
               + [pltpu.VMEM((T // 2, T // 2), F32)])
    if lc is not None:
        in_specs.append(pl.BlockSpec((1, 2) + lc.shape[2:], lambda b, p: (b, p, 0, 0, 0)))
        args.append(lc)
        scratch += [pltpu.SMEM((len(steps),), jnp.int32)] * 2
    else:
        smem = pl.BlockSpec(memory_space=pltpu.SMEM)
        in_specs = [smem, smem] + in_specs
        args = [jnp.asarray(steps[:, 0]), jnp.asarray(steps[:, 1])] + args
    return pl.pallas_call(
        functools.partial(_flash_kernel, head_lanes=head_lanes, has_bias=lc is not None),
        grid=(B, n_pairs),
        in_specs=in_specs,
        out_specs=pl.BlockSpec((1, S, LANES), lambda b, p: (b, 0, p)),
        out_shape=jax.ShapeDtypeStruct((B, S, n_pairs * LANES), BF16),
        scratch_shapes=scratch,
        compiler_params=_params(2),
        name="fox" if lc is not None else "mla",
    )(*args)


def _swa_kernel(sinks_ref, bias_ref, qs_ref, kvs_ref, o_ref):
    W = WINDOW
    group = SWA_HEADS // SWA_KV_HEADS
    step = pl.program_id(1)
    lane = lax.broadcasted_iota(jnp.int32, (W, LANES), 1)
    lo_half = lane < HALF
    zero = jnp.zeros((), BF16)

    for blk in range(SWA_ROWS // W):
        t0 = step * SWA_ROWS + blk * W
        k_start = pl.multiple_of(jnp.maximum(t0 - W, 0), W)
        bias_sel = jnp.where(step == 0, 0, 1) if blk == 0 else 1
        rows = slice(blk * W, (blk + 1) * W)
        for g in range(SWA_KV_HEADS):
            k_t = kvs_ref[0, pl.ds(k_start, 2 * W), g * LANES:(g + 1) * LANES]
            v_t = kvs_ref[0, pl.ds(k_start, 2 * W), (2 + g) * LANES:(3 + g) * LANES]
            q_parts = []
            for pair in range(group // 2):
                q_pair = qs_ref[0, rows, (g * 2 + pair) * LANES:(g * 2 + pair + 1) * LANES]
                q_parts.append(jnp.where(lo_half, q_pair, zero))
                q_parts.append(jnp.where(lo_half, zero, q_pair))
            s_all = _dot_nt(jnp.concatenate(q_parts, axis=0), k_t)
            p_parts, denoms = [], []
            for hq in range(group):
                head = g * group + hq
                sink = sinks_ref[head] * LOG2E
                s = s_all[hq * W:(hq + 1) * W] + bias_ref[bias_sel, head]
                m = jnp.maximum(jnp.max(s, axis=1, keepdims=True), sink)
                p = jnp.exp2(s - m)
                denoms.append(jnp.sum(p, axis=1, keepdims=True) + jnp.exp2(sink - m))
                p_parts.append(p.astype(BF16))
            o_all = _dot(jnp.concatenate(p_parts, axis=0), v_t)
            for pair in range(group // 2):
                o_even = o_all[(2 * pair) * W:(2 * pair + 1) * W] / denoms[2 * pair]
                o_odd = o_all[(2 * pair + 1) * W:(2 * pair + 2) * W] / denoms[2 * pair + 1]
                o_ref[0, rows, (g * 2 + pair) * LANES:(g * 2 + pair + 1) * LANES] = jnp.where(
                    lo_half, o_even, o_odd).astype(o_ref.dtype)


def _swa_bias():
    W = WINDOW
    slopes = 2.0 ** (-8.0 * (np.arange(SWA_HEADS, dtype=np.float32) + 1.0) / SWA_HEADS)
    a = np.arange(W, dtype=np.float32)[:, None]
    c = np.arange(2 * W, dtype=np.float32)[None, :]
    tables = []
    for k_offset in (0.0, float(W)):
        dist = k_offset + a - c
        bias = -slopes.astype(np.float32)[:, None, None] * dist[None]
        tables.append(np.where((dist >= 0) & (dist < W), bias * np.float32(LOG2E), -np.inf).astype(np.float32))
    return jnp.asarray(np.stack(tables))


def _swa(sinks, qs, kvs):
    B, S, C = qs.shape
    bias = _swa_bias()
    return pl.pallas_call(
        _swa_kernel,
        grid=(B, S // SWA_ROWS),
        in_specs=[pl.BlockSpec(memory_space=pltpu.SMEM), _resident(bias.shape),
                  pl.BlockSpec((1, SWA_ROWS, C), lambda b, i: (b, i, 0)),
                  pl.BlockSpec((1, S, kvs.shape[2]), lambda b, i: (b, 0, 0))],
        out_specs=pl.BlockSpec((1, SWA_ROWS, C), lambda b, i: (b, i, 0)),
        out_shape=jax.ShapeDtypeStruct((B, S, C), BF16),
        compiler_params=_params(2),
        name="swa",
    )(sinks, bias, qs, kvs)


def _mid_kernel(x_ref, om_ref, os_ref, gate_ref, w_out_ref, g_in_ref, w_in_ref, w_f_ref,
                x1_ref, qk_ref, v_ref, g1_ref, f_ref, h_scr):
    @pl.when(pl.program_id(0) == 0)
    def _():
        h_scr[...] = jnp.zeros(h_scr.shape, BF16)

    h = h_scr[...]
    w = FOX_WIDTH
    qk_ref[0, :, :w] = (_dot(h, w_in_ref[:, :w]) * LOG2E).astype(BF16)
    qk_ref[0, :, w:] = _dot(h, w_in_ref[:, w:2 * w]).astype(BF16)
    g1_ref[0] = _dot(h, w_in_ref[:, 3 * w:]).astype(BF16)
    f_ref[0] = _dot(h, w_f_ref[...])
    v = _dot(h, w_in_ref[:, 2 * w:3 * w])
    lo_half = lax.broadcasted_iota(jnp.int32, (v.shape[0], LANES), 1) < HALF
    for pair in range(FOX_HEADS // 2):
        v_pair = v[:, pair * LANES:(pair + 1) * LANES]
        v_ref[0, :, (2 * pair) * LANES:(2 * pair + 1) * LANES] = jnp.where(lo_half, v_pair, 1.0).astype(BF16)
        v_ref[0, :, (2 * pair + 1) * LANES:(2 * pair + 2) * LANES] = jnp.where(lo_half, 1.0, v_pair).astype(BF16)

    half = D_MODEL // 2
    gate = _silu(gate_ref[0].astype(F32))
    a_m = (om_ref[0].astype(F32) * gate[:, :half]).astype(BF16)
    a_s = (os_ref[0].astype(F32) * gate[:, half:]).astype(BF16)
    x1 = x_ref[0] + (_dot(a_m, w_out_ref[:half, :]) + _dot(a_s, w_out_ref[half:, :]))
    x1_ref[0] = x1
    h_scr[...] = _rms(x1, g_in_ref[...]).astype(BF16)


def _mid(x, o_mla, o_swa, gate, w_out, g_in, w_in, w_f):
    B, S, D = x.shape
    tm = PROJ_ROWS
    n_steps, this_tile, prev_tile, params = _pipelined_tiles(B, S // tm)
    this_spec = lambda width: pl.BlockSpec((1, tm, width), lambda g: this_tile(g) + (0,))
    prev_spec = lambda width: pl.BlockSpec((1, tm, width), lambda g: prev_tile(g) + (0,))
    outs = [(D, F32), (2 * FOX_WIDTH, BF16), (FOX_HEADS * LANES, BF16), (FOX_WIDTH, BF16), (LANES, F32)]
    return pl.pallas_call(
        _mid_kernel,
        grid=(n_steps,),
        in_specs=[this_spec(width) for width in (D, D // 2, D // 2, D)]
        + [_resident(a.shape) for a in (w_out, g_in, w_in, w_f)],
        out_specs=[this_spec(outs[0][0])] + [prev_spec(width) for width, _ in outs[1:]],
        out_shape=[jax.ShapeDtypeStruct((B, S, width), dtype) for width, dtype in outs],
        scratch_shapes=[pltpu.VMEM((tm, D), BF16)],
        compiler_params=params,
        name="mid",
    )(x, o_mla, o_swa, gate, w_out, g_in, w_in, w_f)


def _split3(x):
    hi = x.astype(BF16)
    r1 = x - hi.astype(F32)
    mid = r1.astype(BF16)
    lo = (r1 - mid.astype(F32)).astype(BF16)
    return hi, mid, lo


def _lcum_kernel(f_ref, b_ref, lc_ref):
    S = f_ref.shape[1]
    log_f = jax.nn.log_sigmoid(f_ref[0] + b_ref[...])
    log_f_t = log_f.T[:FOX_HEADS]
    n = CUM_CHUNK
    tri = (lax.broadcasted_iota(jnp.int32, (n, n), 0) <= lax.broadcasted_iota(jnp.int32, (n, n), 1)).astype(BF16)
    carry = jnp.zeros((FOX_HEADS, 1), F32)
    for c in range(S // n):
        hi, mid, lo = _split3(log_f_t[:, c * n:(c + 1) * n])
        cs = (_dot(hi, tri) + _dot(mid, tri)) + _dot(lo, tri) + carry
        cs2 = cs * LOG2E
        tile, lane0 = divmod(c * n, ATTN_TILE)
        for head in range(FOX_HEADS):
            lc_ref[0, head, tile, :, lane0:lane0 + n] = cs2[head:head + 1]
        carry = cs[:, n - 1:n]


def _lcum(f, b_pad):
    B, S, _ = f.shape
    out_block = (1, FOX_HEADS, S // ATTN_TILE, 1, ATTN_TILE)
    return pl.pallas_call(
        _lcum_kernel,
        grid=(B,),
        in_specs=[pl.BlockSpec((1, S, LANES), lambda b: (b, 0, 0)), _resident(b_pad.shape)],
        out_specs=pl.BlockSpec(out_block, lambda b: (b, 0, 0, 0, 0)),
        out_shape=jax.ShapeDtypeStruct((B,) + out_block[1:], F32),
        compiler_params=_params(1),
        name="lcum",
    )(f, b_pad)


def _final_kernel(x1_ref, o_ref, gate_ref, w_out_ref, g_ref, y_ref):
    a = (o_ref[0].astype(F32) * _silu(gate_ref[0].astype(F32))).astype(BF16)
    x2 = x1_ref[0] + _dot(a, w_out_ref[...])
    y_ref[0] = _rms(x2, g_ref[...])


def _final(x1, o_fox, gate, w_out, g_final):
    B, S, D = x1.shape
    tm = FINAL_ROWS
    row = pl.BlockSpec((1, tm, D), lambda b, i: (b, i, 0))
    return pl.pallas_call(
        _final_kernel,
        grid=(B, S // tm),
        in_specs=[row, row, row, _resident(w_out.shape), _resident(g_final.shape)],
        out_specs=row,
        out_shape=jax.ShapeDtypeStruct((B, S, D), F32),
        compiler_params=_params(2),
        name="final",
    )(x1, o_fox, gate, w_out, g_final)


def _pad_cols(w, left, total):
    return jnp.pad(w, ((0, 0), (left, total - left - w.shape[1])))


def _pad_rows(w, top, total):
    return jnp.pad(w, ((top, total - top - w.shape[0]), (0, 0)))


def _layer0_weights(w_in, w_q_up, w_kv_up):
    sizes = (MLA_Q_RANK, MLA_KV_RANK, MLA_ROPE, SWA_HEADS * SWA_HEAD_DIM,
             SWA_KV_HEADS * SWA_HEAD_DIM, SWA_KV_HEADS * SWA_HEAD_DIM, D_MODEL)
    cq, ckv, kpe, qs, ks, vs, gate = jnp.split(w_in.T, np.cumsum(sizes)[:-1].tolist(), axis=0)
    half = MLA_ROPE // 2
    kpe_sw = jnp.concatenate([kpe[half:], kpe[:half]], axis=0)
    a_t = jnp.concatenate([cq, ckv, _pad_rows(kpe, MLA_NOPE, LANES), _pad_rows(kpe_sw, MLA_NOPE, LANES)], axis=0)
    d = SWA_HEAD_DIM
    kvs_t = jnp.concatenate([ks[:d], ks[:d], ks[d:], ks[d:], vs[:d], vs[:d], vs[d:], vs[d:]], axis=0)
    swa_scale = SWA_HEAD_DIM ** -0.5
    w_a, w_qs, w_kvs, w_gate = [w_t.astype(BF16).T for w_t in (a_t, qs * swa_scale, kvs_t, gate)]
    qk = MLA_NOPE + MLA_ROPE
    wq3 = w_q_up.reshape(MLA_Q_RANK, MLA_HEADS, qk)
    w_q = jnp.pad(wq3, ((0, 0), (0, 0), (0, LANES - qk))).reshape(MLA_Q_RANK, MLA_HEADS * LANES)
    wq_sw = jnp.concatenate([wq3[:, :, MLA_NOPE + half:], wq3[:, :, MLA_NOPE:MLA_NOPE + half]], axis=2)
    w_qsw = jnp.pad(wq_sw, ((0, 0), (0, 0), (MLA_NOPE, LANES - qk))).reshape(MLA_Q_RANK, MLA_HEADS * LANES)
    wkv3 = w_kv_up.reshape(MLA_KV_RANK, MLA_HEADS, MLA_NOPE + MLA_V)
    w_k = jnp.pad(wkv3[:, :, :MLA_NOPE], ((0, 0), (0, 0), (0, LANES - MLA_NOPE))).reshape(MLA_KV_RANK, MLA_HEADS * LANES)
    wv4 = wkv3[:, :, MLA_NOPE:].reshape(MLA_KV_RANK, MLA_HEADS // 2, 2, MLA_V)
    zeros_v = jnp.zeros_like(wv4[:, :, 0])
    w_v = jnp.stack([wv4[:, :, 0], zeros_v, zeros_v, wv4[:, :, 1]], axis=2).reshape(MLA_KV_RANK, MLA_HEADS * LANES)
    return [w_a, w_qs, w_kvs, w_gate] + [w.astype(BF16) for w in (w_q, w_qsw, w_k, w_v)]


def _rope_table():
    half = MLA_ROPE // 2
    inv_freq = 1.0 / (ROPE_THETA ** (jnp.arange(0, MLA_ROPE, 2, dtype=F32) / MLA_ROPE))
    t = jnp.zeros((8, LANES), F32)
    t = t.at[0, MLA_NOPE:MLA_NOPE + half].set(inv_freq).at[0, MLA_NOPE + half:MLA_NOPE + MLA_ROPE].set(inv_freq)
    t = t.at[1, MLA_NOPE:MLA_NOPE + MLA_ROPE].set(1.0)
    t = t.at[2, MLA_NOPE:MLA_NOPE + half].set(-1.0).at[2, MLA_NOPE + half:MLA_NOPE + MLA_ROPE].set(1.0)
    t = t.at[3, :MLA_NOPE].set(1.0)
    return t


def kernel(x, positions, e_g_in, e_w_in, e_g_q_a, e_w_q_up, e_g_kv_a, e_w_kv_up, e_sinks,
           e_w_out, o_g_in, o_w_in, o_b_f, o_w_out, g_final):
    B, S, D = x.shape
    assert D == D_MODEL and all(S % rows == 0 for rows in (PROJ_ROWS, FINAL_ROWS, ATTN_TILE, SWA_ROWS))
    row = lambda g: g.reshape(1, -1).astype(F32)

    w0 = _layer0_weights(e_w_in[0], e_w_q_up[0], e_w_kv_up[0])
    w_a, w_qs, w_kvs, w_gate, w_q, w_qsw, w_k, w_v = w0
    qm, km, vm, qs, kvs, gate0 = _proj0(
        x, positions.reshape(B, S // PROJ_ROWS, 1, PROJ_ROWS), row(e_g_in[0]), w_a, w_qs, w_kvs, w_gate,
        row(e_g_q_a[0]), w_q, w_qsw, row(e_g_kv_a[0]), w_k, w_v, _rope_table())
    o_mla = _flash(qm, km, vm, None, q_col0=0, k_col0=0, v_col0=0,
                   n_pairs=MLA_HEADS // 2, head_lanes=LANES)
    o_swa = _swa(e_sinks[0].astype(F32), qs, kvs)

    sizes = (FOX_WIDTH, FOX_WIDTH, FOX_WIDTH, FOX_HEADS, FOX_WIDTH)
    w1_q, w1_k, w1_v, w1_f, w1_gate = jnp.split(o_w_in[0].T, np.cumsum(sizes)[:-1].tolist(), axis=0)
    fox_scale = FOX_HEAD_DIM ** -0.5
    w1 = jnp.concatenate([w1_q * fox_scale, w1_k, w1_v, w1_gate], axis=0).astype(BF16).T
    w1_f_pad = _pad_rows(w1_f, 0, LANES).astype(BF16).T
    x1, qk1, v1, gate1, f_logit = _mid(x, o_mla, o_swa, gate0, e_w_out[0].astype(BF16), row(o_g_in[0]), w1, w1_f_pad)

    lc = _lcum(f_logit, _pad_cols(row(o_b_f[0]), 0, LANES))
    o_fox = _flash(qk1, qk1, v1, lc, q_col0=0, k_col0=FOX_WIDTH // LANES, v_col0=0,
                   n_pairs=FOX_HEADS // 2, head_lanes=HALF)
    return _final(x1, o_fox, gate1, o_w_out[0].astype(BF16), row(g_final))
```

```python
import functools

import jax
import jax.numpy as jnp
import numpy as np
from jax import lax
from jax.experimental import pallas as pl
from jax.experimental.pallas import tpu as pltpu

F32 = jnp.float32
BF16 = jnp.bfloat16

D_MODEL = 1024
EPS = 1e-6
ROPE_THETA = 10000.0
WINDOW = 128

MLA_HEADS = 8
MLA_Q_RANK = 256
MLA_KV_RANK = 128
MLA_NOPE = 64
MLA_ROPE = 32
MLA_V = 64
SWA_HEADS = 8
SWA_KV_HEADS = 2
SWA_HEAD_DIM = 64
FOX_HEADS = 16
FOX_HEAD_DIM = 64
FOX_WIDTH = FOX_HEADS * FOX_HEAD_DIM

LANES = 128
HALF = 64
V7X_VMEM_LIMIT_BYTES = 56 * 1024 * 1024

PROJ_ROWS = 512
FINAL_ROWS = 1024
ATTN_TILE = 512
SWA_ROWS = 512
CUM_CHUNK = 256
LOG2E = 1.4426950408889634
EXP2_UNDERFLOW = 152.0
NORM_SLACK = 1.01
BOUND_SLACK = 2.0
MIN_FULL_TILES = 4


def _dot(a, b):
    return jnp.dot(a, b, preferred_element_type=F32)


def _dot_nt(a, b):
    return lax.dot_general(a, b, (((1,), (1,)), ((), ())), preferred_element_type=F32)


def _rms(xf, g):
    ms = jnp.mean(xf * xf, axis=-1, keepdims=True)
    return xf * lax.rsqrt(ms + EPS) * g


def _silu(x):
    return x * (1.0 / (1.0 + jnp.exp(-x)))


def _params(n_parallel):
    return pltpu.CompilerParams(
        dimension_semantics=("parallel",) * n_parallel,
        vmem_limit_bytes=V7X_VMEM_LIMIT_BYTES,
    )


def _resident(shape):
    nd = len(shape)
    return pl.BlockSpec(shape, lambda *_: (0,) * nd, pipeline_mode=pl.Buffered(1))


def _proj0_kernel(x_ref, pos_ref, g_in_ref, w_a_ref, w_qs_ref, w_kvs_ref, w_gate_ref,
                  g_q_ref, w_q_ref, w_qsw_ref, g_kv_ref, w_k_ref, w_v_ref, rope_ref,
                  qm_ref, km_ref, vm_ref, qs_ref, kvs_ref, gate_ref):
    h = _rms(x_ref[0], g_in_ref[...]).astype(BF16)

    za = _dot(h, w_a_ref[...])
    cqn = _rms(za[:, :MLA_Q_RANK], g_q_ref[...]).astype(BF16)
    ckvn = _rms(za[:, MLA_Q_RANK:MLA_Q_RANK + MLA_KV_RANK], g_kv_ref[...]).astype(BF16)
    kpe = za[:, 384:512]
    kpe_sw = za[:, 512:640]

    pos_row = pos_ref[0, 0].astype(F32)
    pos_col = jnp.broadcast_to(pos_row, (LANES, pos_row.shape[1])).T
    ang = pos_col * rope_ref[0:1, :]
    cos_t = jnp.cos(ang) * rope_ref[1:2, :]
    sin_t = jnp.sin(ang) * rope_ref[2:3, :]
    scale = (MLA_NOPE + MLA_ROPE) ** -0.5 * LOG2E
    q_cos = (cos_t + rope_ref[3:4, :]) * scale
    q_sin = sin_t * scale
    kpe_rot = kpe * cos_t + kpe_sw * sin_t

    for c in range(MLA_HEADS // 2):
        cols = slice(c * 2 * LANES, (c + 1) * 2 * LANES)
        q = _dot(cqn, w_q_ref[:, cols])
        q_sw = _dot(cqn, w_qsw_ref[:, cols])
        kn = _dot(ckvn, w_k_ref[:, cols])
        for hh in range(2):
            lo = hh * LANES
            qm_ref[0, :, c * 2 * LANES + lo:c * 2 * LANES + lo + LANES] = (
                q[:, lo:lo + LANES] * q_cos + q_sw[:, lo:lo + LANES] * q_sin).astype(BF16)
            km_ref[0, :, c * 2 * LANES + lo:c * 2 * LANES + lo + LANES] = (
                kn[:, lo:lo + LANES] + kpe_rot).astype(BF16)
    v = _dot(ckvn, w_v_ref[...])
    lane_v = lax.broadcasted_iota(jnp.int32, v.shape, 1)
    ones_half = (((lane_v >> 6) ^ (lane_v >> 7)) & 1) == 1
    vm_ref[0] = jnp.where(ones_half, 1.0, v).astype(BF16)
    qs_ref[0] = (_dot(h, w_qs_ref[...]) * LOG2E).astype(BF16)
    kvs_ref[0] = _dot(h, w_kvs_ref[...]).astype(BF16)
    gate_ref[0] = _dot(h, w_gate_ref[...]).astype(BF16)


def _pipelined_tiles(n_batch, tiles_per_batch):
    n_tiles = n_batch * tiles_per_batch

    def this_tile(g):
        t = jnp.minimum(g, n_tiles - 1)
        return t // tiles_per_batch, t % tiles_per_batch

    def prev_tile(g):
        t = jnp.maximum(g - 1, 0)
        return t // tiles_per_batch, t % tiles_per_batch

    params = pltpu.CompilerParams(dimension_semantics=("arbitrary",), vmem_limit_bytes=V7X_VMEM_LIMIT_BYTES)
    return n_tiles + 1, this_tile, prev_tile, params


def _proj0(x, pos4, g_in, w_a, w_qs, w_kvs, w_gate, g_q, w_q, w_qsw, g_kv, w_k, w_v, rope):
    B, S, D = x.shape
    tm = PROJ_ROWS
    row = lambda width: pl.BlockSpec((1, tm, width), lambda b, i: (b, i, 0))
    out_widths = (MLA_HEADS * LANES, MLA_HEADS * LANES, MLA_HEADS * LANES,
                  SWA_HEADS * SWA_HEAD_DIM, 4 * LANES, D)
    return pl.pallas_call(
        _proj0_kernel,
        grid=(B, S // tm),
        in_specs=[row(D), pl.BlockSpec((1, 1, 1, tm), lambda b, i: (b, i, 0, 0))]
        + [_resident(a.shape) for a in (g_in, w_a, w_qs, w_kvs, w_gate, g_q, w_q, w_qsw, g_kv, w_k, w_v, rope)],
        out_specs=[row(w) for w in out_widths],
        out_shape=[jax.ShapeDtypeStruct((B, S, w), BF16) for w in out_widths],
        compiler_params=_params(2),
        name="proj0",
    )(x, pos4, g_in, w_a, w_qs, w_kvs, w_gate, g_q, w_q, w_qsw, g_kv, w_k, w_v, rope)


def _fox_skip_counts(q_ref, k_ref, lc_ref, nq):
    T = ATTN_TILE
    q_all, k_all = q_ref[0], k_ref[0]
    prods = jnp.concatenate([q_all * q_all, k_all * k_all, q_all * k_all], axis=1)
    d_id = lax.broadcasted_iota(jnp.int32, (3 * LANES, LANES), 0)
    c_id = lax.broadcasted_iota(jnp.int32, (3 * LANES, LANES), 1)
    head_cols = (c_id == 2 * (d_id >> 7) + ((d_id >> 6) & 1)).astype(BF16)
    sums = _dot(prods, head_cols)
    k_max = jnp.max(sums, axis=0, keepdims=True)[:, 2:4]
    j_lo = [0]
    for i in range(1, nq):
        r = slice(i * T, (i + 1) * T)
        u = (NORM_SLACK * jnp.sqrt(jnp.max(sums[r], axis=0, keepdims=True)[:, 0:2] * k_max)
             - jnp.min(sums[r], axis=0, keepdims=True)[:, 4:6] + BOUND_SLACK)
        count = jnp.zeros((1, 1), jnp.int32)
        for j in range(i):
            skip = None
            for hh in range(2):
                bound = u[:, hh:hh + 1] + lc_ref[0, hh, i][:, 0:1] - lc_ref[0, hh, j][:, T - 1:T]
                skip_h = bound < -EXP2_UNDERFLOW
                skip = skip_h if skip is None else skip & skip_h
            count = count + skip.astype(jnp.int32)
        j_lo.append(count[0, 0])
    return j_lo


def _fox_write_schedule(j_lo, ti_ref, tj_ref, nq):
    pos = jnp.int32(0)
    pad_i = pad_j = jnp.int32(0)
    for i in range(1, nq):
        for j in range(i):
            keep = j >= j_lo[i]

            @pl.when(keep)
            def _(i=i, j=j, pos=pos):
                ti_ref[nq + pos] = jnp.int32(i)
                tj_ref[nq + pos] = jnp.int32(j)

            pos = pos + keep.astype(jnp.int32)
            pad_i = jnp.where(keep, pad_i, i)
            pad_j = jnp.where(keep, pad_j, j)
    n_full = jnp.maximum(((pos + 1) >> 1) << 1, MIN_FULL_TILES)
    for extra in range(MIN_FULL_TILES):
        @pl.when(pos + extra < n_full)
        def _(extra=extra):
            ti_ref[nq + pos + extra] = pad_i
            tj_ref[nq + pos + extra] = pad_j
    return n_full


def _flash_kernel(*refs, head_lanes, has_bias):
    T = ATTN_TILE
    if has_bias:
        q_ref, k_ref, v_ref, lc_ref, o_ref = refs[:5]
        scr = refs[5:-2]
        ti_ref, tj_ref = refs[-2:]
        nq = q_ref.shape[1] // T
        j_lo = _fox_skip_counts(q_ref, k_ref, lc_ref, nq)
    else:
        ti_ref, tj_ref, q_ref, k_ref, v_ref, o_ref = refs[:6]
        lc_ref = None
        scr = refs[6:]
        nq = q_ref.shape[1] // T
    p_scr, alpha_scr = scr[0:2], scr[2:4]
    m_scr, acc_scr, mask_scr = scr[4:]
    lane = lax.broadcasted_iota(jnp.int32, (T, LANES), 1)
    lo_half = lane < HALF
    zero = jnp.zeros((), BF16)

    def start(x, align):
        return x if isinstance(x, int) else pl.multiple_of(x, align)

    def rows(i):
        return pl.ds(start(i * T, T), T)

    def tile_of(t, diag):
        if diag:
            assert isinstance(t, int)
            return t, t
        return ti_ref[t], tj_ref[t]

    H = T // 2
    full_blocks = ((0, T, T),)
    diag_blocks = ((0, H, H), (H, H, T))

    def softmax(t, slot, diag):
        i, j = tile_of(t, diag)
        if head_lanes == HALF:
            q_pair = q_ref[0, rows(i), :]
            q_heads = [jnp.where(lo_half, q_pair, zero), jnp.where(lo_half, zero, q_pair)]
        for hh in range(2):
            for r0, nr, nc in (diag_blocks if diag else full_blocks):
                kr = pl.ds(start(j * T, T), nc)
                if head_lanes == LANES:
                    q_h = q_ref[0, pl.ds(start(i * T + r0, H), nr), hh * LANES:(hh + 1) * LANES]
                    k_t = k_ref[0, kr, hh * LANES:(hh + 1) * LANES]
                else:
                    q_h = q_heads[hh][r0:r0 + nr]
                    k_t = k_ref[0, kr, :]
                s = _dot_nt(q_h, k_t)
                if has_bias:
                    s = s + (lc_ref[0, hh, i][:, 0:1] - lc_ref[0, hh, j][:, :nc])
                if diag:
                    tri = s[:, nc - H:] + mask_scr[...]
                    s = tri if nc == H else jnp.concatenate([s[:, :nc - H], tri], axis=1)
                state_rows = pl.ds(start(i * T + r0, H), nr)
                m_cur = jnp.max(s, axis=1, keepdims=True)
                if diag:
                    m_next = jnp.broadcast_to(m_cur, (nr, LANES))
                else:
                    m_prev = m_scr[hh, state_rows, :]
                    m_next = jnp.maximum(m_prev, m_cur)
                    alpha_scr[slot][hh, r0:r0 + nr, :] = jnp.exp2(m_prev - m_next)
                p = jnp.exp2(s - jnp.concatenate([m_next] * (nc // LANES), axis=1))
                p_scr[slot][hh, r0:r0 + nr, :nc] = p.astype(BF16)
                m_scr[hh, state_rows, :] = m_next

    def accumulate(t, slot, diag):
        i, j = tile_of(t, diag)
        for hh in range(2):
            for r0, nr, nc in (diag_blocks if diag else full_blocks):
                state_rows = pl.ds(start(i * T + r0, H), nr)
                v_t = v_ref[0, pl.ds(start(j * T, T), nc), hh * LANES:(hh + 1) * LANES]
                pv = _dot(p_scr[slot][hh, r0:r0 + nr, :nc], v_t)
                if diag:
                    acc_scr[hh, state_rows, :] = pv
                else:
                    acc_scr[hh, state_rows, :] = (
                        alpha_scr[slot][hh, r0:r0 + nr, :] * acc_scr[hh, state_rows, :] + pv)

    row_id = lax.broadcasted_iota(jnp.int32, (H, H), 0)
    col_id = lax.broadcasted_iota(jnp.int32, (H, H), 1)
    mask_scr[...] = jnp.where(col_id <= row_id, 0.0, -jnp.inf).astype(F32)

    def is_diag(step):
        return step < nq

    def step_pair(t, t_static):
        softmax(t, 1, is_diag(t_static))
        accumulate(t - 1, 0, is_diag(t_static - 1))
        softmax(t + 1, 0, is_diag(t_static + 1))
        accumulate(t, 1, is_diag(t_static))

    def make_body(u_static):
        def body(u, carry):
            step_pair(4 * u + 3, 4 * u_static + 3)
            step_pair(4 * u + 5, 4 * u_static + 5)
            return carry
        return body

    assert nq % 4 == 0 and MIN_FULL_TILES == 4
    n_diag_bodies = (nq - 4) // 4
    softmax(0, 0, True)
    step_pair(1, 1)
    for u in range(n_diag_bodies):
        make_body(u)(u, 0)
    if has_bias:
        n_full = _fox_write_schedule(j_lo, ti_ref, tj_ref, nq)
        n_bodies = (n_full >> 2) + 1
        odd_pair = (n_full & 2) != 0
        last_step = nq + n_full - 1
    else:
        assert (ti_ref.shape[0] - 4) % 4 == 0
        n_bodies = (ti_ref.shape[0] - 4) // 4
        odd_pair = None
        last_step = ti_ref.shape[0] - 1
    make_body(n_diag_bodies)(n_diag_bodies, 0)
    lax.fori_loop(n_diag_bodies + 1, n_bodies, make_body(n_diag_bodies + 1), 0)
    if odd_pair is not None:
        @pl.when(odd_pair)
        def _():
            step_pair(4 * n_bodies + 3, 4 * (n_diag_bodies + 1) + 3)
    softmax(last_step, 1, False)
    accumulate(last_step - 1, 0, False)
    accumulate(last_step, 1, False)

    for i in range(nq):
        r = pl.ds(i * T, T)
        acc_even, acc_odd = acc_scr[0, r, :], acc_scr[1, r, :]
        num = jnp.where(lo_half, acc_even, acc_odd)
        den = pltpu.roll(jnp.where(lo_half, acc_odd, acc_even), HALF, 1)
        o_ref[0, r, :] = (num / den).astype(o_ref.dtype)


def _tile_schedule(nq):
    steps = [(i, i) for i in range(nq)] + [(i, j) for i in range(nq) for j in range(i)]
    assert (len(steps) - 4) % 4 == 0 and len(steps) - nq >= MIN_FULL_TILES
    return np.array(steps, np.int32)


def _flash(q, k, v, lc, *, q_col0, k_col0, v_col0, n_pairs, head_lanes):
    B, S, _ = q.shape
    T = ATTN_TILE
    qk_w = 2 * head_lanes
    qk_step = qk_w // LANES
    steps = _tile_schedule(S // T)
    in_specs = [
        pl.BlockSpec((1, S, qk_w), lambda b, p: (b, 0, q_col0 // qk_step + p)),
        pl.BlockSpec((1, S, qk_w), lambda b, p: (b, 0, k_col0 // qk_step + p)),
        pl.BlockSpec((1, S, 2 * LANES), lambda b, p: (b, 0, v_col0 // 2 + p)),
    ]
    args = [q, k, v]
    per_slot = lambda shape, dtype: [pltpu.VMEM(shape, dtype)] * 2
    scratch = (per_slot((2, T, T), BF16) + per_slot((2, T, LANES), F32)
               + [pltpu.VMEM((2, S, LANES), F32)] * 2
               + [pltpu.VMEM((T // 2, T // 2), F32)])
    if lc is not None:
        in_specs.append(pl.BlockSpec((1, 2) + lc.shape[2:], lambda b, p: (b, p, 0, 0, 0)))
        args.append(lc)
        scratch += [pltpu.SMEM((len(steps),), jnp.int32)] * 2
    else:
        smem = pl.BlockSpec(memory_space=pltpu.SMEM)
        in_specs = [smem, smem] + in_specs
        args = [jnp.asarray(steps[:, 0]), jnp.asarray(steps[:, 1])] + args
    return pl.pallas_call(
        functools.partial(_flash_kernel, head_lanes=head_lanes, has_bias=lc is not None),
        grid=(B, n_pairs),
        in_specs=in_specs,
        out_specs=pl.BlockSpec((1, S, LANES), lambda b, p: (b, 0, p)),
        out_shape=jax.ShapeDtypeStruct((B, S, n_pairs * LANES), BF16),
        scratch_shapes=scratch,
        compiler_params=_params(2),
        name="fox" if lc is not None else "mla",
    )(*args)


def _swa_kernel(sinks_ref, bias_ref, qs_ref, kvs_ref, o_ref):
    W = WINDOW
    group = SWA_HEADS // SWA_KV_HEADS
    step = pl.program_id(1)
    lane = lax.broadcasted_iota(jnp.int32, (W, LANES), 1)
    lo_half = lane < HALF
    zero = jnp.zeros((), BF16)

    for blk in range(SWA_ROWS // W):
        t0 = step * SWA_ROWS + blk * W
        k_start = pl.multiple_of(jnp.maximum(t0 - W, 0), W)
        bias_sel = jnp.where(step == 0, 0, 1) if blk == 0 else 1
        rows = slice(blk * W, (blk + 1) * W)
        for g in range(SWA_KV_HEADS):
            k_t = kvs_ref[0, pl.ds(k_start, 2 * W), g * LANES:(g + 1) * LANES]
            v_t = kvs_ref[0, pl.ds(k_start, 2 * W), (2 + g) * LANES:(3 + g) * LANES]
            q_parts = []
            for pair in range(group // 2):
                q_pair = qs_ref[0, rows, (g * 2 + pair) * LANES:(g * 2 + pair + 1) * LANES]
                q_parts.append(jnp.where(lo_half, q_pair, zero))
                q_parts.append(jnp.where(lo_half, zero, q_pair))
            s_all = _dot_nt(jnp.concatenate(q_parts, axis=0), k_t)
            p_parts, denoms = [], []
            for hq in range(group):
                head = g * group + hq
                sink = sinks_ref[head] * LOG2E
                s = s_all[hq * W:(hq + 1) * W] + bias_ref[bias_sel, head]
                m = jnp.maximum(jnp.max(s, axis=1, keepdims=True), sink)
                p = jnp.exp2(s - m)
                denoms.append(jnp.sum(p, axis=1, keepdims=True) + jnp.exp2(sink - m))
                p_parts.append(p.astype(BF16))
            o_all = _dot(jnp.concatenate(p_parts, axis=0), v_t)
            for pair in range(group // 2):
                o_even = o_all[(2 * pair) * W:(2 * pair + 1) * W] / denoms[2 * pair]
                o_odd = o_all[(2 * pair + 1) * W:(2 * pair + 2) * W] / denoms[2 * pair + 1]
                o_ref[0, rows, (g * 2 + pair) * LANES:(g * 2 + pair + 1) * LANES] = jnp.where(
                    lo_half, o_even, o_odd).astype(o_ref.dtype)


def _swa_bias():
    W = WINDOW
    slopes = 2.0 ** (-8.0 * (np.arange(SWA_HEADS, dtype=np.float32) + 1.0) / SWA_HEADS)
    a = np.arange(W, dtype=np.float32)[:, None]
    c = np.arange(2 * W, dtype=np.float32)[None, :]
    tables = []
    for k_offset in (0.0, float(W)):
        dist = k_offset + a - c
        bias = -slopes.astype(np.float32)[:, None, None] * dist[None]
        tables.append(np.where((dist >= 0) & (dist < W), bias * np.float32(LOG2E), -np.inf).astype(np.float32))
    return jnp.asarray(np.stack(tables))


def _swa(sinks, qs, kvs):
    B, S, C = qs.shape
    bias = _swa_bias()
    return pl.pallas_call(
        _swa_kernel,
        grid=(B, S // SWA_ROWS),
        in_specs=[pl.BlockSpec(memory_space=pltpu.SMEM), _resident(bias.shape),
                  pl.BlockSpec((1, SWA_ROWS, C), lambda b, i: (b, i, 0)),
                  pl.BlockSpec((1, S, kvs.shape[2]), lambda b, i: (b, 0, 0))],
        out_specs=pl.BlockSpec((1, SWA_ROWS, C), lambda b, i: (b, i, 0)),
        out_shape=jax.ShapeDtypeStruct((B, S, C), BF16),
        compiler_params=_params(2),
        name="swa",
    )(sinks, bias, qs, kvs)


def _mid_kernel(x_ref, om_ref, os_ref, gate_ref, w_out_ref, g_in_ref, w_in_ref, w_f_ref,
                x1_ref, qk_ref, v_ref, g1_ref, f_ref, h_scr):
    @pl.when(pl.program_id(0) == 0)
    def _():
        h_scr[...] = jnp.zeros(h_scr.shape, BF16)

    h = h_scr[...]
    w = FOX_WIDTH
    qk_ref[0, :, :w] = (_dot(h, w_in_ref[:, :w]) * LOG2E).astype(BF16)
    qk_ref[0, :, w:] = _dot(h, w_in_ref[:, w:2 * w]).astype(BF16)
    g1_ref[0] = _dot(h, w_in_ref[:, 3 * w:]).astype(BF16)
    f_ref[0] = _dot(h, w_f_ref[...])
    v = _dot(h, w_in_ref[:, 2 * w:3 * w])
    lo_half = lax.broadcasted_iota(jnp.int32, (v.shape[0], LANES), 1) < HALF
    for pair in range(FOX_HEADS // 2):
        v_pair = v[:, pair * LANES:(pair + 1) * LANES]
        v_ref[0, :, (2 * pair) * LANES:(2 * pair + 1) * LANES] = jnp.where(lo_half, v_pair, 1.0).astype(BF16)
        v_ref[0, :, (2 * pair + 1) * LANES:(2 * pair + 2) * LANES] = jnp.where(lo_half, 1.0, v_pair).astype(BF16)

    half = D_MODEL // 2
    gate = _silu(gate_ref[0].astype(F32))
    a_m = (om_ref[0].astype(F32) * gate[:, :half]).astype(BF16)
    a_s = (os_ref[0].astype(F32) * gate[:, half:]).astype(BF16)
    x1 = x_ref[0] + (_dot(a_m, w_out_ref[:half, :]) + _dot(a_s, w_out_ref[half:, :]))
    x1_ref[0] = x1
    h_scr[...] = _rms(x1, g_in_ref[...]).astype(BF16)


def _mid(x, o_mla, o_swa, gate, w_out, g_in, w_in, w_f):
    B, S, D = x.shape
    tm = PROJ_ROWS
    n_steps, this_tile, prev_tile, params = _pipelined_tiles(B, S // tm)
    this_spec = lambda width: pl.BlockSpec((1, tm, width), lambda g: this_tile(g) + (0,))
    prev_spec = lambda width: pl.BlockSpec((1, tm, width), lambda g: prev_tile(g) + (0,))
    outs = [(D, F32), (2 * FOX_WIDTH, BF16), (FOX_HEADS * LANES, BF16), (FOX_WIDTH, BF16), (LANES, F32)]
    return pl.pallas_call(
        _mid_kernel,
        grid=(n_steps,),
        in_specs=[this_spec(width) for width in (D, D // 2, D // 2, D)]
        + [_resident(a.shape) for a in (w_out, g_in, w_in, w_f)],
        out_specs=[this_spec(outs[0][0])] + [prev_spec(width) for width, _ in outs[1:]],
        out_shape=[jax.ShapeDtypeStruct((B, S, width), dtype) for width, dtype in outs],
        scratch_shapes=[pltpu.VMEM((tm, D), BF16)],
        compiler_params=params,
        name="mid",
    )(x, o_mla, o_swa, gate, w_out, g_in, w_in, w_f)


def _split3(x):
    hi = x.astype(BF16)
    r1 = x - hi.astype(F32)
    mid = r1.astype(BF16)
    lo = (r1 - mid.astype(F32)).astype(BF16)
    return hi, mid, lo


def _lcum_kernel(f_ref, b_ref, lc_ref):
    S = f_ref.shape[1]
    log_f = jax.nn.log_sigmoid(f_ref[0] + b_ref[...])
    log_f_t = log_f.T[:FOX_HEADS]
    n = CUM_CHUNK
    tri = (lax.broadcasted_iota(jnp.int32, (n, n), 0) <= lax.broadcasted_iota(jnp.int32, (n, n), 1)).astype(BF16)
    carry = jnp.zeros((FOX_HEADS, 1), F32)
    for c in range(S // n):
        hi, mid, lo = _split3(log_f_t[:, c * n:(c + 1) * n])
        cs = (_dot(hi, tri) + _dot(mid, tri)) + _dot(lo, tri) + carry
        cs2 = cs * LOG2E
        tile, lane0 = divmod(c * n, ATTN_TILE)
        for head in range(FOX_HEADS):
            lc_ref[0, head, tile, :, lane0:lane0 + n] = cs2[head:head + 1]
        carry = cs[:, n - 1:n]


def _lcum(f, b_pad):
    B, S, _ = f.shape
    out_block = (1, FOX_HEADS, S // ATTN_TILE, 1, ATTN_TILE)
    return pl.pallas_call(
        _lcum_kernel,
        grid=(B,),
        in_specs=[pl.BlockSpec((1, S, LANES), lambda b: (b, 0, 0)), _resident(b_pad.shape)],
        out_specs=pl.BlockSpec(out_block, lambda b: (b, 0, 0, 0, 0)),
        out_shape=jax.ShapeDtypeStruct((B,) + out_block[1:], F32),
        compiler_params=_params(1),
        name="lcum",
    )(f, b_pad)


def _final_kernel(x1_ref, o_ref, gate_ref, w_out_ref, g_ref, y_ref):
    a = (o_ref[0].astype(F32) * _silu(gate_ref[0].astype(F32))).astype(BF16)
    x2 = x1_ref[0] + _dot(a, w_out_ref[...])
    y_ref[0] = _rms(x2, g_ref[...])


def _final(x1, o_fox, gate, w_out, g_final):
    B, S, D = x1.shape
    tm = FINAL_ROWS
    row = pl.BlockSpec((1, tm, D), lambda b, i: (b, i, 0))
    return pl.pallas_call(
        _final_kernel,
        grid=(B, S // tm),
        in_specs=[row, row, row, _resident(w_out.shape), _resident(g_final.shape)],
        out_specs=row,
        out_shape=jax.ShapeDtypeStruct((B, S, D), F32),
        compiler_params=_params(2),
        name="final",
    )(x1, o_fox, gate, w_out, g_final)


def _pad_cols(w, left, total):
    return jnp.pad(w, ((0, 0), (left, total - left - w.shape[1])))


def _pad_rows(w, top, total):
    return jnp.pad(w, ((top, total - top - w.shape[0]), (0, 0)))


def _layer0_weights(w_in, w_q_up, w_kv_up):
    sizes = (MLA_Q_RANK, MLA_KV_RANK, MLA_ROPE, SWA_HEADS * SWA_HEAD_DIM,
             SWA_KV_HEADS * SWA_HEAD_DIM, SWA_KV_HEADS * SWA_HEAD_DIM, D_MODEL)
    cq, ckv, kpe, qs, ks, vs, gate = jnp.split(w_in.T, np.cumsum(sizes)[:-1].tolist(), axis=0)
    half = MLA_ROPE // 2
    kpe_sw = jnp.concatenate([kpe[half:], kpe[:half]], axis=0)
    a_t = jnp.concatenate([cq, ckv, _pad_rows(kpe, MLA_NOPE, LANES), _pad_rows(kpe_sw, MLA_NOPE, LANES)], axis=0)
    d = SWA_HEAD_DIM
    kvs_t = jnp.concatenate([ks[:d], ks[:d], ks[d:], ks[d:], vs[:d], vs[:d], vs[d:], vs[d:]], axis=0)
    swa_scale = SWA_HEAD_DIM ** -0.5
    w_a, w_qs, w_kvs, w_gate = [w_t.astype(BF16).T for w_t in (a_t, qs * swa_scale, kvs_t, gate)]
    qk = MLA_NOPE + MLA_ROPE
    wq3 = w_q_up.reshape(MLA_Q_RANK, MLA_HEADS, qk)
    w_q = jnp.pad(wq3, ((0, 0), (0, 0), (0, LANES - qk))).reshape(MLA_Q_RANK, MLA_HEADS * LANES)
    wq_sw = jnp.concatenate([wq3[:, :, MLA_NOPE + half:], wq3[:, :, MLA_NOPE:MLA_NOPE + half]], axis=2)
    w_qsw = jnp.pad(wq_sw, ((0, 0), (0, 0), (MLA_NOPE, LANES - qk))).reshape(MLA_Q_RANK, MLA_HEADS * LANES)
    wkv3 = w_kv_up.reshape(MLA_KV_RANK, MLA_HEADS, MLA_NOPE + MLA_V)
    w_k = jnp.pad(wkv3[:, :, :MLA_NOPE], ((0, 0), (0, 0), (0, LANES - MLA_NOPE))).reshape(MLA_KV_RANK, MLA_HEADS * LANES)
    wv4 = wkv3[:, :, MLA_NOPE:].reshape(MLA_KV_RANK, MLA_HEADS // 2, 2, MLA_V)
    zeros_v = jnp.zeros_like(wv4[:, :, 0])
    w_v = jnp.stack([wv4[:, :, 0], zeros_v, zeros_v, wv4[:, :, 1]], axis=2).reshape(MLA_KV_RANK, MLA_HEADS * LANES)
    return [w_a, w_qs, w_kvs, w_gate] + [w.astype(BF16) for w in (w_q, w_qsw, w_k, w_v)]


def _rope_table():
    half = MLA_ROPE // 2
    inv_freq = 1.0 / (ROPE_THETA ** (jnp.arange(0, MLA_ROPE, 2, dtype=F32) / MLA_ROPE))
    t = jnp.zeros((8, LANES), F32)
    t = t.at[0, MLA_NOPE:MLA_NOPE + half].set(inv_freq).at[0, MLA_NOPE + half:MLA_NOPE + MLA_ROPE].set(inv_freq)
    t = t.at[1, MLA_NOPE:MLA_NOPE + MLA_ROPE].set(1.0)
    t = t.at[2, MLA_NOPE:MLA_NOPE + half].set(-1.0).at[2, MLA_NOPE + half:MLA_NOPE + MLA_ROPE].set(1.0)
    t = t.at[3, :MLA_NOPE].set(1.0)
    return t


def kernel(x, positions, e_g_in, e_w_in, e_g_q_a, e_w_q_up, e_g_kv_a, e_w_kv_up, e_sinks,
           e_w_out, o_g_in, o_w_in, o_b_f, o_w_out, g_final):
    B, S, D = x.shape
    assert D == D_MODEL and all(S % rows == 0 for rows in (PROJ_ROWS, FINAL_ROWS, ATTN_TILE, SWA_ROWS))
    row = lambda g: g.reshape(1, -1).astype(F32)

    w0 = _layer0_weights(e_w_in[0], e_w_q_up[0], e_w_kv_up[0])
    w_a, w_qs, w_kvs, w_gate, w_q, w_qsw, w_k, w_v = w0
    qm, km, vm, qs, kvs, gate0 = _proj0(
        x, positions.reshape(B, S // PROJ_ROWS, 1, PROJ_ROWS), row(e_g_in[0]), w_a, w_qs, w_kvs, w_gate,
        row(e_g_q_a[0]), w_q, w_qsw, row(e_g_kv_a[0]), w_k, w_v, _rope_table())
    o_mla = _flash(qm, km, vm, None, q_col0=0, k_col0=0, v_col0=0,
                   n_pairs=MLA_HEADS // 2, head_lanes=LANES)
    o_swa = _swa(e_sinks[0].astype(F32), qs, kvs)

    sizes = (FOX_WIDTH, FOX_WIDTH, FOX_WIDTH, FOX_HEADS, FOX_WIDTH)
    w1_q, w1_k, w1_v, w1_f, w1_gate = jnp.split(o_w_in[0].T, np.cumsum(sizes)[:-1].tolist(), axis=0)
    fox_scale = FOX_HEAD_DIM ** -0.5
    w1 = jnp.concatenate([w1_q * fox_scale, w1_k, w1_v, w1_gate], axis=0).astype(BF16).T
    w1_f_pad = _pad_rows(w1_f, 0, LANES).astype(BF16).T
    x1, qk1, v1, gate1, f_logit = _mid(x, o_mla, o_swa, gate0, e_w_out[0].astype(BF16), row(o_g_in[0]), w1, w1_f_pad)

    lc = _lcum(f_logit, _pad_cols(row(o_b_f[0]), 0, LANES))
    o_fox = _flash(qk1, qk1, v1, lc, q_col0=0, k_col0=FOX_WIDTH // LANES, v_col0=0,
                   n_pairs=FOX_HEADS // 2, head_lanes=HALF)
    return _final(x1, o_fox, gate1, o_w_out[0].astype(BF16), row(g_final))
```

```python
import functools

import jax
import jax.numpy as jnp
import numpy as np
from jax import lax
from jax.experimental import pallas as pl
from jax.experimental.pallas import tpu as pltpu

F32 = jnp.float32
BF16 = jnp.bfloat16

D_MODEL = 1024
EPS = 1e-6
ROPE_THETA = 10000.0
WINDOW = 128

MLA_HEADS = 8
MLA_Q_RANK = 256
MLA_KV_RANK = 128
MLA_NOPE = 64
MLA_ROPE = 32
MLA_V = 64
SWA_HEADS = 8
SWA_KV_HEADS = 2
SWA_HEAD_DIM = 64
FOX_HEADS = 16
FOX_HEAD_DIM = 64
FOX_WIDTH = FOX_HEADS * FOX_HEAD_DIM

LANES = 128
HALF = 64
V7X_VMEM_LIMIT_BYTES = 56 * 1024 * 1024

PROJ_ROWS = 512
FINAL_ROWS = 1024
ATTN_TILE = 512
SWA_ROWS = 512
CUM_CHUNK = 256
LOG2E = 1.4426950408889634
EXP2_UNDERFLOW = 152.0
NORM_SLACK = 1.01
BOUND_SLACK = 2.0
MIN_FULL_TILES = 4


def _dot(a, b):
    return jnp.dot(a, b, preferred_element_type=F32)


def _dot_nt(a, b):
    return lax.dot_general(a, b, (((1,), (1,)), ((), ())), preferred_element_type=F32)


def _rms(xf, g):
    ms = jnp.mean(xf * xf, axis=-1, keepdims=True)
    return xf * lax.rsqrt(ms + EPS) * g


def _silu(x):
    return x * (1.0 / (1.0 + jnp.exp(-x)))


def _params(n_parallel):
    return pltpu.CompilerParams(
        dimension_semantics=("parallel",) * n_parallel,
        vmem_limit_bytes=V7X_VMEM_LIMIT_BYTES,
    )


def _resident(shape):
    nd = len(shape)
    return pl.BlockSpec(shape, lambda *_: (0,) * nd, pipeline_mode=pl.Buffered(1))


def _proj0_kernel(x_ref, pos_ref, g_in_ref, w_a_ref, w_qs_ref, w_kvs_ref, w_gate_ref,
                  g_q_ref, w_q_ref, g_kv_ref, w_k_ref, w_v_ref, rope_ref,
                  qm_ref, km_ref, vm_ref, qs_ref, kvs_ref, gate_ref):
    h = _rms(x_ref[0], g_in_ref[...]).astype(BF16)

    za = _dot(h, w_a_ref[...])
    cqn = _rms(za[:, :MLA_Q_RANK], g_q_ref[...]).astype(BF16)
    ckvn = _rms(za[:, MLA_Q_RANK:MLA_Q_RANK + MLA_KV_RANK], g_kv_ref[...]).astype(BF16)
    kpe = za[:, 384:512]
    kpe_sw = pltpu.roll(kpe, LANES - MLA_ROPE, 1)

    pos_row = pos_ref[0, 0].astype(F32)
    pos_col = jnp.broadcast_to(pos_row, (LANES, pos_row.shape[1])).T
    ang = pos_col * rope_ref[0:1, :]
    cos_t = jnp.cos(ang) * rope_ref[1:2, :]
    sin_t = jnp.sin(ang) * rope_ref[2:3, :]
    scale = (MLA_NOPE + MLA_ROPE) ** -0.5 * LOG2E
    q_cos = (cos_t + rope_ref[3:4, :]) * scale
    half = MLA_ROPE // 2
    q_sin_lo = jnp.sin(ang) * (rope_ref[4:5, :] * scale)
    q_sin_hi = jnp.sin(ang) * (rope_ref[5:6, :] * scale)
    kpe_rot = kpe * cos_t + kpe_sw * sin_t

    for c in range(MLA_HEADS // 2):
        cols = slice(c * 2 * LANES, (c + 1) * 2 * LANES)
        q = _dot(cqn, w_q_ref[:, cols])
        kn = _dot(ckvn, w_k_ref[:, cols])
        for hh in range(2):
            lo = hh * LANES
            q_h = q[:, lo:lo + LANES]
            q_rot = (q_h * q_cos + pltpu.roll(q_h, LANES - half, 1) * q_sin_lo
                     + pltpu.roll(q_h, half, 1) * q_sin_hi)
            qm_ref[0, :, c * 2 * LANES + lo:c * 2 * LANES + lo + LANES] = q_rot.astype(BF16)
            km_ref[0, :, c * 2 * LANES + lo:c * 2 * LANES + lo + LANES] = (
                kn[:, lo:lo + LANES] + kpe_rot).astype(BF16)
    v = _dot(ckvn, w_v_ref[...])
    lane_v = lax.broadcasted_iota(jnp.int32, v.shape, 1)
    ones_half = (((lane_v >> 6) ^ (lane_v >> 7)) & 1) == 1
    vm_ref[0] = jnp.where(ones_half, 1.0, v).astype(BF16)
    qs_ref[0] = (_dot(h, w_qs_ref[...]) * LOG2E).astype(BF16)
    kvs_ref[0] = _dot(h, w_kvs_ref[...]).astype(BF16)
    gate_ref[0] = _dot(h, w_gate_ref[...]).astype(BF16)


def _pipelined_tiles(n_batch, tiles_per_batch):
    n_tiles = n_batch * tiles_per_batch

    def this_tile(g):
        t = jnp.minimum(g, n_tiles - 1)
        return t // tiles_per_batch, t % tiles_per_batch

    def prev_tile(g):
        t = jnp.maximum(g - 1, 0)
        return t // tiles_per_batch, t % tiles_per_batch

    params = pltpu.CompilerParams(dimension_semantics=("arbitrary",), vmem_limit_bytes=V7X_VMEM_LIMIT_BYTES)
    return n_tiles + 1, this_tile, prev_tile, params


def _proj0(x, pos4, g_in, w_a, w_qs, w_kvs, w_gate, g_q, w_q, g_kv, w_k, w_v, rope):
    B, S, D = x.shape
    tm = PROJ_ROWS
    row = lambda width: pl.BlockSpec((1, tm, width), lambda b, i: (b, i, 0))
    out_widths = (MLA_HEADS * LANES, MLA_HEADS * LANES, MLA_HEADS * LANES,
                  SWA_HEADS * SWA_HEAD_DIM, 4 * LANES, D)
    return pl.pallas_call(
        _proj0_kernel,
        grid=(B, S // tm),
        in_specs=[row(D), pl.BlockSpec((1, 1, 1, tm), lambda b, i: (b, i, 0, 0))]
        + [_resident(a.shape) for a in (g_in, w_a, w_qs, w_kvs, w_gate, g_q, w_q, g_kv, w_k, w_v, rope)],
        out_specs=[row(w) for w in out_widths],
        out_shape=[jax.ShapeDtypeStruct((B, S, w), BF16) for w in out_widths],
        compiler_params=_params(2),
        name="proj0",
    )(x, pos4, g_in, w_a, w_qs, w_kvs, w_gate, g_q, w_q, g_kv, w_k, w_v, rope)


def _fox_skip_counts(q_ref, k_ref, lc_ref, nq):
    T = ATTN_TILE
    q_all, k_all = q_ref[0], k_ref[0]
    prods = jnp.concatenate([q_all * q_all, k_all * k_all, q_all * k_all], axis=1)
    d_id = lax.broadcasted_iota(jnp.int32, (3 * LANES, LANES), 0)
    c_id = lax.broadcasted_iota(jnp.int32, (3 * LANES, LANES), 1)
    head_cols = (c_id == 2 * (d_id >> 7) + ((d_id >> 6) & 1)).astype(BF16)
    sums = _dot(prods, head_cols)
    k_max = jnp.max(sums, axis=0, keepdims=True)[:, 2:4]
    j_lo = [0]
    for i in range(1, nq):
        r = slice(i * T, (i + 1) * T)
        u = (NORM_SLACK * jnp.sqrt(jnp.max(sums[r], axis=0, keepdims=True)[:, 0:2] * k_max)
             - jnp.min(sums[r], axis=0, keepdims=True)[:, 4:6] + BOUND_SLACK)
        count = jnp.zeros((1, 1), jnp.int32)
        for j in range(i):
            skip = None
            for hh in range(2):
                bound = u[:, hh:hh + 1] + lc_ref[0, hh, i][:, 0:1] - lc_ref[0, hh, j][:, T - 1:T]
                skip_h = bound < -EXP2_UNDERFLOW
                skip = skip_h if skip is None else skip & skip_h
            count = count + skip.astype(jnp.int32)
        j_lo.append(count[0, 0])
    return j_lo


def _fox_write_schedule(j_lo, ti_ref, tj_ref, nq):
    pos = jnp.int32(0)
    pad_i = pad_j = jnp.int32(0)
    for i in range(1, nq):
        for j in range(i):
            keep = j >= j_lo[i]

            @pl.when(keep)
            def _(i=i, j=j, pos=pos):
                ti_ref[nq + pos] = jnp.int32(i)
                tj_ref[nq + pos] = jnp.int32(j)

            pos = pos + keep.astype(jnp.int32)
            pad_i = jnp.where(keep, pad_i, i)
            pad_j = jnp.where(keep, pad_j, j)
    n_full = jnp.maximum(((pos + 1) >> 1) << 1, MIN_FULL_TILES)
    for extra in range(MIN_FULL_TILES):
        @pl.when(pos + extra < n_full)
        def _(extra=extra):
            ti_ref[nq + pos + extra] = pad_i
            tj_ref[nq + pos + extra] = pad_j
    return n_full


def _flash_kernel(*refs, head_lanes, has_bias):
    T = ATTN_TILE
    if has_bias:
        q_ref, k_ref, v_ref, lc_ref, o_ref = refs[:5]
        scr = refs[5:-2]
        ti_ref, tj_ref = refs[-2:]
        nq = q_ref.shape[1] // T
        j_lo = _fox_skip_counts(q_ref, k_ref, lc_ref, nq)
    else:
        ti_ref, tj_ref, q_ref, k_ref, v_ref, o_ref = refs[:6]
        lc_ref = None
        scr = refs[6:]
        nq = q_ref.shape[1] // T
    p_scr, alpha_scr = scr[0:2], scr[2:4]
    m_scr, acc_scr, mask_scr = scr[4:]
    lane = lax.broadcasted_iota(jnp.int32, (T, LANES), 1)
    lo_half = lane < HALF
    zero = jnp.zeros((), BF16)

    def start(x, align):
        return x if isinstance(x, int) else pl.multiple_of(x, align)

    def rows(i):
        return pl.ds(start(i * T, T), T)

    def tile_of(t, diag):
        if diag:
            assert isinstance(t, int)
            return t, t
        return ti_ref[t], tj_ref[t]

    H = T // 2
    full_blocks = ((0, T, T),)
    diag_blocks = ((0, H, H), (H, H, T))

    def softmax(t, slot, diag):
        i, j = tile_of(t, diag)
        if head_lanes == HALF:
            q_pair = q_ref[0, rows(i), :]
            q_heads = [jnp.where(lo_half, q_pair, zero), jnp.where(lo_half, zero, q_pair)]
        for hh in range(2):
            for r0, nr, nc in (diag_blocks if diag else full_blocks):
                kr = pl.ds(start(j * T, T), nc)
                if head_lanes == LANES:
                    q_h = q_ref[0, pl.ds(start(i * T + r0, H), nr), hh * LANES:(hh + 1) * LANES]
                    k_t = k_ref[0, kr, hh * LANES:(hh + 1) * LANES]
                else:
                    q_h = q_heads[hh][r0:r0 + nr]
                    k_t = k_ref[0, kr, :]
                s = _dot_nt(q_h, k_t)
                if has_bias:
                    s = s + (lc_ref[0, hh, i][:, 0:1] - lc_ref[0, hh, j][:, :nc])
                if diag:
                    tri = s[:, nc - H:] + mask_scr[...]
                    s = tri if nc == H else jnp.concatenate([s[:, :nc - H], tri], axis=1)
                state_rows = pl.ds(start(i * T + r0, H), nr)
                m_cur = jnp.max(s, axis=1, keepdims=True)
                if diag:
                    m_next = jnp.broadcast_to(m_cur, (nr, LANES))
                else:
                    m_prev = m_scr[hh, state_rows, :]
                    m_next = jnp.maximum(m_prev, m_cur)
                    alpha_scr[slot][hh, r0:r0 + nr, :] = jnp.exp2(m_prev - m_next)
                p = jnp.exp2(s - jnp.concatenate([m_next] * (nc // LANES), axis=1))
                p_scr[slot][hh, r0:r0 + nr, :nc] = p.astype(BF16)
                m_scr[hh, state_rows, :] = m_next

    def accumulate(t, slot, diag):
        i, j = tile_of(t, diag)
        for hh in range(2):
            for r0, nr, nc in (diag_blocks if diag else full_blocks):
                state_rows = pl.ds(start(i * T + r0, H), nr)
                v_t = v_ref[0, pl.ds(start(j * T, T), nc), hh * LANES:(hh + 1) * LANES]
                pv = _dot(p_scr[slot][hh, r0:r0 + nr, :nc], v_t)
                if diag:
                    acc_scr[hh, state_rows, :] = pv
                else:
                    acc_scr[hh, state_rows, :] = (
                        alpha_scr[slot][hh, r0:r0 + nr, :] * acc_scr[hh, state_rows, :] + pv)

    row_id = lax.broadcasted_iota(jnp.int32, (H, H), 0)
    col_id = lax.broadcasted_iota(jnp.int32, (H, H), 1)
    mask_scr[...] = jnp.where(col_id <= row_id, 0.0, -jnp.inf).astype(F32)

    def is_diag(step):
        return step < nq

    def step_pair(t, t_static):
        softmax(t, 1, is_diag(t_static))
        accumulate(t - 1, 0, is_diag(t_static - 1))
        softmax(t + 1, 0, is_diag(t_static + 1))
        accumulate(t, 1, is_diag(t_static))

    def make_body(u_static):
        def body(u, carry):
            step_pair(4 * u + 3, 4 * u_static + 3)
            step_pair(4 * u + 5, 4 * u_static + 5)
            return carry
        return body

    assert nq % 4 == 0 and MIN_FULL_TILES == 4
    n_diag_bodies = (nq - 4) // 4
    softmax(0, 0, True)
    step_pair(1, 1)
    for u in range(n_diag_bodies):
        make_body(u)(u, 0)
    if has_bias:
        n_full = _fox_write_schedule(j_lo, ti_ref, tj_ref, nq)
        n_bodies = (n_full >> 2) + 1
        odd_pair = (n_full & 2) != 0
        last_step = nq + n_full - 1
    else:
        assert (ti_ref.shape[0] - 4) % 4 == 0
        n_bodies = (ti_ref.shape[0] - 4) // 4
        odd_pair = None
        last_step = ti_ref.shape[0] - 1
    make_body(n_diag_bodies)(n_diag_bodies, 0)
    lax.fori_loop(n_diag_bodies + 1, n_bodies, make_body(n_diag_bodies + 1), 0)
    if odd_pair is not None:
        @pl.when(odd_pair)
        def _():
            step_pair(4 * n_bodies + 3, 4 * (n_diag_bodies + 1) + 3)
    softmax(last_step, 1, False)
    accumulate(last_step - 1, 0, False)
    accumulate(last_step, 1, False)

    for i in range(nq):
        r = pl.ds(i * T, T)
        acc_even, acc_odd = acc_scr[0, r, :], acc_scr[1, r, :]
        num = jnp.where(lo_half, acc_even, acc_odd)
        den = pltpu.roll(jnp.where(lo_half, acc_odd, acc_even), HALF, 1)
        o_ref[0, r, :] = (num / den).astype(o_ref.dtype)


def _tile_schedule(nq):
    steps = [(i, i) for i in range(nq)] + [(i, j) for i in range(nq) for j in range(i)]
    assert (len(steps) - 4) % 4 == 0 and len(steps) - nq >= MIN_FULL_TILES
    return np.array(steps, np.int32)


def _flash(q, k, v, lc, *, q_col0, k_col0, v_col0, n_pairs, head_lanes):
    B, S, _ = q.shape
    T = ATTN_TILE
    qk_w = 2 * head_lanes
    qk_step = qk_w // LANES
    steps = _tile_schedule(S // T)
    in_specs = [
        pl.BlockSpec((1, S, qk_w), lambda b, p: (b, 0, q_col0 // qk_step + p)),
        pl.BlockSpec((1, S, qk_w), lambda b, p: (b, 0, k_col0 // qk_step + p)),
        pl.BlockSpec((1, S, 2 * LANES), lambda b, p: (b, 0, v_col0 // 2 + p)),
    ]
    args = [q, k, v]
    per_slot = lambda shape, dtype: [pltpu.VMEM(shape, dtype)] * 2
    scratch = (per_slot((2, T, T), BF16) + per_slot((2, T, LANES), F32)
               + [pltpu.VMEM((2, S, LANES), F32)] * 2
               + [pltpu.VMEM((T // 2, T // 2), F32)])
    if lc is not None:
        in_specs.append(pl.BlockSpec((1, 2) + lc.shape[2:], lambda b, p: (b, p, 0, 0, 0)))
        args.append(lc)
        scratch += [pltpu.SMEM((len(steps),), jnp.int32)] * 2
    else:
        smem = pl.BlockSpec(memory_space=pltpu.SMEM)
        in_specs = [smem, smem] + in_specs
        args = [jnp.asarray(steps[:, 0]), jnp.asarray(steps[:, 1])] + args
    return pl.pallas_call(
        functools.partial(_flash_kernel, head_lanes=head_lanes, has_bias=lc is not None),
        grid=(B, n_pairs),
        in_specs=in_specs,
        out_specs=pl.BlockSpec((1, S, LANES), lambda b, p: (b, 0, p)),
        out_shape=jax.ShapeDtypeStruct((B, S, n_pairs * LANES), BF16),
        scratch_shapes=scratch,
        compiler_params=_params(2),
        name="fox" if lc is not None else "mla",
    )(*args)


def _swa_kernel(sinks_ref, bias_ref, qs_ref, kvs_ref, o_ref):
    W = WINDOW
    group = SWA_HEADS // SWA_KV_HEADS
    step = pl.program_id(1)
    lane = lax.broadcasted_iota(jnp.int32, (W, LANES), 1)
    lo_half = lane < HALF
    zero = jnp.zeros((), BF16)

    for blk in range(SWA_ROWS // W):
        t0 = step * SWA_ROWS + blk * W
        k_start = pl.multiple_of(jnp.maximum(t0 - W, 0), W)
        bias_sel = jnp.where(step == 0, 0, 1) if blk == 0 else 1
        rows = slice(blk * W, (blk + 1) * W)
        for g in range(SWA_KV_HEADS):
            k_t = kvs_ref[0, pl.ds(k_start, 2 * W), g * LANES:(g + 1) * LANES]
            v_t = kvs_ref[0, pl.ds(k_start, 2 * W), (2 + g) * LANES:(3 + g) * LANES]
            q_parts = []
            for pair in range(group // 2):
                q_pair = qs_ref[0, rows, (g * 2 + pair) * LANES:(g * 2 + pair + 1) * LANES]
                q_parts.append(jnp.where(lo_half, q_pair, zero))
                q_parts.append(jnp.where(lo_half, zero, q_pair))
            s_all = _dot_nt(jnp.concatenate(q_parts, axis=0), k_t)
            p_parts, denoms = [], []
            for hq in range(group):
                head = g * group + hq
                sink = sinks_ref[head] * LOG2E
                s = s_all[hq * W:(hq + 1) * W] + bias_ref[bias_sel, head]
                m = jnp.maximum(jnp.max(s, axis=1, keepdims=True), sink)
                p = jnp.exp2(s - m)
                denoms.append(jnp.sum(p, axis=1, keepdims=True) + jnp.exp2(sink - m))
                p_parts.append(p.astype(BF16))
            o_all = _dot(jnp.concatenate(p_parts, axis=0), v_t)
            for pair in range(group // 2):
                o_even = o_all[(2 * pair) * W:(2 * pair + 1) * W] / denoms[2 * pair]
                o_odd = o_all[(2 * pair + 1) * W:(2 * pair + 2) * W] / denoms[2 * pair + 1]
                o_ref[0, rows, (g * 2 + pair) * LANES:(g * 2 + pair + 1) * LANES] = jnp.where(
                    lo_half, o_even, o_odd).astype(o_ref.dtype)


def _swa_bias():
    W = WINDOW
    slopes = 2.0 ** (-8.0 * (np.arange(SWA_HEADS, dtype=np.float32) + 1.0) / SWA_HEADS)
    a = np.arange(W, dtype=np.float32)[:, None]
    c = np.arange(2 * W, dtype=np.float32)[None, :]
    tables = []
    for k_offset in (0.0, float(W)):
        dist = k_offset + a - c
        bias = -slopes.astype(np.float32)[:, None, None] * dist[None]
        tables.append(np.where((dist >= 0) & (dist < W), bias * np.float32(LOG2E), -np.inf).astype(np.float32))
    return jnp.asarray(np.stack(tables))


def _swa(sinks, qs, kvs):
    B, S, C = qs.shape
    bias = _swa_bias()
    return pl.pallas_call(
        _swa_kernel,
        grid=(B, S // SWA_ROWS),
        in_specs=[pl.BlockSpec(memory_space=pltpu.SMEM), _resident(bias.shape),
                  pl.BlockSpec((1, SWA_ROWS, C), lambda b, i: (b, i, 0)),
                  pl.BlockSpec((1, S, kvs.shape[2]), lambda b, i: (b, 0, 0))],
        out_specs=pl.BlockSpec((1, SWA_ROWS, C), lambda b, i: (b, i, 0)),
        out_shape=jax.ShapeDtypeStruct((B, S, C), BF16),
        compiler_params=_params(2),
        name="swa",
    )(sinks, bias, qs, kvs)


def _mid_kernel(x_ref, om_ref, os_ref, gate_ref, w_out_ref, g_in_ref, w_in_ref, w_f_ref,
                x1_ref, qk_ref, v_ref, g1_ref, f_ref, h_scr):
    @pl.when(pl.program_id(0) == 0)
    def _():
        h_scr[...] = jnp.zeros(h_scr.shape, BF16)

    h = h_scr[...]
    w = FOX_WIDTH
    qk_ref[0, :, :w] = (_dot(h, w_in_ref[:, :w]) * LOG2E).astype(BF16)
    qk_ref[0, :, w:] = _dot(h, w_in_ref[:, w:2 * w]).astype(BF16)
    g1_ref[0] = _dot(h, w_in_ref[:, 3 * w:]).astype(BF16)
    f_ref[0] = _dot(h, w_f_ref[...])
    v = _dot(h, w_in_ref[:, 2 * w:3 * w])
    lo_half = lax.broadcasted_iota(jnp.int32, (v.shape[0], LANES), 1) < HALF
    for pair in range(FOX_HEADS // 2):
        v_pair = v[:, pair * LANES:(pair + 1) * LANES]
        v_ref[0, :, (2 * pair) * LANES:(2 * pair + 1) * LANES] = jnp.where(lo_half, v_pair, 1.0).astype(BF16)
        v_ref[0, :, (2 * pair + 1) * LANES:(2 * pair + 2) * LANES] = jnp.where(lo_half, 1.0, v_pair).astype(BF16)

    half = D_MODEL // 2
    gate = _silu(gate_ref[0].astype(F32))
    a_m = (om_ref[0].astype(F32) * gate[:, :half]).astype(BF16)
    a_s = (os_ref[0].astype(F32) * gate[:, half:]).astype(BF16)
    x1 = x_ref[0] + (_dot(a_m, w_out_ref[:half, :]) + _dot(a_s, w_out_ref[half:, :]))
    x1_ref[0] = x1
    h_scr[...] = _rms(x1, g_in_ref[...]).astype(BF16)


def _mid(x, o_mla, o_swa, gate, w_out, g_in, w_in, w_f):
    B, S, D = x.shape
    tm = PROJ_ROWS
    n_steps, this_tile, prev_tile, params = _pipelined_tiles(B, S // tm)
    this_spec = lambda width: pl.BlockSpec((1, tm, width), lambda g: this_tile(g) + (0,))
    prev_spec = lambda width: pl.BlockSpec((1, tm, width), lambda g: prev_tile(g) + (0,))
    outs = [(D, F32), (2 * FOX_WIDTH, BF16), (FOX_HEADS * LANES, BF16), (FOX_WIDTH, BF16), (LANES, F32)]
    return pl.pallas_call(
        _mid_kernel,
        grid=(n_steps,),
        in_specs=[this_spec(width) for width in (D, D // 2, D // 2, D)]
        + [_resident(a.shape) for a in (w_out, g_in, w_in, w_f)],
        out_specs=[this_spec(outs[0][0])] + [prev_spec(width) for width, _ in outs[1:]],
        out_shape=[jax.ShapeDtypeStruct((B, S, width), dtype) for width, dtype in outs],
        scratch_shapes=[pltpu.VMEM((tm, D), BF16)],
        compiler_params=params,
        name="mid",
    )(x, o_mla, o_swa, gate, w_out, g_in, w_in, w_f)


def _split3(x):
    hi = x.astype(BF16)
    r1 = x - hi.astype(F32)
    mid = r1.astype(BF16)
    lo = (r1 - mid.astype(F32)).astype(BF16)
    return hi, mid, lo


def _lcum_kernel(f_ref, b_ref, lc_ref):
    S = f_ref.shape[1]
    log_f = jax.nn.log_sigmoid(f_ref[0] + b_ref[...])
    log_f_t = log_f.T[:FOX_HEADS]
    n = CUM_CHUNK
    tri = (lax.broadcasted_iota(jnp.int32, (n, n), 0) <= lax.broadcasted_iota(jnp.int32, (n, n), 1)).astype(BF16)
    carry = jnp.zeros((FOX_HEADS, 1), F32)
    for c in range(S // n):
        hi, mid, lo = _split3(log_f_t[:, c * n:(c + 1) * n])
        cs = (_dot(hi, tri) + _dot(mid, tri)) + _dot(lo, tri) + carry
        cs2 = cs * LOG2E
        tile, lane0 = divmod(c * n, ATTN_TILE)
        for head in range(FOX_HEADS):
            lc_ref[0, head, tile, :, lane0:lane0 + n] = cs2[head:head + 1]
        carry = cs[:, n - 1:n]


def _lcum(f, b_pad):
    B, S, _ = f.shape
    out_block = (1, FOX_HEADS, S // ATTN_TILE, 1, ATTN_TILE)
    return pl.pallas_call(
        _lcum_kernel,
        grid=(B,),
        in_specs=[pl.BlockSpec((1, S, LANES), lambda b: (b, 0, 0)), _resident(b_pad.shape)],
        out_specs=pl.BlockSpec(out_block, lambda b: (b, 0, 0, 0, 0)),
        out_shape=jax.ShapeDtypeStruct((B,) + out_block[1:], F32),
        compiler_params=_params(1),
        name="lcum",
    )(f, b_pad)


def _final_kernel(x1_ref, o_ref, gate_ref, w_out_ref, g_ref, y_ref):
    a = (o_ref[0].astype(F32) * _silu(gate_ref[0].astype(F32))).astype(BF16)
    x2 = x1_ref[0] + _dot(a, w_out_ref[...])
    y_ref[0] = _rms(x2, g_ref[...])


def _final(x1, o_fox, gate, w_out, g_final):
    B, S, D = x1.shape
    tm = FINAL_ROWS
    row = pl.BlockSpec((1, tm, D), lambda b, i: (b, i, 0))
    return pl.pallas_call(
        _final_kernel,
        grid=(B, S // tm),
        in_specs=[row, row, row, _resident(w_out.shape), _resident(g_final.shape)],
        out_specs=row,
        out_shape=jax.ShapeDtypeStruct((B, S, D), F32),
        compiler_params=_params(2),
        name="final",
    )(x1, o_fox, gate, w_out, g_final)


def _pad_cols(w, left, total):
    return jnp.pad(w, ((0, 0), (left, total - left - w.shape[1])))


def _pad_rows(w, top, total):
    return jnp.pad(w, ((top, total - top - w.shape[0]), (0, 0)))


def _layer0_weights(w_in, w_q_up, w_kv_up):
    sizes = (MLA_Q_RANK, MLA_KV_RANK, MLA_ROPE, SWA_HEADS * SWA_HEAD_DIM,
             SWA_KV_HEADS * SWA_HEAD_DIM, SWA_KV_HEADS * SWA_HEAD_DIM, D_MODEL)
    cq, ckv, kpe, qs, ks, vs, gate = jnp.split(w_in.T, np.cumsum(sizes)[:-1].tolist(), axis=0)
    half = MLA_ROPE // 2
    kpe_block = _pad_rows(jnp.concatenate([kpe, kpe[half:], kpe[:half]], axis=0), MLA_NOPE, LANES)
    a_t = jnp.concatenate([cq, ckv, kpe_block], axis=0)
    d = SWA_HEAD_DIM
    kvs_t = jnp.concatenate([ks[:d], ks[:d], ks[d:], ks[d:], vs[:d], vs[:d], vs[d:], vs[d:]], axis=0)
    swa_scale = SWA_HEAD_DIM ** -0.5
    w_a, w_qs, w_kvs, w_gate = [w_t.astype(BF16).T for w_t in (a_t, qs * swa_scale, kvs_t, gate)]
    qk = MLA_NOPE + MLA_ROPE
    wq3 = w_q_up.reshape(MLA_Q_RANK, MLA_HEADS, qk)
    w_q = jnp.pad(wq3, ((0, 0), (0, 0), (0, LANES - qk))).reshape(MLA_Q_RANK, MLA_HEADS * LANES)
    wkv3 = w_kv_up.reshape(MLA_KV_RANK, MLA_HEADS, MLA_NOPE + MLA_V)
    w_k = jnp.pad(wkv3[:, :, :MLA_NOPE], ((0, 0), (0, 0), (0, LANES - MLA_NOPE))).reshape(MLA_KV_RANK, MLA_HEADS * LANES)
    wv4 = wkv3[:, :, MLA_NOPE:].reshape(MLA_KV_RANK, MLA_HEADS // 2, 2, MLA_V)
    zeros_v = jnp.zeros_like(wv4[:, :, 0])
    w_v = jnp.stack([wv4[:, :, 0], zeros_v, zeros_v, wv4[:, :, 1]], axis=2).reshape(MLA_KV_RANK, MLA_HEADS * LANES)
    return [w_a, w_qs, w_kvs, w_gate] + [w.astype(BF16) for w in (w_q, w_k, w_v)]


def _rope_table():
    half = MLA_ROPE // 2
    inv_freq = 1.0 / (ROPE_THETA ** (jnp.arange(0, MLA_ROPE, 2, dtype=F32) / MLA_ROPE))
    t = jnp.zeros((8, LANES), F32)
    t = t.at[0, MLA_NOPE:MLA_NOPE + half].set(inv_freq).at[0, MLA_NOPE + half:MLA_NOPE + MLA_ROPE].set(inv_freq)
    t = t.at[1, MLA_NOPE:MLA_NOPE + MLA_ROPE].set(1.0)
    t = t.at[2, MLA_NOPE:MLA_NOPE + half].set(-1.0).at[2, MLA_NOPE + half:MLA_NOPE + MLA_ROPE].set(1.0)
    t = t.at[3, :MLA_NOPE].set(1.0)
    t = t.at[4, MLA_NOPE:MLA_NOPE + half].set(-1.0).at[5, MLA_NOPE + half:MLA_NOPE + MLA_ROPE].set(1.0)
    return t


def kernel(x, positions, e_g_in, e_w_in, e_g_q_a, e_w_q_up, e_g_kv_a, e_w_kv_up, e_sinks,
           e_w_out, o_g_in, o_w_in, o_b_f, o_w_out, g_final):
    B, S, D = x.shape
    assert D == D_MODEL and all(S % rows == 0 for rows in (PROJ_ROWS, FINAL_ROWS, ATTN_TILE, SWA_ROWS))
    row = lambda g: g.reshape(1, -1).astype(F32)

    w0 = _layer0_weights(e_w_in[0], e_w_q_up[0], e_w_kv_up[0])
    w_a, w_qs, w_kvs, w_gate, w_q, w_k, w_v = w0
    qm, km, vm, qs, kvs, gate0 = _proj0(
        x, positions.reshape(B, S // PROJ_ROWS, 1, PROJ_ROWS), row(e_g_in[0]), w_a, w_qs, w_kvs, w_gate,
        row(e_g_q_a[0]), w_q, row(e_g_kv_a[0]), w_k, w_v, _rope_table())
    o_mla = _flash(qm, km, vm, None, q_col0=0, k_col0=0, v_col0=0,
                   n_pairs=MLA_HEADS // 2, head_lanes=LANES)
    o_swa = _swa(e_sinks[0].astype(F32), qs, kvs)

    sizes = (FOX_WIDTH, FOX_WIDTH, FOX_WIDTH, FOX_HEADS, FOX_WIDTH)
    w1_q, w1_k, w1_v, w1_f, w1_gate = jnp.split(o_w_in[0].T, np.cumsum(sizes)[:-1].tolist(), axis=0)
    fox_scale = FOX_HEAD_DIM ** -0.5
    w1 = jnp.concatenate([w1_q * fox_scale, w1_k, w1_v, w1_gate], axis=0).astype(BF16).T
    w1_f_pad = _pad_rows(w1_f, 0, LANES).astype(BF16).T
    x1, qk1, v1, gate1, f_logit = _mid(x, o_mla, o_swa, gate0, e_w_out[0].astype(BF16), row(o_g_in[0]), w1, w1_f_pad)

    lc = _lcum(f_logit, _pad_cols(row(o_b_f[0]), 0, LANES))
    o_fox = _flash(qk1, qk1, v1, lc, q_col0=0, k_col0=FOX_WIDTH // LANES, v_col0=0,
                   n_pairs=FOX_HEADS // 2, head_lanes=HALF)
    return _final(x1, o_fox, gate1, o_w_out[0].astype(BF16), row(g_final))
```

```python
import functools

import jax
import jax.numpy as jnp
import numpy as np
from jax import lax
from jax.experimental import pallas as pl
from jax.experimental.pallas import tpu as pltpu

F32 = jnp.float32
BF16 = jnp.bfloat16

D_MODEL = 1024
EPS = 1e-6
ROPE_THETA = 10000.0
WINDOW = 128

MLA_HEADS = 8
MLA_Q_RANK = 256
MLA_KV_RANK = 128
MLA_NOPE = 64
MLA_ROPE = 32
MLA_V = 64
SWA_HEADS = 8
SWA_KV_HEADS = 2
SWA_HEAD_DIM = 64
FOX_HEADS = 16
FOX_HEAD_DIM = 64
FOX_WIDTH = FOX_HEADS * FOX_HEAD_DIM

LANES = 128
HALF = 64
V7X_VMEM_LIMIT_BYTES = 56 * 1024 * 1024

PROJ_ROWS = 512
FINAL_ROWS = 1024
ATTN_TILE = 512
SWA_ROWS = 512
CUM_CHUNK = 256
LOG2E = 1.4426950408889634
EXP2_UNDERFLOW = 152.0
NORM_SLACK = 1.01
BOUND_SLACK = 2.0
MIN_FULL_TILES = 4


def _dot(a, b):
    return jnp.dot(a, b, preferred_element_type=F32)


def _dot_nt(a, b):
    return lax.dot_general(a, b, (((1,), (1,)), ((), ())), preferred_element_type=F32)


def _rms(xf, g):
    ms = jnp.mean(xf * xf, axis=-1, keepdims=True)
    return xf * lax.rsqrt(ms + EPS) * g


def _silu(x):
    return x * (1.0 / (1.0 + jnp.exp(-x)))


def _params(n_parallel):
    return pltpu.CompilerParams(
        dimension_semantics=("parallel",) * n_parallel,
        vmem_limit_bytes=V7X_VMEM_LIMIT_BYTES,
    )


def _resident(shape):
    nd = len(shape)
    return pl.BlockSpec(shape, lambda *_: (0,) * nd, pipeline_mode=pl.Buffered(1))


def _proj0_kernel(x_ref, pos_ref, g_in_ref, w_a_ref, w_qs_ref, w_kvs_ref, w_gate_ref,
                  g_q_ref, w_q_ref, g_kv_ref, w_k_ref, w_v_ref, rope_ref,
                  qm_ref, km_ref, vm_ref, qs_ref, kvs_ref, gate_ref):
    h = _rms(x_ref[0], g_in_ref[...]).astype(BF16)

    za = _dot(h, w_a_ref[...])
    cqn = _rms(za[:, :MLA_Q_RANK], g_q_ref[...]).astype(BF16)
    ckvn = _rms(za[:, MLA_Q_RANK:MLA_Q_RANK + MLA_KV_RANK], g_kv_ref[...]).astype(BF16)
    kpe = za[:, 384:512]
    kpe_sw = pltpu.roll(kpe, LANES - MLA_ROPE, 1)

    pos_row = pos_ref[0, 0].astype(F32)
    pos_col = jnp.broadcast_to(pos_row, (LANES, pos_row.shape[1])).T
    ang = pos_col * rope_ref[0:1, :]
    cos_t = jnp.cos(ang) * rope_ref[1:2, :]
    sin_t = jnp.sin(ang) * rope_ref[2:3, :]
    scale = (MLA_NOPE + MLA_ROPE) ** -0.5 * LOG2E
    q_cos = (cos_t + rope_ref[3:4, :]) * scale
    half = MLA_ROPE // 2
    q_sin_lo = jnp.sin(ang) * (rope_ref[4:5, :] * scale)
    q_sin_hi = jnp.sin(ang) * (rope_ref[5:6, :] * scale)
    kpe_rot = kpe * cos_t + kpe_sw * sin_t

    for c in range(MLA_HEADS // 2):
        cols = slice(c * 2 * LANES, (c + 1) * 2 * LANES)
        q = _dot(cqn, w_q_ref[:, cols])
        kn = _dot(ckvn, w_k_ref[:, cols])
        for hh in range(2):
            lo = hh * LANES
            q_h = q[:, lo:lo + LANES]
            q_rot = (q_h * q_cos + pltpu.roll(q_h, LANES - half, 1) * q_sin_lo
                     + pltpu.roll(q_h, half, 1) * q_sin_hi)
            qm_ref[0, :, c * 2 * LANES + lo:c * 2 * LANES + lo + LANES] = q_rot.astype(BF16)
            km_ref[0, :, c * 2 * LANES + lo:c * 2 * LANES + lo + LANES] = (
                kn[:, lo:lo + LANES] + kpe_rot).astype(BF16)
    v = _dot(ckvn, w_v_ref[...])
    lane_v = lax.broadcasted_iota(jnp.int32, v.shape, 1)
    ones_half = (((lane_v >> 6) ^ (lane_v >> 7)) & 1) == 1
    vm_ref[0] = jnp.where(ones_half, 1.0, v).astype(BF16)
    qs_ref[0] = (_dot(h, w_qs_ref[...]) * LOG2E).astype(BF16)
    kvs_ref[0] = _dot(h, w_kvs_ref[...]).astype(BF16)
    gate_ref[0] = _dot(h, w_gate_ref[...]).astype(BF16)


def _pipelined_tiles(n_batch, tiles_per_batch):
    n_tiles = n_batch * tiles_per_batch

    def this_tile(g):
        t = jnp.minimum(g, n_tiles - 1)
        return t // tiles_per_batch, t % tiles_per_batch

    def prev_tile(g):
        t = jnp.maximum(g - 1, 0)
        return t // tiles_per_batch, t % tiles_per_batch

    params = pltpu.CompilerParams(dimension_semantics=("arbitrary",), vmem_limit_bytes=V7X_VMEM_LIMIT_BYTES)
    return n_tiles + 1, this_tile, prev_tile, params


def _proj0(x, pos4, g_in, w_a, w_qs, w_kvs, w_gate, g_q, w_q, g_kv, w_k, w_v, rope):
    B, S, D = x.shape
    tm = PROJ_ROWS
    row = lambda width: pl.BlockSpec((1, tm, width), lambda b, i: (b, i, 0))
    out_widths = (MLA_HEADS * LANES, MLA_HEADS * LANES, MLA_HEADS * LANES,
                  SWA_HEADS * SWA_HEAD_DIM, 4 * LANES, D)
    return pl.pallas_call(
        _proj0_kernel,
        grid=(B, S // tm),
        in_specs=[row(D), pl.BlockSpec((1, 1, 1, tm), lambda b, i: (b, i, 0, 0))]
        + [_resident(a.shape) for a in (g_in, w_a, w_qs, w_kvs, w_gate, g_q, w_q, g_kv, w_k, w_v, rope)],
        out_specs=[row(w) for w in out_widths],
        out_shape=[jax.ShapeDtypeStruct((B, S, w), BF16) for w in out_widths],
        compiler_params=_params(2),
        name="proj0",
    )(x, pos4, g_in, w_a, w_qs, w_kvs, w_gate, g_q, w_q, g_kv, w_k, w_v, rope)


def _fox_skip_counts(q_ref, k_ref, lc_ref, nq):
    T = ATTN_TILE
    q_all, k_all = q_ref[0], k_ref[0]
    prods = jnp.concatenate([q_all * q_all, k_all * k_all, q_all * k_all], axis=1)
    d_id = lax.broadcasted_iota(jnp.int32, (3 * LANES, LANES), 0)
    c_id = lax.broadcasted_iota(jnp.int32, (3 * LANES, LANES), 1)
    head_cols = (c_id == 2 * (d_id >> 7) + ((d_id >> 6) & 1)).astype(BF16)
    sums = _dot(prods, head_cols)
    k_max = jnp.max(sums, axis=0, keepdims=True)[:, 2:4]
    j_lo = [0]
    for i in range(1, nq):
        r = slice(i * T, (i + 1) * T)
        u = (NORM_SLACK * jnp.sqrt(jnp.max(sums[r], axis=0, keepdims=True)[:, 0:2] * k_max)
             - jnp.min(sums[r], axis=0, keepdims=True)[:, 4:6] + BOUND_SLACK)
        count = jnp.zeros((1, 1), jnp.int32)
        for j in range(i):
            skip = None
            for hh in range(2):
                bound = u[:, hh:hh + 1] + lc_ref[0, hh, i][:, 0:1] - lc_ref[0, hh, j][:, T - 1:T]
                skip_h = bound < -EXP2_UNDERFLOW
                skip = skip_h if skip is None else skip & skip_h
            count = count + skip.astype(jnp.int32)
        j_lo.append(count[0, 0])
    return j_lo


def _fox_write_schedule(j_lo, ti_ref, tj_ref, nq):
    pos = jnp.int32(0)
    pad_i = pad_j = jnp.int32(0)
    for i in range(1, nq):
        for j in range(i):
            keep = j >= j_lo[i]

            @pl.when(keep)
            def _(i=i, j=j, pos=pos):
                ti_ref[nq + pos] = jnp.int32(i)
                tj_ref[nq + pos] = jnp.int32(j)

            pos = pos + keep.astype(jnp.int32)
            pad_i = jnp.where(keep, pad_i, i)
            pad_j = jnp.where(keep, pad_j, j)
    n_full = jnp.maximum(((pos + 1) >> 1) << 1, MIN_FULL_TILES)
    for extra in range(MIN_FULL_TILES):
        @pl.when(pos + extra < n_full)
        def _(extra=extra):
            ti_ref[nq + pos + extra] = pad_i
            tj_ref[nq + pos + extra] = pad_j
    return n_full


def _flash_kernel(*refs, head_lanes, has_bias):
    T = ATTN_TILE
    if has_bias:
        q_ref, k_ref, v_ref, lc_ref, o_ref = refs[:5]
        scr = refs[5:-2]
        ti_ref, tj_ref = refs[-2:]
        nq = q_ref.shape[1] // T
        j_lo = _fox_skip_counts(q_ref, k_ref, lc_ref, nq)
    else:
        ti_ref, tj_ref, q_ref, k_ref, v_ref, o_ref = refs[:6]
        lc_ref = None
        scr = refs[6:]
        nq = q_ref.shape[1] // T
    p_scr, alpha_scr = scr[0:2], scr[2:4]
    m_scr, acc_scr, mask_scr = scr[4:]
    lane = lax.broadcasted_iota(jnp.int32, (T, LANES), 1)
    lo_half = lane < HALF
    zero = jnp.zeros((), BF16)

    def start(x, align):
        return x if isinstance(x, int) else pl.multiple_of(x, align)

    def rows(i):
        return pl.ds(start(i * T, T), T)

    def tile_of(t, diag):
        if diag:
            assert isinstance(t, int)
            return t, t
        return ti_ref[t], tj_ref[t]

    H = T // 2
    full_blocks = ((0, T, T),)
    diag_blocks = ((0, H, H), (H, H, T))

    def softmax(t, slot, diag):
        i, j = tile_of(t, diag)
        if head_lanes == HALF:
            q_pair = q_ref[0, rows(i), :]
            q_heads = [jnp.where(lo_half, q_pair, zero), jnp.where(lo_half, zero, q_pair)]
        for hh in range(2):
            for r0, nr, nc in (diag_blocks if diag else full_blocks):
                kr = pl.ds(start(j * T, T), nc)
                if head_lanes == LANES:
                    q_h = q_ref[0, pl.ds(start(i * T + r0, H), nr), hh * LANES:(hh + 1) * LANES]
                    k_t = k_ref[0, kr, hh * LANES:(hh + 1) * LANES]
                else:
                    q_h = q_heads[hh][r0:r0 + nr]
                    k_t = k_ref[0, kr, :]
                s = _dot_nt(q_h, k_t)
                if has_bias:
                    s = s + (lc_ref[0, hh, i][:, 0:1] - lc_ref[0, hh, j][:, :nc])
                if diag:
                    tri = s[:, nc - H:] + mask_scr[...]
                    s = tri if nc == H else jnp.concatenate([s[:, :nc - H], tri], axis=1)
                state_rows = pl.ds(start(i * T + r0, H), nr)
                m_cur = jnp.max(s, axis=1, keepdims=True)
                if diag:
                    m_next = jnp.broadcast_to(m_cur, (nr, LANES))
                else:
                    m_prev = m_scr[hh, state_rows, :]
                    m_next = jnp.maximum(m_prev, m_cur)
                    alpha_scr[slot][hh, r0:r0 + nr, :] = jnp.exp2(m_prev - m_next)
                p = jnp.exp2(s - jnp.concatenate([m_next] * (nc // LANES), axis=1))
                p_scr[slot][hh, r0:r0 + nr, :nc] = p.astype(BF16)
                m_scr[hh, state_rows, :] = m_next

    def accumulate(t, slot, diag):
        i, j = tile_of(t, diag)
        for hh in range(2):
            for r0, nr, nc in (diag_blocks if diag else full_blocks):
                state_rows = pl.ds(start(i * T + r0, H), nr)
                v_t = v_ref[0, pl.ds(start(j * T, T), nc), hh * LANES:(hh + 1) * LANES]
                pv = _dot(p_scr[slot][hh, r0:r0 + nr, :nc], v_t)
                if diag:
                    acc_scr[hh, state_rows, :] = pv
                else:
                    acc_scr[hh, state_rows, :] = (
                        alpha_scr[slot][hh, r0:r0 + nr, :] * acc_scr[hh, state_rows, :] + pv)

    row_id = lax.broadcasted_iota(jnp.int32, (H, H), 0)
    col_id = lax.broadcasted_iota(jnp.int32, (H, H), 1)
    mask_scr[...] = jnp.where(col_id <= row_id, 0.0, -jnp.inf).astype(F32)

    def is_diag(step):
        return step < nq

    def step_pair(t, t_static):
        softmax(t, 1, is_diag(t_static))
        accumulate(t - 1, 0, is_diag(t_static - 1))
        softmax(t + 1, 0, is_diag(t_static + 1))
        accumulate(t, 1, is_diag(t_static))

    def make_body(u_static):
        def body(u, carry):
            step_pair(4 * u + 3, 4 * u_static + 3)
            step_pair(4 * u + 5, 4 * u_static + 5)
            return carry
        return body

    assert nq % 4 == 0 and MIN_FULL_TILES == 4
    n_diag_bodies = (nq - 4) // 4
    softmax(0, 0, True)
    step_pair(1, 1)
    for u in range(n_diag_bodies):
        make_body(u)(u, 0)
    if has_bias:
        n_full = _fox_write_schedule(j_lo, ti_ref, tj_ref, nq)
        n_bodies = (n_full >> 2) + 1
        odd_pair = (n_full & 2) != 0
        last_step = nq + n_full - 1
    else:
        assert (ti_ref.shape[0] - 4) % 4 == 0
        n_bodies = (ti_ref.shape[0] - 4) // 4
        odd_pair = None
        last_step = ti_ref.shape[0] - 1
    make_body(n_diag_bodies)(n_diag_bodies, 0)
    lax.fori_loop(n_diag_bodies + 1, n_bodies, make_body(n_diag_bodies + 1), 0)
    if odd_pair is not None:
        @pl.when(odd_pair)
        def _():
            step_pair(4 * n_bodies + 3, 4 * (n_diag_bodies + 1) + 3)
    softmax(last_step, 1, False)
    accumulate(last_step - 1, 0, False)
    accumulate(last_step, 1, False)

    for i in range(nq):
        r = pl.ds(i * T, T)
        acc_even, acc_odd = acc_scr[0, r, :], acc_scr[1, r, :]
        num = jnp.where(lo_half, acc_even, acc_odd)
        den = pltpu.roll(jnp.where(lo_half, acc_odd, acc_even), HALF, 1)
        o_ref[0, r, :] = (num / den).astype(o_ref.dtype)


def _tile_schedule(nq):
    steps = [(i, i) for i in range(nq)] + [(i, j) for i in range(nq) for j in range(i)]
    assert (len(steps) - 4) % 4 == 0 and len(steps) - nq >= MIN_FULL_TILES
    return np.array(steps, np.int32)


def _flash(q, k, v, lc, *, q_col0, k_col0, v_col0, n_pairs, head_lanes):
    B, S, _ = q.shape
    T = ATTN_TILE
    qk_w = 2 * head_lanes
    qk_step = qk_w // LANES
    steps = _tile_schedule(S // T)
    in_specs = [
        pl.BlockSpec((1, S, qk_w), lambda b, p: (b, 0, q_col0 // qk_step + p)),
        pl.BlockSpec((1, S, qk_w), lambda b, p: (b, 0, k_col0 // qk_step + p)),
        pl.BlockSpec((1, S, 2 * LANES), lambda b, p: (b, 0, v_col0 // 2 + p)),
    ]
    args = [q, k, v]
    per_slot = lambda shape, dtype: [pltpu.VMEM(shape, dtype)] * 2
    scratch = (per_slot((2, T, T), BF16) + per_slot((2, T, LANES), F32)
               + [pltpu.VMEM((2, S, LANES), F32)] * 2
               + [pltpu.VMEM((T // 2, T // 2), F32)])
    if lc is not None:
        in_specs.append(pl.BlockSpec((1, 2) + lc.shape[2:], lambda b, p: (b, p, 0, 0, 0)))
        args.append(lc)
        scratch += [pltpu.SMEM((len(steps),), jnp.int32)] * 2
    else:
        smem = pl.BlockSpec(memory_space=pltpu.SMEM)
        in_specs = [smem, smem] + in_specs
        args = [jnp.asarray(steps[:, 0]), jnp.asarray(steps[:, 1])] + args
    return pl.pallas_call(
        functools.partial(_flash_kernel, head_lanes=head_lanes, has_bias=lc is not None),
        grid=(B, n_pairs),
        in_specs=in_specs,
        out_specs=pl.BlockSpec((1, S, LANES), lambda b, p: (b, 0, p)),
        out_shape=jax.ShapeDtypeStruct((B, S, n_pairs * LANES), BF16),
        scratch_shapes=scratch,
        compiler_params=_params(2),
        name="fox" if lc is not None else "mla",
    )(*args)


def _swa_kernel(sinks_ref, bias_ref, qs_ref, kvs_ref, o_ref):
    W = WINDOW
    group = SWA_HEADS // SWA_KV_HEADS
    step = pl.program_id(1)
    lane = lax.broadcasted_iota(jnp.int32, (W, LANES), 1)
    lo_half = lane < HALF
    zero = jnp.zeros((), BF16)

    for blk in range(SWA_ROWS // W):
        t0 = step * SWA_ROWS + blk * W
        k_start = pl.multiple_of(jnp.maximum(t0 - W, 0), W)
        bias_sel = jnp.where(step == 0, 0, 1) if blk == 0 else 1
        rows = slice(blk * W, (blk + 1) * W)
        for g in range(SWA_KV_HEADS):
            k_t = kvs_ref[0, pl.ds(k_start, 2 * W), g * LANES:(g + 1) * LANES]
            v_t = kvs_ref[0, pl.ds(k_start, 2 * W), (2 + g) * LANES:(3 + g) * LANES]
            q_parts = []
            for pair in range(group // 2):
                q_pair = qs_ref[0, rows, (g * 2 + pair) * LANES:(g * 2 + pair + 1) * LANES]
                q_parts.append(jnp.where(lo_half, q_pair, zero))
                q_parts.append(jnp.where(lo_half, zero, q_pair))
            s_all = _dot_nt(jnp.concatenate(q_parts, axis=0), k_t)
            p_parts, denoms = [], []
            for hq in range(group):
                head = g * group + hq
                sink = sinks_ref[head] * LOG2E
                s = s_all[hq * W:(hq + 1) * W] + bias_ref[bias_sel, head]
                m = jnp.maximum(jnp.max(s, axis=1, keepdims=True), sink)
                p = jnp.exp2(s - m)
                denoms.append(jnp.sum(p, axis=1, keepdims=True) + jnp.exp2(sink - m))
                p_parts.append(p.astype(BF16))
            o_all = _dot(jnp.concatenate(p_parts, axis=0), v_t)
            for pair in range(group // 2):
                o_even = o_all[(2 * pair) * W:(2 * pair + 1) * W] / denoms[2 * pair]
                o_odd = o_all[(2 * pair + 1) * W:(2 * pair + 2) * W] / denoms[2 * pair + 1]
                o_ref[0, rows, (g * 2 + pair) * LANES:(g * 2 + pair + 1) * LANES] = jnp.where(
                    lo_half, o_even, o_odd).astype(o_ref.dtype)


def _swa_bias():
    W = WINDOW
    slopes = 2.0 ** (-8.0 * (np.arange(SWA_HEADS, dtype=np.float32) + 1.0) / SWA_HEADS)
    a = np.arange(W, dtype=np.float32)[:, None]
    c = np.arange(2 * W, dtype=np.float32)[None, :]
    tables = []
    for k_offset in (0.0, float(W)):
        dist = k_offset + a - c
        bias = -slopes.astype(np.float32)[:, None, None] * dist[None]
        tables.append(np.where((dist >= 0) & (dist < W), bias * np.float32(LOG2E), -np.inf).astype(np.float32))
    return jnp.asarray(np.stack(tables))


def _swa(sinks, qs, kvs):
    B, S, C = qs.shape
    bias = _swa_bias()
    return pl.pallas_call(
        _swa_kernel,
        grid=(B, S // SWA_ROWS),
        in_specs=[pl.BlockSpec(memory_space=pltpu.SMEM), _resident(bias.shape),
                  pl.BlockSpec((1, SWA_ROWS, C), lambda b, i: (b, i, 0)),
                  pl.BlockSpec((1, S, kvs.shape[2]), lambda b, i: (b, 0, 0))],
        out_specs=pl.BlockSpec((1, SWA_ROWS, C), lambda b, i: (b, i, 0)),
        out_shape=jax.ShapeDtypeStruct((B, S, C), BF16),
        compiler_params=_params(2),
        name="swa",
    )(sinks, bias, qs, kvs)


def _mid_kernel(x_ref, om_ref, os_ref, gate_ref, w_out_ref, g_in_ref, w_in_ref, w_f_ref,
                x1_ref, qk_ref, v_ref, g1_ref, f_ref, h_scr):
    @pl.when(pl.program_id(0) == 0)
    def _():
        h_scr[...] = jnp.zeros(h_scr.shape, BF16)

    h = h_scr[...]
    w = FOX_WIDTH
    qk_ref[0, :, :w] = (_dot(h, w_in_ref[:, :w]) * LOG2E).astype(BF16)
    qk_ref[0, :, w:] = _dot(h, w_in_ref[:, w:2 * w]).astype(BF16)
    g1_ref[0] = _dot(h, w_in_ref[:, 3 * w:]).astype(BF16)
    f_ref[0] = _dot(h, w_f_ref[...])
    v = _dot(h, w_in_ref[:, 2 * w:3 * w])
    lo_half = lax.broadcasted_iota(jnp.int32, (v.shape[0], LANES), 1) < HALF
    for pair in range(FOX_HEADS // 2):
        v_pair = v[:, pair * LANES:(pair + 1) * LANES]
        v_ref[0, :, (2 * pair) * LANES:(2 * pair + 1) * LANES] = jnp.where(lo_half, v_pair, 1.0).astype(BF16)
        v_ref[0, :, (2 * pair + 1) * LANES:(2 * pair + 2) * LANES] = jnp.where(lo_half, 1.0, v_pair).astype(BF16)

    half = D_MODEL // 2
    gate = _silu(gate_ref[0].astype(F32))
    a_m = (om_ref[0].astype(F32) * gate[:, :half]).astype(BF16)
    a_s = (os_ref[0].astype(F32) * gate[:, half:]).astype(BF16)
    x1 = x_ref[0] + (_dot(a_m, w_out_ref[:half, :]) + _dot(a_s, w_out_ref[half:, :]))
    x1_ref[0] = x1
    h_scr[...] = _rms(x1, g_in_ref[...]).astype(BF16)


def _mid(x, o_mla, o_swa, gate, w_out, g_in, w_in, w_f):
    B, S, D = x.shape
    tm = PROJ_ROWS
    n_steps, this_tile, prev_tile, params = _pipelined_tiles(B, S // tm)
    this_spec = lambda width: pl.BlockSpec((1, tm, width), lambda g: this_tile(g) + (0,))
    prev_spec = lambda width: pl.BlockSpec((1, tm, width), lambda g: prev_tile(g) + (0,))
    outs = [(D, F32), (2 * FOX_WIDTH, BF16), (FOX_HEADS * LANES, BF16), (FOX_WIDTH, BF16), (LANES, F32)]
    return pl.pallas_call(
        _mid_kernel,
        grid=(n_steps,),
        in_specs=[this_spec(width) for width in (D, D // 2, D // 2, D)]
        + [_resident(a.shape) for a in (w_out, g_in, w_in, w_f)],
        out_specs=[this_spec(outs[0][0])] + [prev_spec(width) for width, _ in outs[1:]],
        out_shape=[jax.ShapeDtypeStruct((B, S, width), dtype) for width, dtype in outs],
        scratch_shapes=[pltpu.VMEM((tm, D), BF16)],
        compiler_params=params,
        name="mid",
    )(x, o_mla, o_swa, gate, w_out, g_in, w_in, w_f)


def _split3(x):
    hi = x.astype(BF16)
    r1 = x - hi.astype(F32)
    mid = r1.astype(BF16)
    lo = (r1 - mid.astype(F32)).astype(BF16)
    return hi, mid, lo


def _lcum_kernel(f_ref, b_ref, lc_ref):
    S = f_ref.shape[1]
    log_f = jax.nn.log_sigmoid(f_ref[0] + b_ref[...])
    log_f_t = log_f.T[:FOX_HEADS]
    n = CUM_CHUNK
    tri = (lax.broadcasted_iota(jnp.int32, (n, n), 0) <= lax.broadcasted_iota(jnp.int32, (n, n), 1)).astype(BF16)
    carry = jnp.zeros((FOX_HEADS, 1), F32)
    for c in range(S // n):
        hi, mid, lo = _split3(log_f_t[:, c * n:(c + 1) * n])
        cs = (_dot(hi, tri) + _dot(mid, tri)) + _dot(lo, tri) + carry
        cs2 = cs * LOG2E
        tile, lane0 = divmod(c * n, ATTN_TILE)
        for head in range(FOX_HEADS):
            lc_ref[0, head, tile, :, lane0:lane0 + n] = cs2[head:head + 1]
        carry = cs[:, n - 1:n]


def _lcum(f, b_pad):
    B, S, _ = f.shape
    out_block = (1, FOX_HEADS, S // ATTN_TILE, 1, ATTN_TILE)
    return pl.pallas_call(
        _lcum_kernel,
        grid=(B,),
        in_specs=[pl.BlockSpec((1, S, LANES), lambda b: (b, 0, 0)), _resident(b_pad.shape)],
        out_specs=pl.BlockSpec(out_block, lambda b: (b, 0, 0, 0, 0)),
        out_shape=jax.ShapeDtypeStruct((B,) + out_block[1:], F32),
        compiler_params=_params(1),
        name="lcum",
    )(f, b_pad)


def _final_kernel(x1_ref, o_ref, gate_ref, w_out_ref, g_ref, y_ref):
    a = (o_ref[0].astype(F32) * _silu(gate_ref[0].astype(F32))).astype(BF16)
    x2 = x1_ref[0] + _dot(a, w_out_ref[...])
    y_ref[0] = _rms(x2, g_ref[...])


def _final(x1, o_fox, gate, w_out, g_final):
    B, S, D = x1.shape
    tm = FINAL_ROWS
    row = pl.BlockSpec((1, tm, D), lambda b, i: (b, i, 0))
    return pl.pallas_call(
        _final_kernel,
        grid=(B, S // tm),
        in_specs=[row, row, row, _resident(w_out.shape), _resident(g_final.shape)],
        out_specs=row,
        out_shape=jax.ShapeDtypeStruct((B, S, D), F32),
        compiler_params=_params(2),
        name="final",
    )(x1, o_fox, gate, w_out, g_final)


def _pad_cols(w, left, total):
    return jnp.pad(w, ((0, 0), (left, total - left - w.shape[1])))


def _pad_rows(w, top, total):
    return jnp.pad(w, ((top, total - top - w.shape[0]), (0, 0)))


def _layer0_weights(w_in, w_q_up, w_kv_up):
    sizes = (MLA_Q_RANK, MLA_KV_RANK, MLA_ROPE, SWA_HEADS * SWA_HEAD_DIM,
             SWA_KV_HEADS * SWA_HEAD_DIM, SWA_KV_HEADS * SWA_HEAD_DIM, D_MODEL)
    cq, ckv, kpe, qs, ks, vs, gate = jnp.split(w_in.astype(BF16).T, np.cumsum(sizes)[:-1].tolist(), axis=0)
    half = MLA_ROPE // 2
    kpe_block = _pad_rows(jnp.concatenate([kpe, kpe[half:], kpe[:half]], axis=0), MLA_NOPE, LANES)
    a_t = jnp.concatenate([cq, ckv, kpe_block], axis=0)
    d = SWA_HEAD_DIM
    kvs_t = jnp.concatenate([ks[:d], ks[:d], ks[d:], ks[d:], vs[:d], vs[:d], vs[d:], vs[d:]], axis=0)
    swa_scale = SWA_HEAD_DIM ** -0.5
    w_a, w_qs, w_kvs, w_gate = [w_t.astype(BF16).T for w_t in (a_t, qs * swa_scale, kvs_t, gate)]
    qk = MLA_NOPE + MLA_ROPE
    wq3 = w_q_up.reshape(MLA_Q_RANK, MLA_HEADS, qk)
    w_q = jnp.pad(wq3, ((0, 0), (0, 0), (0, LANES - qk))).reshape(MLA_Q_RANK, MLA_HEADS * LANES)
    wkv3 = w_kv_up.reshape(MLA_KV_RANK, MLA_HEADS, MLA_NOPE + MLA_V)
    w_k = jnp.pad(wkv3[:, :, :MLA_NOPE], ((0, 0), (0, 0), (0, LANES - MLA_NOPE))).reshape(MLA_KV_RANK, MLA_HEADS * LANES)
    wv4 = wkv3[:, :, MLA_NOPE:].reshape(MLA_KV_RANK, MLA_HEADS // 2, 2, MLA_V)
    zeros_v = jnp.zeros_like(wv4[:, :, 0])
    w_v = jnp.stack([wv4[:, :, 0], zeros_v, zeros_v, wv4[:, :, 1]], axis=2).reshape(MLA_KV_RANK, MLA_HEADS * LANES)
    return [w_a, w_qs, w_kvs, w_gate] + [w.astype(BF16) for w in (w_q, w_k, w_v)]


def _rope_table():
    half = MLA_ROPE // 2
    inv_freq = 1.0 / (ROPE_THETA ** (jnp.arange(0, MLA_ROPE, 2, dtype=F32) / MLA_ROPE))
    t = jnp.zeros((8, LANES), F32)
    t = t.at[0, MLA_NOPE:MLA_NOPE + half].set(inv_freq).at[0, MLA_NOPE + half:MLA_NOPE + MLA_ROPE].set(inv_freq)
    t = t.at[1, MLA_NOPE:MLA_NOPE + MLA_ROPE].set(1.0)
    t = t.at[2, MLA_NOPE:MLA_NOPE + half].set(-1.0).at[2, MLA_NOPE + half:MLA_NOPE + MLA_ROPE].set(1.0)
    t = t.at[3, :MLA_NOPE].set(1.0)
    t = t.at[4, MLA_NOPE:MLA_NOPE + half].set(-1.0).at[5, MLA_NOPE + half:MLA_NOPE + MLA_ROPE].set(1.0)
    return t


def kernel(x, positions, e_g_in, e_w_in, e_g_q_a, e_w_q_up, e_g_kv_a, e_w_kv_up, e_sinks,
           e_w_out, o_g_in, o_w_in, o_b_f, o_w_out, g_final):
    B, S, D = x.shape
    assert D == D_MODEL and all(S % rows == 0 for rows in (PROJ_ROWS, FINAL_ROWS, ATTN_TILE, SWA_ROWS))
    row = lambda g: g.reshape(1, -1).astype(F32)

    w0 = _layer0_weights(e_w_in[0], e_w_q_up[0], e_w_kv_up[0])
    w_a, w_qs, w_kvs, w_gate, w_q, w_k, w_v = w0
    qm, km, vm, qs, kvs, gate0 = _proj0(
        x, positions.reshape(B, S // PROJ_ROWS, 1, PROJ_ROWS), row(e_g_in[0]), w_a, w_qs, w_kvs, w_gate,
        row(e_g_q_a[0]), w_q, row(e_g_kv_a[0]), w_k, w_v, _rope_table())
    o_mla = _flash(qm, km, vm, None, q_col0=0, k_col0=0, v_col0=0,
                   n_pairs=MLA_HEADS // 2, head_lanes=LANES)
    o_swa = _swa(e_sinks[0].astype(F32), qs, kvs)

    sizes = (FOX_WIDTH, FOX_WIDTH, FOX_WIDTH, FOX_HEADS, FOX_WIDTH)
    w1_q, w1_k, w1_v, w1_f, w1_gate = jnp.split(o_w_in[0].astype(BF16).T, np.cumsum(sizes)[:-1].tolist(), axis=0)
    fox_scale = FOX_HEAD_DIM ** -0.5
    w1 = jnp.concatenate([w1_q * fox_scale, w1_k, w1_v, w1_gate], axis=0).astype(BF16).T
    w1_f_pad = _pad_rows(w1_f, 0, LANES).astype(BF16).T
    x1, qk1, v1, gate1, f_logit = _mid(x, o_mla, o_swa, gate0, e_w_out[0].astype(BF16), row(o_g_in[0]), w1, w1_f_pad)

    lc = _lcum(f_logit, _pad_cols(row(o_b_f[0]), 0, LANES))
    o_fox = _flash(qk1, qk1, v1, lc, q_col0=0, k_col0=FOX_WIDTH // LANES, v_col0=0,
                   n_pairs=FOX_HEADS // 2, head_lanes=HALF)
    return _final(x1, o_fox, gate1, o_w_out[0].astype(BF16), row(g_final))
```

```python
import functools

import jax
import jax.numpy as jnp
import numpy as np
from jax import lax
from jax.experimental import pallas as pl
from jax.experimental.pallas import tpu as pltpu

F32 = jnp.float32
BF16 = jnp.bfloat16

D_MODEL = 1024
EPS = 1e-6
ROPE_THETA = 10000.0
WINDOW = 128

MLA_HEADS = 8
MLA_Q_RANK = 256
MLA_KV_RANK = 128
MLA_NOPE = 64
MLA_ROPE = 32
MLA_V = 64
SWA_HEADS = 8
SWA_KV_HEADS = 2
SWA_HEAD_DIM = 64
FOX_HEADS = 16
FOX_HEAD_DIM = 64
FOX_WIDTH = FOX_HEADS * FOX_HEAD_DIM

LANES = 128
HALF = 64
V7X_VMEM_LIMIT_BYTES = 56 * 1024 * 1024

PROJ_ROWS = 512
FINAL_ROWS = 1024
ATTN_TILE = 512
SWA_ROWS = 1024
CUM_CHUNK = 256
LOG2E = 1.4426950408889634
EXP2_UNDERFLOW = 152.0
NORM_SLACK = 1.01
BOUND_SLACK = 2.0
MIN_FULL_TILES = 4


def _dot(a, b):
    return jnp.dot(a, b, preferred_element_type=F32)


def _dot_nt(a, b):
    return lax.dot_general(a, b, (((1,), (1,)), ((), ())), preferred_element_type=F32)


def _rms(xf, g):
    ms = jnp.mean(xf * xf, axis=-1, keepdims=True)
    return xf * lax.rsqrt(ms + EPS) * g


def _silu(x):
    return x * (1.0 / (1.0 + jnp.exp(-x)))


def _params(n_parallel):
    return pltpu.CompilerParams(
        dimension_semantics=("parallel",) * n_parallel,
        vmem_limit_bytes=V7X_VMEM_LIMIT_BYTES,
    )


def _resident(shape):
    nd = len(shape)
    return pl.BlockSpec(shape, lambda *_: (0,) * nd, pipeline_mode=pl.Buffered(1))


def _proj0_kernel(x_ref, pos_ref, g_in_ref, w_a_ref, w_qs_ref, w_kvs_ref, w_gate_ref,
                  g_q_ref, w_q_ref, g_kv_ref, w_k_ref, w_v_ref, rope_ref,
                  qm_ref, km_ref, vm_ref, qs_ref, kvs_ref, gate_ref):
    h = _rms(x_ref[0], g_in_ref[...]).astype(BF16)

    za = _dot(h, w_a_ref[...])
    cqn = _rms(za[:, :MLA_Q_RANK], g_q_ref[...]).astype(BF16)
    ckvn = _rms(za[:, MLA_Q_RANK:MLA_Q_RANK + MLA_KV_RANK], g_kv_ref[...]).astype(BF16)
    kpe = za[:, 384:512]
    kpe_sw = pltpu.roll(kpe, LANES - MLA_ROPE, 1)

    pos_row = pos_ref[0, 0].astype(F32)
    pos_col = jnp.broadcast_to(pos_row, (LANES, pos_row.shape[1])).T
    ang = pos_col * rope_ref[0:1, :]
    cos_t = jnp.cos(ang) * rope_ref[1:2, :]
    sin_t = jnp.sin(ang) * rope_ref[2:3, :]
    scale = (MLA_NOPE + MLA_ROPE) ** -0.5 * LOG2E
    q_cos = (cos_t + rope_ref[3:4, :]) * scale
    half = MLA_ROPE // 2
    q_sin_lo = jnp.sin(ang) * (rope_ref[4:5, :] * scale)
    q_sin_hi = jnp.sin(ang) * (rope_ref[5:6, :] * scale)
    kpe_rot = kpe * cos_t + kpe_sw * sin_t

    for c in range(MLA_HEADS // 2):
        cols = slice(c * 2 * LANES, (c + 1) * 2 * LANES)
        q = _dot(cqn, w_q_ref[:, cols])
        kn = _dot(ckvn, w_k_ref[:, cols])
        for hh in range(2):
            lo = hh * LANES
            q_h = q[:, lo:lo + LANES]
            q_rot = (q_h * q_cos + pltpu.roll(q_h, LANES - half, 1) * q_sin_lo
                     + pltpu.roll(q_h, half, 1) * q_sin_hi)
            qm_ref[0, :, c * 2 * LANES + lo:c * 2 * LANES + lo + LANES] = q_rot.astype(BF16)
            km_ref[0, :, c * 2 * LANES + lo:c * 2 * LANES + lo + LANES] = (
                kn[:, lo:lo + LANES] + kpe_rot).astype(BF16)
    v = _dot(ckvn, w_v_ref[...])
    lane_v = lax.broadcasted_iota(jnp.int32, v.shape, 1)
    ones_half = (((lane_v >> 6) ^ (lane_v >> 7)) & 1) == 1
    vm_ref[0] = jnp.where(ones_half, 1.0, v).astype(BF16)
    qs_ref[0] = (_dot(h, w_qs_ref[...]) * LOG2E).astype(BF16)
    kvs_ref[0] = _dot(h, w_kvs_ref[...]).astype(BF16)
    gate_ref[0] = _dot(h, w_gate_ref[...]).astype(BF16)


def _pipelined_tiles(n_batch, tiles_per_batch):
    n_tiles = n_batch * tiles_per_batch

    def this_tile(g):
        t = jnp.minimum(g, n_tiles - 1)
        return t // tiles_per_batch, t % tiles_per_batch

    def prev_tile(g):
        t = jnp.maximum(g - 1, 0)
        return t // tiles_per_batch, t % tiles_per_batch

    params = pltpu.CompilerParams(dimension_semantics=("arbitrary",), vmem_limit_bytes=V7X_VMEM_LIMIT_BYTES)
    return n_tiles + 1, this_tile, prev_tile, params


def _proj0(x, pos4, g_in, w_a, w_qs, w_kvs, w_gate, g_q, w_q, g_kv, w_k, w_v, rope):
    B, S, D = x.shape
    tm = PROJ_ROWS
    row = lambda width: pl.BlockSpec((1, tm, width), lambda b, i: (b, i, 0))
    out_widths = (MLA_HEADS * LANES, MLA_HEADS * LANES, MLA_HEADS * LANES,
                  SWA_HEADS * SWA_HEAD_DIM, 4 * LANES, D)
    return pl.pallas_call(
        _proj0_kernel,
        grid=(B, S // tm),
        in_specs=[row(D), pl.BlockSpec((1, 1, 1, tm), lambda b, i: (b, i, 0, 0))]
        + [_resident(a.shape) for a in (g_in, w_a, w_qs, w_kvs, w_gate, g_q, w_q, g_kv, w_k, w_v, rope)],
        out_specs=[row(w) for w in out_widths],
        out_shape=[jax.ShapeDtypeStruct((B, S, w), BF16) for w in out_widths],
        compiler_params=_params(2),
        name="proj0",
    )(x, pos4, g_in, w_a, w_qs, w_kvs, w_gate, g_q, w_q, g_kv, w_k, w_v, rope)


def _fox_skip_counts(q_ref, k_ref, lc_ref, nq):
    T = ATTN_TILE
    q_all, k_all = q_ref[0], k_ref[0]
    prods = jnp.concatenate([q_all * q_all, k_all * k_all, q_all * k_all], axis=1)
    d_id = lax.broadcasted_iota(jnp.int32, (3 * LANES, LANES), 0)
    c_id = lax.broadcasted_iota(jnp.int32, (3 * LANES, LANES), 1)
    head_cols = (c_id == 2 * (d_id >> 7) + ((d_id >> 6) & 1)).astype(BF16)
    sums = _dot(prods, head_cols)
    k_max = jnp.max(sums, axis=0, keepdims=True)[:, 2:4]
    j_lo = [0]
    for i in range(1, nq):
        r = slice(i * T, (i + 1) * T)
        u = (NORM_SLACK * jnp.sqrt(jnp.max(sums[r], axis=0, keepdims=True)[:, 0:2] * k_max)
             - jnp.min(sums[r], axis=0, keepdims=True)[:, 4:6] + BOUND_SLACK)
        count = jnp.zeros((1, 1), jnp.int32)
        for j in range(i):
            skip = None
            for hh in range(2):
                bound = u[:, hh:hh + 1] + lc_ref[0, hh, i][:, 0:1] - lc_ref[0, hh, j][:, T - 1:T]
                skip_h = bound < -EXP2_UNDERFLOW
                skip = skip_h if skip is None else skip & skip_h
            count = count + skip.astype(jnp.int32)
        j_lo.append(count[0, 0])
    return j_lo


def _fox_write_schedule(j_lo, ti_ref, tj_ref, nq):
    pos = jnp.int32(0)
    pad_i = pad_j = jnp.int32(0)
    for i in range(1, nq):
        for j in range(i):
            keep = j >= j_lo[i]

            @pl.when(keep)
            def _(i=i, j=j, pos=pos):
                ti_ref[nq + pos] = jnp.int32(i)
                tj_ref[nq + pos] = jnp.int32(j)

            pos = pos + keep.astype(jnp.int32)
            pad_i = jnp.where(keep, pad_i, i)
            pad_j = jnp.where(keep, pad_j, j)
    n_full = jnp.maximum(((pos + 1) >> 1) << 1, MIN_FULL_TILES)
    for extra in range(MIN_FULL_TILES):
        @pl.when(pos + extra < n_full)
        def _(extra=extra):
            ti_ref[nq + pos + extra] = pad_i
            tj_ref[nq + pos + extra] = pad_j
    return n_full


def _flash_kernel(*refs, head_lanes, has_bias):
    T = ATTN_TILE
    if has_bias:
        q_ref, k_ref, v_ref, lc_ref, o_ref = refs[:5]
        scr = refs[5:-2]
        ti_ref, tj_ref = refs[-2:]
        nq = q_ref.shape[1] // T
        j_lo = _fox_skip_counts(q_ref, k_ref, lc_ref, nq)
    else:
        ti_ref, tj_ref, q_ref, k_ref, v_ref, o_ref = refs[:6]
        lc_ref = None
        scr = refs[6:]
        nq = q_ref.shape[1] // T
    p_scr, alpha_scr = scr[0:2], scr[2:4]
    m_scr, acc_scr, mask_scr = scr[4:]
    lane = lax.broadcasted_iota(jnp.int32, (T, LANES), 1)
    lo_half = lane < HALF
    zero = jnp.zeros((), BF16)

    def start(x, align):
        return x if isinstance(x, int) else pl.multiple_of(x, align)

    def rows(i):
        return pl.ds(start(i * T, T), T)

    def tile_of(t, diag):
        if diag:
            assert isinstance(t, int)
            return t, t
        return ti_ref[t], tj_ref[t]

    H = T // 2
    full_blocks = ((0, T, T),)
    diag_blocks = ((0, H, H), (H, H, T))

    def softmax(t, slot, diag):
        i, j = tile_of(t, diag)
        if head_lanes == HALF:
            q_pair = q_ref[0, rows(i), :]
            q_heads = [jnp.where(lo_half, q_pair, zero), jnp.where(lo_half, zero, q_pair)]
        for hh in range(2):
            for r0, nr, nc in (diag_blocks if diag else full_blocks):
                kr = pl.ds(start(j * T, T), nc)
                if head_lanes == LANES:
                    q_h = q_ref[0, pl.ds(start(i * T + r0, H), nr), hh * LANES:(hh + 1) * LANES]
                    k_t = k_ref[0, kr, hh * LANES:(hh + 1) * LANES]
                else:
                    q_h = q_heads[hh][r0:r0 + nr]
                    k_t = k_ref[0, kr, :]
                s = _dot_nt(q_h, k_t)
                if has_bias:
                    s = s + (lc_ref[0, hh, i][:, 0:1] - lc_ref[0, hh, j][:, :nc])
                if diag:
                    tri = s[:, nc - H:] + mask_scr[...]
                    s = tri if nc == H else jnp.concatenate([s[:, :nc - H], tri], axis=1)
                state_rows = pl.ds(start(i * T + r0, H), nr)
                m_cur = jnp.max(s, axis=1, keepdims=True)
                if diag:
                    m_next = jnp.broadcast_to(m_cur, (nr, LANES))
                else:
                    m_prev = m_scr[hh, state_rows, :]
                    m_next = jnp.maximum(m_prev, m_cur)
                    alpha_scr[slot][hh, r0:r0 + nr, :] = jnp.exp2(m_prev - m_next)
                p = jnp.exp2(s - jnp.concatenate([m_next] * (nc // LANES), axis=1))
                p_scr[slot][hh, r0:r0 + nr, :nc] = p.astype(BF16)
                m_scr[hh, state_rows, :] = m_next

    def accumulate(t, slot, diag):
        i, j = tile_of(t, diag)
        for hh in range(2):
            for r0, nr, nc in (diag_blocks if diag else full_blocks):
                state_rows = pl.ds(start(i * T + r0, H), nr)
                v_t = v_ref[0, pl.ds(start(j * T, T), nc), hh * LANES:(hh + 1) * LANES]
                pv = _dot(p_scr[slot][hh, r0:r0 + nr, :nc], v_t)
                if diag:
                    acc_scr[hh, state_rows, :] = pv
                else:
                    acc_scr[hh, state_rows, :] = (
                        alpha_scr[slot][hh, r0:r0 + nr, :] * acc_scr[hh, state_rows, :] + pv)

    row_id = lax.broadcasted_iota(jnp.int32, (H, H), 0)
    col_id = lax.broadcasted_iota(jnp.int32, (H, H), 1)
    mask_scr[...] = jnp.where(col_id <= row_id, 0.0, -jnp.inf).astype(F32)

    def is_diag(step):
        return step < nq

    def step_pair(t, t_static):
        softmax(t, 1, is_diag(t_static))
        accumulate(t - 1, 0, is_diag(t_static - 1))
        softmax(t + 1, 0, is_diag(t_static + 1))
        accumulate(t, 1, is_diag(t_static))

    def make_body(u_static):
        def body(u, carry):
            step_pair(4 * u + 3, 4 * u_static + 3)
            step_pair(4 * u + 5, 4 * u_static + 5)
            return carry
        return body

    assert nq % 4 == 0 and MIN_FULL_TILES == 4
    n_diag_bodies = (nq - 4) // 4
    softmax(0, 0, True)
    step_pair(1, 1)
    for u in range(n_diag_bodies):
        make_body(u)(u, 0)
    if has_bias:
        n_full = _fox_write_schedule(j_lo, ti_ref, tj_ref, nq)
        n_bodies = (n_full >> 2) + 1
        odd_pair = (n_full & 2) != 0
        last_step = nq + n_full - 1
    else:
        assert (ti_ref.shape[0] - 4) % 4 == 0
        n_bodies = (ti_ref.shape[0] - 4) // 4
        odd_pair = None
        last_step = ti_ref.shape[0] - 1
    make_body(n_diag_bodies)(n_diag_bodies, 0)
    lax.fori_loop(n_diag_bodies + 1, n_bodies, make_body(n_diag_bodies + 1), 0)
    if odd_pair is not None:
        @pl.when(odd_pair)
        def _():
            step_pair(4 * n_bodies + 3, 4 * (n_diag_bodies + 1) + 3)
    softmax(last_step, 1, False)
    accumulate(last_step - 1, 0, False)
    accumulate(last_step, 1, False)

    for i in range(nq):
        r = pl.ds(i * T, T)
        acc_even, acc_odd = acc_scr[0, r, :], acc_scr[1, r, :]
        num = jnp.where(lo_half, acc_even, acc_odd)
        den = pltpu.roll(jnp.where(lo_half, acc_odd, acc_even), HALF, 1)
        o_ref[0, r, :] = (num / den).astype(o_ref.dtype)


def _tile_schedule(nq):
    steps = [(i, i) for i in range(nq)] + [(i, j) for i in range(nq) for j in range(i)]
    assert (len(steps) - 4) % 4 == 0 and len(steps) - nq >= MIN_FULL_TILES
    return np.array(steps, np.int32)


def _flash(q, k, v, lc, *, q_col0, k_col0, v_col0, n_pairs, head_lanes):
    B, S, _ = q.shape
    T = ATTN_TILE
    qk_w = 2 * head_lanes
    qk_step = qk_w // LANES
    steps = _tile_schedule(S // T)
    in_specs = [
        pl.BlockSpec((1, S, qk_w), lambda b, p: (b, 0, q_col0 // qk_step + p)),
        pl.BlockSpec((1, S, qk_w), lambda b, p: (b, 0, k_col0 // qk_step + p)),
        pl.BlockSpec((1, S, 2 * LANES), lambda b, p: (b, 0, v_col0 // 2 + p)),
    ]
    args = [q, k, v]
    per_slot = lambda shape, dtype: [pltpu.VMEM(shape, dtype)] * 2
    scratch = (per_slot((2, T, T), BF16) + per_slot((2, T, LANES), F32)
               + [pltpu.VMEM((2, S, LANES), F32)] * 2
               + [pltpu.VMEM((T // 2, T // 2), F32)])
    if lc is not None:
        in_specs.append(pl.BlockSpec((1, 2) + lc.shape[2:], lambda b, p: (b, p, 0, 0, 0)))
        args.append(lc)
        scratch += [pltpu.SMEM((len(steps),), jnp.int32)] * 2
    else:
        smem = pl.BlockSpec(memory_space=pltpu.SMEM)
        in_specs = [smem, smem] + in_specs
        args = [jnp.asarray(steps[:, 0]), jnp.asarray(steps[:, 1])] + args
    return pl.pallas_call(
        functools.partial(_flash_kernel, head_lanes=head_lanes, has_bias=lc is not None),
        grid=(B, n_pairs),
        in_specs=in_specs,
        out_specs=pl.BlockSpec((1, S, LANES), lambda b, p: (b, 0, p)),
        out_shape=jax.ShapeDtypeStruct((B, S, n_pairs * LANES), BF16),
        scratch_shapes=scratch,
        compiler_params=_params(2),
        name="fox" if lc is not None else "mla",
    )(*args)


def _swa_kernel(sinks_ref, bias_ref, qs_ref, kvs_ref, o_ref):
    W = WINDOW
    group = SWA_HEADS // SWA_KV_HEADS
    step = pl.program_id(1)
    lane = lax.broadcasted_iota(jnp.int32, (W, LANES), 1)
    lo_half = lane < HALF
    zero = jnp.zeros((), BF16)

    for blk in range(SWA_ROWS // W):
        t0 = step * SWA_ROWS + blk * W
        k_start = pl.multiple_of(jnp.maximum(t0 - W, 0), W)
        bias_sel = jnp.where(step == 0, 0, 1) if blk == 0 else 1
        rows = slice(blk * W, (blk + 1) * W)
        for g in range(SWA_KV_HEADS):
            k_t = kvs_ref[0, pl.ds(k_start, 2 * W), g * LANES:(g + 1) * LANES]
            v_t = kvs_ref[0, pl.ds(k_start, 2 * W), (2 + g) * LANES:(3 + g) * LANES]
            q_parts = []
            for pair in range(group // 2):
                q_pair = qs_ref[0, rows, (g * 2 + pair) * LANES:(g * 2 + pair + 1) * LANES]
                q_parts.append(jnp.where(lo_half, q_pair, zero))
                q_parts.append(jnp.where(lo_half, zero, q_pair))
            s_all = _dot_nt(jnp.concatenate(q_parts, axis=0), k_t)
            p_parts, denoms = [], []
            for hq in range(group):
                head = g * group + hq
                sink = sinks_ref[head] * LOG2E
                s = s_all[hq * W:(hq + 1) * W] + bias_ref[bias_sel, head]
                m = jnp.maximum(jnp.max(s, axis=1, keepdims=True), sink)
                p = jnp.exp2(s - m)
                denoms.append(jnp.sum(p, axis=1, keepdims=True) + jnp.exp2(sink - m))
                p_parts.append(p.astype(BF16))
            o_all = _dot(jnp.concatenate(p_parts, axis=0), v_t)
            for pair in range(group // 2):
                o_even = o_all[(2 * pair) * W:(2 * pair + 1) * W] / denoms[2 * pair]
                o_odd = o_all[(2 * pair + 1) * W:(2 * pair + 2) * W] / denoms[2 * pair + 1]
                o_ref[0, rows, (g * 2 + pair) * LANES:(g * 2 + pair + 1) * LANES] = jnp.where(
                    lo_half, o_even, o_odd).astype(o_ref.dtype)


def _swa_bias():
    W = WINDOW
    slopes = 2.0 ** (-8.0 * (np.arange(SWA_HEADS, dtype=np.float32) + 1.0) / SWA_HEADS)
    a = np.arange(W, dtype=np.float32)[:, None]
    c = np.arange(2 * W, dtype=np.float32)[None, :]
    tables = []
    for k_offset in (0.0, float(W)):
        dist = k_offset + a - c
        bias = -slopes.astype(np.float32)[:, None, None] * dist[None]
        tables.append(np.where((dist >= 0) & (dist < W), bias * np.float32(LOG2E), -np.inf).astype(np.float32))
    return jnp.asarray(np.stack(tables))


def _swa(sinks, qs, kvs):
    B, S, C = qs.shape
    bias = _swa_bias()
    return pl.pallas_call(
        _swa_kernel,
        grid=(B, S // SWA_ROWS),
        in_specs=[pl.BlockSpec(memory_space=pltpu.SMEM), _resident(bias.shape),
                  pl.BlockSpec((1, SWA_ROWS, C), lambda b, i: (b, i, 0)),
                  pl.BlockSpec((1, S, kvs.shape[2]), lambda b, i: (b, 0, 0))],
        out_specs=pl.BlockSpec((1, SWA_ROWS, C), lambda b, i: (b, i, 0)),
        out_shape=jax.ShapeDtypeStruct((B, S, C), BF16),
        compiler_params=_params(2),
        name="swa",
    )(sinks, bias, qs, kvs)


def _mid_kernel(x_ref, om_ref, os_ref, gate_ref, w_out_ref, g_in_ref, w_in_ref, w_f_ref,
                x1_ref, qk_ref, v_ref, g1_ref, f_ref, h_scr):
    @pl.when(pl.program_id(0) == 0)
    def _():
        h_scr[...] = jnp.zeros(h_scr.shape, BF16)

    h = h_scr[...]
    w = FOX_WIDTH
    qk_ref[0, :, :w] = (_dot(h, w_in_ref[:, :w]) * LOG2E).astype(BF16)
    qk_ref[0, :, w:] = _dot(h, w_in_ref[:, w:2 * w]).astype(BF16)
    g1_ref[0] = _dot(h, w_in_ref[:, 3 * w:]).astype(BF16)
    f_ref[0] = _dot(h, w_f_ref[...])
    v = _dot(h, w_in_ref[:, 2 * w:3 * w])
    lo_half = lax.broadcasted_iota(jnp.int32, (v.shape[0], LANES), 1) < HALF
    for pair in range(FOX_HEADS // 2):
        v_pair = v[:, pair * LANES:(pair + 1) * LANES]
        v_ref[0, :, (2 * pair) * LANES:(2 * pair + 1) * LANES] = jnp.where(lo_half, v_pair, 1.0).astype(BF16)
        v_ref[0, :, (2 * pair + 1) * LANES:(2 * pair + 2) * LANES] = jnp.where(lo_half, 1.0, v_pair).astype(BF16)

    half = D_MODEL // 2
    gate = _silu(gate_ref[0].astype(F32))
    a_m = (om_ref[0].astype(F32) * gate[:, :half]).astype(BF16)
    a_s = (os_ref[0].astype(F32) * gate[:, half:]).astype(BF16)
    x1 = x_ref[0] + (_dot(a_m, w_out_ref[:half, :]) + _dot(a_s, w_out_ref[half:, :]))
    x1_ref[0] = x1
    h_scr[...] = _rms(x1, g_in_ref[...]).astype(BF16)


def _mid(x, o_mla, o_swa, gate, w_out, g_in, w_in, w_f):
    B, S, D = x.shape
    tm = PROJ_ROWS
    n_steps, this_tile, prev_tile, params = _pipelined_tiles(B, S // tm)
    this_spec = lambda width: pl.BlockSpec((1, tm, width), lambda g: this_tile(g) + (0,))
    prev_spec = lambda width: pl.BlockSpec((1, tm, width), lambda g: prev_tile(g) + (0,))
    outs = [(D, F32), (2 * FOX_WIDTH, BF16), (FOX_HEADS * LANES, BF16), (FOX_WIDTH, BF16), (LANES, F32)]
    return pl.pallas_call(
        _mid_kernel,
        grid=(n_steps,),
        in_specs=[this_spec(width) for width in (D, D // 2, D // 2, D)]
        + [_resident(a.shape) for a in (w_out, g_in, w_in, w_f)],
        out_specs=[this_spec(outs[0][0])] + [prev_spec(width) for width, _ in outs[1:]],
        out_shape=[jax.ShapeDtypeStruct((B, S, width), dtype) for width, dtype in outs],
        scratch_shapes=[pltpu.VMEM((tm, D), BF16)],
        compiler_params=params,
        name="mid",
    )(x, o_mla, o_swa, gate, w_out, g_in, w_in, w_f)


def _split3(x):
    hi = x.astype(BF16)
    r1 = x - hi.astype(F32)
    mid = r1.astype(BF16)
    lo = (r1 - mid.astype(F32)).astype(BF16)
    return hi, mid, lo


def _lcum_kernel(f_ref, b_ref, lc_ref):
    S = f_ref.shape[1]
    log_f = jax.nn.log_sigmoid(f_ref[0] + b_ref[...])
    log_f_t = log_f.T[:FOX_HEADS]
    n = CUM_CHUNK
    tri = (lax.broadcasted_iota(jnp.int32, (n, n), 0) <= lax.broadcasted_iota(jnp.int32, (n, n), 1)).astype(BF16)
    carry = jnp.zeros((FOX_HEADS, 1), F32)
    for c in range(S // n):
        hi, mid, lo = _split3(log_f_t[:, c * n:(c + 1) * n])
        cs = (_dot(hi, tri) + _dot(mid, tri)) + _dot(lo, tri) + carry
        cs2 = cs * LOG2E
        tile, lane0 = divmod(c * n, ATTN_TILE)
        for head in range(FOX_HEADS):
            lc_ref[0, head, tile, :, lane0:lane0 + n] = cs2[head:head + 1]
        carry = cs[:, n - 1:n]


def _lcum(f, b_pad):
    B, S, _ = f.shape
    out_block = (1, FOX_HEADS, S // ATTN_TILE, 1, ATTN_TILE)
    return pl.pallas_call(
        _lcum_kernel,
        grid=(B,),
        in_specs=[pl.BlockSpec((1, S, LANES), lambda b: (b, 0, 0)), _resident(b_pad.shape)],
        out_specs=pl.BlockSpec(out_block, lambda b: (b, 0, 0, 0, 0)),
        out_shape=jax.ShapeDtypeStruct((B,) + out_block[1:], F32),
        compiler_params=_params(1),
        name="lcum",
    )(f, b_pad)


def _final_kernel(x1_ref, o_ref, gate_ref, w_out_ref, g_ref, y_ref):
    a = (o_ref[0].astype(F32) * _silu(gate_ref[0].astype(F32))).astype(BF16)
    x2 = x1_ref[0] + _dot(a, w_out_ref[...])
    y_ref[0] = _rms(x2, g_ref[...])


def _final(x1, o_fox, gate, w_out, g_final):
    B, S, D = x1.shape
    tm = FINAL_ROWS
    row = pl.BlockSpec((1, tm, D), lambda b, i: (b, i, 0))
    return pl.pallas_call(
        _final_kernel,
        grid=(B, S // tm),
        in_specs=[row, row, row, _resident(w_out.shape), _resident(g_final.shape)],
        out_specs=row,
        out_shape=jax.ShapeDtypeStruct((B, S, D), F32),
        compiler_params=_params(2),
        name="final",
    )(x1, o_fox, gate, w_out, g_final)


def _pad_cols(w, left, total):
    return jnp.pad(w, ((0, 0), (left, total - left - w.shape[1])))


def _pad_rows(w, top, total):
    return jnp.pad(w, ((top, total - top - w.shape[0]), (0, 0)))


def _layer0_weights(w_in, w_q_up, w_kv_up):
    sizes = (MLA_Q_RANK, MLA_KV_RANK, MLA_ROPE, SWA_HEADS * SWA_HEAD_DIM,
             SWA_KV_HEADS * SWA_HEAD_DIM, SWA_KV_HEADS * SWA_HEAD_DIM, D_MODEL)
    cq, ckv, kpe, qs, ks, vs, gate = jnp.split(w_in.astype(BF16).T, np.cumsum(sizes)[:-1].tolist(), axis=0)
    half = MLA_ROPE // 2
    kpe_block = _pad_rows(jnp.concatenate([kpe, kpe[half:], kpe[:half]], axis=0), MLA_NOPE, LANES)
    a_t = jnp.concatenate([cq, ckv, kpe_block], axis=0)
    d = SWA_HEAD_DIM
    kvs_t = jnp.concatenate([ks[:d], ks[:d], ks[d:], ks[d:], vs[:d], vs[:d], vs[d:], vs[d:]], axis=0)
    swa_scale = SWA_HEAD_DIM ** -0.5
    w_a, w_qs, w_kvs, w_gate = [w_t.astype(BF16).T for w_t in (a_t, qs * swa_scale, kvs_t, gate)]
    qk = MLA_NOPE + MLA_ROPE
    wq3 = w_q_up.reshape(MLA_Q_RANK, MLA_HEADS, qk)
    w_q = jnp.pad(wq3, ((0, 0), (0, 0), (0, LANES - qk))).reshape(MLA_Q_RANK, MLA_HEADS * LANES)
    wkv3 = w_kv_up.reshape(MLA_KV_RANK, MLA_HEADS, MLA_NOPE + MLA_V)
    w_k = jnp.pad(wkv3[:, :, :MLA_NOPE], ((0, 0), (0, 0), (0, LANES - MLA_NOPE))).reshape(MLA_KV_RANK, MLA_HEADS * LANES)
    wv4 = wkv3[:, :, MLA_NOPE:].reshape(MLA_KV_RANK, MLA_HEADS // 2, 2, MLA_V)
    zeros_v = jnp.zeros_like(wv4[:, :, 0])
    w_v = jnp.stack([wv4[:, :, 0], zeros_v, zeros_v, wv4[:, :, 1]], axis=2).reshape(MLA_KV_RANK, MLA_HEADS * LANES)
    return [w_a, w_qs, w_kvs, w_gate] + [w.astype(BF16) for w in (w_q, w_k, w_v)]


def _rope_table():
    half = MLA_ROPE // 2
    inv_freq = 1.0 / (ROPE_THETA ** (jnp.arange(0, MLA_ROPE, 2, dtype=F32) / MLA_ROPE))
    t = jnp.zeros((8, LANES), F32)
    t = t.at[0, MLA_NOPE:MLA_NOPE + half].set(inv_freq).at[0, MLA_NOPE + half:MLA_NOPE + MLA_ROPE].set(inv_freq)
    t = t.at[1, MLA_NOPE:MLA_NOPE + MLA_ROPE].set(1.0)
    t = t.at[2, MLA_NOPE:MLA_NOPE + half].set(-1.0).at[2, MLA_NOPE + half:MLA_NOPE + MLA_ROPE].set(1.0)
    t = t.at[3, :MLA_NOPE].set(1.0)
    t = t.at[4, MLA_NOPE:MLA_NOPE + half].set(-1.0).at[5, MLA_NOPE + half:MLA_NOPE + MLA_ROPE].set(1.0)
    return t


def kernel(x, positions, e_g_in, e_w_in, e_g_q_a, e_w_q_up, e_g_kv_a, e_w_kv_up, e_sinks,
           e_w_out, o_g_in, o_w_in, o_b_f, o_w_out, g_final):
    B, S, D = x.shape
    assert D == D_MODEL and all(S % rows == 0 for rows in (PROJ_ROWS, FINAL_ROWS, ATTN_TILE, SWA_ROWS))
    row = lambda g: g.reshape(1, -1).astype(F32)

    w0 = _layer0_weights(e_w_in[0], e_w_q_up[0], e_w_kv_up[0])
    w_a, w_qs, w_kvs, w_gate, w_q, w_k, w_v = w0
    qm, km, vm, qs, kvs, gate0 = _proj0(
        x, positions.reshape(B, S // PROJ_ROWS, 1, PROJ_ROWS), row(e_g_in[0]), w_a, w_qs, w_kvs, w_gate,
        row(e_g_q_a[0]), w_q, row(e_g_kv_a[0]), w_k, w_v, _rope_table())
    o_mla = _flash(qm, km, vm, None, q_col0=0, k_col0=0, v_col0=0,
                   n_pairs=MLA_HEADS // 2, head_lanes=LANES)
    o_swa = _swa(e_sinks[0].astype(F32), qs, kvs)

    sizes = (FOX_WIDTH, FOX_WIDTH, FOX_WIDTH, FOX_HEADS, FOX_WIDTH)
    w1_q, w1_k, w1_v, w1_f, w1_gate = jnp.split(o_w_in[0].astype(BF16).T, np.cumsum(sizes)[:-1].tolist(), axis=0)
    fox_scale = FOX_HEAD_DIM ** -0.5
    w1 = jnp.concatenate([w1_q * fox_scale, w1_k, w1_v, w1_gate], axis=0).astype(BF16).T
    w1_f_pad = _pad_rows(w1_f, 0, LANES).astype(BF16).T
    x1, qk1, v1, gate1, f_logit = _mid(x, o_mla, o_swa, gate0, e_w_out[0].astype(BF16), row(o_g_in[0]), w1, w1_f_pad)

    lc = _lcum(f_logit, _pad_cols(row(o_b_f[0]), 0, LANES))
    o_fox = _flash(qk1, qk1, v1, lc, q_col0=0, k_col0=FOX_WIDTH // LANES, v_col0=0,
                   n_pairs=FOX_HEADS // 2, head_lanes=HALF)
    return _final(x1, o_fox, gate1, o_w_out[0].astype(BF16), row(g_final))
```

```python
import functools

import jax
import jax.numpy as jnp
import numpy as np
from jax import lax
from jax.experimental import pallas as pl
from jax.experimental.pallas import tpu as pltpu

F32 = jnp.float32
BF16 = jnp.bfloat16

D_MODEL = 1024
EPS = 1e-6
ROPE_THETA = 10000.0
WINDOW = 128

MLA_HEADS = 8
MLA_Q_RANK = 256
MLA_KV_RANK = 128
MLA_NOPE = 64
MLA_ROPE = 32
MLA_V = 64
SWA_HEADS = 8
SWA_KV_HEADS = 2
SWA_HEAD_DIM = 64
FOX_HEADS = 16
FOX_HEAD_DIM = 64
FOX_WIDTH = FOX_HEADS * FOX_HEAD_DIM

LANES = 128
HALF = 64
V7X_VMEM_LIMIT_BYTES = 56 * 1024 * 1024

PROJ_ROWS = 512
FINAL_ROWS = 1024
ATTN_TILE = 512
SWA_ROWS = 1024
CUM_CHUNK = 256
LOG2E = 1.4426950408889634
EXP2_UNDERFLOW = 152.0
NORM_SLACK = 1.01
BOUND_SLACK = 2.0
MIN_FULL_TILES = 4


def _dot(a, b):
    return jnp.dot(a, b, preferred_element_type=F32)


def _dot_nt(a, b):
    return lax.dot_general(a, b, (((1,), (1,)), ((), ())), preferred_element_type=F32)


def _rms(xf, g):
    ms = jnp.mean(xf * xf, axis=-1, keepdims=True)
    return xf * lax.rsqrt(ms + EPS) * g


def _silu(x):
    return x * (1.0 / (1.0 + jnp.exp(-x)))


def _params(n_parallel):
    return pltpu.CompilerParams(
        dimension_semantics=("parallel",) * n_parallel,
        vmem_limit_bytes=V7X_VMEM_LIMIT_BYTES,
    )


def _resident(shape):
    nd = len(shape)
    return pl.BlockSpec(shape, lambda *_: (0,) * nd, pipeline_mode=pl.Buffered(1))


def _proj0_kernel(x_ref, pos_ref, g_in_ref, w_a_ref, w_qs_ref, w_kvs_ref, w_gate_ref,
                  g_q_ref, w_q_ref, g_kv_ref, w_k_ref, w_v_ref, rope_ref,
                  qm_ref, km_ref, vm_ref, qs_ref, kvs_ref, gate_ref):
    h = _rms(x_ref[0], g_in_ref[...]).astype(BF16)

    za = _dot(h, w_a_ref[...])
    cqn = _rms(za[:, :MLA_Q_RANK], g_q_ref[...]).astype(BF16)
    ckvn = _rms(za[:, MLA_Q_RANK:MLA_Q_RANK + MLA_KV_RANK], g_kv_ref[...]).astype(BF16)
    kpe = za[:, 384:512]
    kpe_sw = pltpu.roll(kpe, LANES - MLA_ROPE, 1)

    pos_row = pos_ref[0, 0].astype(F32)
    pos_col = jnp.broadcast_to(pos_row, (LANES, pos_row.shape[1])).T
    ang = pos_col * rope_ref[0:1, :]
    cos_t = jnp.cos(ang) * rope_ref[1:2, :]
    sin_t = jnp.sin(ang) * rope_ref[2:3, :]
    scale = (MLA_NOPE + MLA_ROPE) ** -0.5 * LOG2E
    q_cos = (cos_t + rope_ref[3:4, :]) * scale
    half = MLA_ROPE // 2
    q_sin_lo = jnp.sin(ang) * (rope_ref[4:5, :] * scale)
    q_sin_hi = jnp.sin(ang) * (rope_ref[5:6, :] * scale)
    kpe_rot = kpe * cos_t + kpe_sw * sin_t

    for c in range(MLA_HEADS // 2):
        cols = slice(c * 2 * LANES, (c + 1) * 2 * LANES)
        q = _dot(cqn, w_q_ref[:, cols])
        kn = _dot(ckvn, w_k_ref[:, cols])
        for hh in range(2):
            lo = hh * LANES
            q_h = q[:, lo:lo + LANES]
            q_rot = (q_h * q_cos + pltpu.roll(q_h, LANES - half, 1) * q_sin_lo
                     + pltpu.roll(q_h, half, 1) * q_sin_hi)
            qm_ref[0, :, c * 2 * LANES + lo:c * 2 * LANES + lo + LANES] = q_rot.astype(BF16)
            km_ref[0, :, c * 2 * LANES + lo:c * 2 * LANES + lo + LANES] = (
                kn[:, lo:lo + LANES] + kpe_rot).astype(BF16)
    v = _dot(ckvn, w_v_ref[...])
    lane_v = lax.broadcasted_iota(jnp.int32, v.shape, 1)
    ones_half = (((lane_v >> 6) ^ (lane_v >> 7)) & 1) == 1
    vm_ref[0] = jnp.where(ones_half, 1.0, v).astype(BF16)
    qs_ref[0] = (_dot(h, w_qs_ref[...]) * LOG2E).astype(BF16)
    kvs_ref[0] = _dot(h, w_kvs_ref[...]).astype(BF16)
    gate_ref[0] = _dot(h, w_gate_ref[...]).astype(BF16)


def _pipelined_tiles(n_batch, tiles_per_batch):
    n_tiles = n_batch * tiles_per_batch

    def this_tile(g):
        t = jnp.minimum(g, n_tiles - 1)
        return t // tiles_per_batch, t % tiles_per_batch

    def prev_tile(g):
        t = jnp.maximum(g - 1, 0)
        return t // tiles_per_batch, t % tiles_per_batch

    params = pltpu.CompilerParams(dimension_semantics=("arbitrary",), vmem_limit_bytes=V7X_VMEM_LIMIT_BYTES)
    return n_tiles + 1, this_tile, prev_tile, params


def _proj0(x, pos4, g_in, w_a, w_qs, w_kvs, w_gate, g_q, w_q, g_kv, w_k, w_v, rope):
    B, S, D = x.shape
    tm = PROJ_ROWS
    row = lambda width: pl.BlockSpec((1, tm, width), lambda b, i: (b, i, 0))
    out_widths = (MLA_HEADS * LANES, MLA_HEADS * LANES, MLA_HEADS * LANES,
                  SWA_HEADS * SWA_HEAD_DIM, 4 * LANES, D)
    return pl.pallas_call(
        _proj0_kernel,
        grid=(B, S // tm),
        in_specs=[row(D), pl.BlockSpec((1, 1, 1, tm), lambda b, i: (b, i, 0, 0))]
        + [_resident(a.shape) for a in (g_in, w_a, w_qs, w_kvs, w_gate, g_q, w_q, g_kv, w_k, w_v, rope)],
        out_specs=[row(w) for w in out_widths],
        out_shape=[jax.ShapeDtypeStruct((B, S, w), BF16) for w in out_widths],
        compiler_params=_params(2),
        name="proj0",
    )(x, pos4, g_in, w_a, w_qs, w_kvs, w_gate, g_q, w_q, g_kv, w_k, w_v, rope)


def _fox_skip_counts(q_ref, k_ref, lc_ref, nq):
    T = ATTN_TILE
    q_all, k_all = q_ref[0], k_ref[0]
    prods = jnp.concatenate([q_all * q_all, k_all * k_all, q_all * k_all], axis=1)
    d_id = lax.broadcasted_iota(jnp.int32, (3 * LANES, LANES), 0)
    c_id = lax.broadcasted_iota(jnp.int32, (3 * LANES, LANES), 1)
    head_cols = (c_id == 2 * (d_id >> 7) + ((d_id >> 6) & 1)).astype(BF16)
    sums = _dot(prods, head_cols)
    k_max = jnp.max(sums, axis=0, keepdims=True)[:, 2:4]
    j_lo = [0]
    for i in range(1, nq):
        r = slice(i * T, (i + 1) * T)
        u = (NORM_SLACK * jnp.sqrt(jnp.max(sums[r], axis=0, keepdims=True)[:, 0:2] * k_max)
             - jnp.min(sums[r], axis=0, keepdims=True)[:, 4:6] + BOUND_SLACK)
        count = jnp.zeros((1, 1), jnp.int32)
        for j in range(i):
            skip = None
            for hh in range(2):
                bound = u[:, hh:hh + 1] + lc_ref[0, hh, i][:, 0:1] - lc_ref[0, hh, j][:, T - 1:T]
                skip_h = bound < -EXP2_UNDERFLOW
                skip = skip_h if skip is None else skip & skip_h
            count = count + skip.astype(jnp.int32)
        j_lo.append(count[0, 0])
    return j_lo


def _fox_write_schedule(j_lo, ti_ref, tj_ref, nq):
    pos = jnp.int32(0)
    pad_i = pad_j = jnp.int32(0)
    for i in range(1, nq):
        for j in range(i):
            keep = j >= j_lo[i]

            @pl.when(keep)
            def _(i=i, j=j, pos=pos):
                ti_ref[nq + pos] = jnp.int32(i)
                tj_ref[nq + pos] = jnp.int32(j)

            pos = pos + keep.astype(jnp.int32)
            pad_i = jnp.where(keep, pad_i, i)
            pad_j = jnp.where(keep, pad_j, j)
    n_full = jnp.maximum(((pos + 1) >> 1) << 1, MIN_FULL_TILES)
    for extra in range(MIN_FULL_TILES):
        @pl.when(pos + extra < n_full)
        def _(extra=extra):
            ti_ref[nq + pos + extra] = pad_i
            tj_ref[nq + pos + extra] = pad_j
    return n_full


def _flash_kernel(*refs, head_lanes, has_bias):
    T = ATTN_TILE
    if has_bias:
        q_ref, k_ref, v_ref, lc_ref, o_ref = refs[:5]
        scr = refs[5:-2]
        ti_ref, tj_ref = refs[-2:]
        nq = q_ref.shape[1] // T
        j_lo = _fox_skip_counts(q_ref, k_ref, lc_ref, nq)
    else:
        ti_ref, tj_ref, q_ref, k_ref, v_ref, o_ref = refs[:6]
        lc_ref = None
        scr = refs[6:]
        nq = q_ref.shape[1] // T
    p_scr, alpha_scr = scr[0:2], scr[2:4]
    m_scr, acc_scr, mask_scr = scr[4:]
    lane = lax.broadcasted_iota(jnp.int32, (T, LANES), 1)
    lo_half = lane < HALF
    zero = jnp.zeros((), BF16)

    def start(x, align):
        return x if isinstance(x, int) else pl.multiple_of(x, align)

    def rows(i):
        return pl.ds(start(i * T, T), T)

    def tile_of(t, diag):
        if diag:
            assert isinstance(t, int)
            return t, t
        return ti_ref[t], tj_ref[t]

    H = T // 2
    full_blocks = ((0, T, T),)
    diag_blocks = ((0, H, H), (H, H, T))

    def softmax(t, slot, diag):
        i, j = tile_of(t, diag)
        if head_lanes == HALF:
            q_pair = q_ref[0, rows(i), :]
            q_heads = [jnp.where(lo_half, q_pair, zero), jnp.where(lo_half, zero, q_pair)]
        for hh in range(2):
            for r0, nr, nc in (diag_blocks if diag else full_blocks):
                kr = pl.ds(start(j * T, T), nc)
                if head_lanes == LANES:
                    q_h = q_ref[0, pl.ds(start(i * T + r0, H), nr), hh * LANES:(hh + 1) * LANES]
                    k_t = k_ref[0, kr, hh * LANES:(hh + 1) * LANES]
                else:
                    q_h = q_heads[hh][r0:r0 + nr]
                    k_t = k_ref[0, kr, :]
                s = _dot_nt(q_h, k_t)
                if has_bias:
                    s = s + (lc_ref[0, hh, i][:, 0:1] - lc_ref[0, hh, j][:, :nc])
                if diag:
                    tri = s[:, nc - H:] + mask_scr[...]
                    s = tri if nc == H else jnp.concatenate([s[:, :nc - H], tri], axis=1)
                state_rows = pl.ds(start(i * T + r0, H), nr)
                m_cur = jnp.max(s, axis=1, keepdims=True)
                if diag:
                    m_next = jnp.broadcast_to(m_cur, (nr, LANES))
                else:
                    m_prev = m_scr[hh, state_rows, :]
                    m_next = jnp.maximum(m_prev, m_cur)
                    alpha_scr[slot][hh, r0:r0 + nr, :] = jnp.exp2(m_prev - m_next)
                p = jnp.exp2(s - jnp.concatenate([m_next] * (nc // LANES), axis=1))
                p_scr[slot][hh, r0:r0 + nr, :nc] = p.astype(BF16)
                m_scr[hh, state_rows, :] = m_next

    def accumulate(t, slot, diag):
        i, j = tile_of(t, diag)
        for hh in range(2):
            for r0, nr, nc in (diag_blocks if diag else full_blocks):
                state_rows = pl.ds(start(i * T + r0, H), nr)
                v_t = v_ref[0, pl.ds(start(j * T, T), nc), hh * LANES:(hh + 1) * LANES]
                pv = _dot(p_scr[slot][hh, r0:r0 + nr, :nc], v_t)
                if diag:
                    acc_scr[hh, state_rows, :] = pv
                else:
                    acc_scr[hh, state_rows, :] = (
                        alpha_scr[slot][hh, r0:r0 + nr, :] * acc_scr[hh, state_rows, :] + pv)

    row_id = lax.broadcasted_iota(jnp.int32, (H, H), 0)
    col_id = lax.broadcasted_iota(jnp.int32, (H, H), 1)
    mask_scr[...] = jnp.where(col_id <= row_id, 0.0, -jnp.inf).astype(F32)

    def is_diag(step):
        return step < nq

    def step_pair(t, t_static):
        softmax(t, 1, is_diag(t_static))
        accumulate(t - 1, 0, is_diag(t_static - 1))
        softmax(t + 1, 0, is_diag(t_static + 1))
        accumulate(t, 1, is_diag(t_static))

    def make_body(u_static):
        def body(u, carry):
            step_pair(4 * u + 3, 4 * u_static + 3)
            step_pair(4 * u + 5, 4 * u_static + 5)
            return carry
        return body

    assert nq % 4 == 0 and MIN_FULL_TILES == 4
    n_diag_bodies = (nq - 4) // 4
    softmax(0, 0, True)
    step_pair(1, 1)
    for u in range(n_diag_bodies):
        make_body(u)(u, 0)
    if has_bias:
        n_full = _fox_write_schedule(j_lo, ti_ref, tj_ref, nq)
        n_bodies = (n_full >> 2) + 1
        odd_pair = (n_full & 2) != 0
        last_step = nq + n_full - 1
    else:
        assert (ti_ref.shape[0] - 4) % 4 == 0
        n_bodies = (ti_ref.shape[0] - 4) // 4
        odd_pair = None
        last_step = ti_ref.shape[0] - 1
    make_body(n_diag_bodies)(n_diag_bodies, 0)
    lax.fori_loop(n_diag_bodies + 1, n_bodies, make_body(n_diag_bodies + 1), 0)
    if odd_pair is not None:
        @pl.when(odd_pair)
        def _():
            step_pair(4 * n_bodies + 3, 4 * (n_diag_bodies + 1) + 3)
    softmax(last_step, 1, False)
    accumulate(last_step - 1, 0, False)
    accumulate(last_step, 1, False)

    for i in range(nq):
        r = pl.ds(i * T, T)
        acc_even, acc_odd = acc_scr[0, r, :], acc_scr[1, r, :]
        num = jnp.where(lo_half, acc_even, acc_odd)
        den = pltpu.roll(jnp.where(lo_half, acc_odd, acc_even), HALF, 1)
        o_ref[0, r, :] = (num / den).astype(o_ref.dtype)


def _tile_schedule(nq):
    steps = [(i, i) for i in range(nq)] + [(i, j) for i in range(nq) for j in range(i)]
    assert (len(steps) - 4) % 4 == 0 and len(steps) - nq >= MIN_FULL_TILES
    return np.array(steps, np.int32)


def _flash(q, k, v, lc, *, q_col0, k_col0, v_col0, n_pairs, head_lanes):
    B, S, _ = q.shape
    T = ATTN_TILE
    qk_w = 2 * head_lanes
    qk_step = qk_w // LANES
    steps = _tile_schedule(S // T)
    in_specs = [
        pl.BlockSpec((1, S, qk_w), lambda b, p: (b, 0, q_col0 // qk_step + p)),
        pl.BlockSpec((1, S, qk_w), lambda b, p: (b, 0, k_col0 // qk_step + p)),
        pl.BlockSpec((1, S, 2 * LANES), lambda b, p: (b, 0, v_col0 // 2 + p)),
    ]
    args = [q, k, v]
    per_slot = lambda shape, dtype: [pltpu.VMEM(shape, dtype)] * 2
    scratch = (per_slot((2, T, T), BF16) + per_slot((2, T, LANES), F32)
               + [pltpu.VMEM((2, S, LANES), F32)] * 2
               + [pltpu.VMEM((T // 2, T // 2), F32)])
    if lc is not None:
        in_specs.append(pl.BlockSpec((1, 2) + lc.shape[2:], lambda b, p: (b, p, 0, 0, 0)))
        args.append(lc)
        scratch += [pltpu.SMEM((len(steps),), jnp.int32)] * 2
    else:
        smem = pl.BlockSpec(memory_space=pltpu.SMEM)
        in_specs = [smem, smem] + in_specs
        args = [jnp.asarray(steps[:, 0]), jnp.asarray(steps[:, 1])] + args
    return pl.pallas_call(
        functools.partial(_flash_kernel, head_lanes=head_lanes, has_bias=lc is not None),
        grid=(B, n_pairs),
        in_specs=in_specs,
        out_specs=pl.BlockSpec((1, S, LANES), lambda b, p: (b, 0, p)),
        out_shape=jax.ShapeDtypeStruct((B, S, n_pairs * LANES), BF16),
        scratch_shapes=scratch,
        compiler_params=_params(2),
        name="fox" if lc is not None else "mla",
    )(*args)


def _swa_kernel(sinks_ref, bias_ref, qs_ref, kvs_ref, o_ref):
    W = WINDOW
    group = SWA_HEADS // SWA_KV_HEADS
    step = pl.program_id(1)
    lane = lax.broadcasted_iota(jnp.int32, (W, LANES), 1)
    lo_half = lane < HALF
    zero = jnp.zeros((), BF16)

    for blk in range(SWA_ROWS // W):
        t0 = step * SWA_ROWS + blk * W
        k_start = pl.multiple_of(jnp.maximum(t0 - W, 0), W)
        bias_sel = jnp.where(step == 0, 0, 1) if blk == 0 else 1
        rows = slice(blk * W, (blk + 1) * W)
        for g in range(SWA_KV_HEADS):
            k_t = kvs_ref[0, pl.ds(k_start, 2 * W), g * LANES:(g + 1) * LANES]
            v_t = kvs_ref[0, pl.ds(k_start, 2 * W), (2 + g) * LANES:(3 + g) * LANES]
            q_parts = []
            for pair in range(group // 2):
                q_pair = qs_ref[0, rows, (g * 2 + pair) * LANES:(g * 2 + pair + 1) * LANES]
                q_parts.append(jnp.where(lo_half, q_pair, zero))
                q_parts.append(jnp.where(lo_half, zero, q_pair))
            s_all = _dot_nt(jnp.concatenate(q_parts, axis=0), k_t)
            p_parts, denoms = [], []
            for hq in range(group):
                head = g * group + hq
                sink = sinks_ref[head] * LOG2E
                s = s_all[hq * W:(hq + 1) * W] + bias_ref[bias_sel, head]
                m = jnp.maximum(jnp.max(s, axis=1, keepdims=True), sink)
                p = jnp.exp2(s - m)
                denoms.append(jnp.sum(p, axis=1, keepdims=True) + jnp.exp2(sink - m))
                p_parts.append(p.astype(BF16))
            o_all = _dot(jnp.concatenate(p_parts, axis=0), v_t)
            for pair in range(group // 2):
                o_even = o_all[(2 * pair) * W:(2 * pair + 1) * W] / denoms[2 * pair]
                o_odd = o_all[(2 * pair + 1) * W:(2 * pair + 2) * W] / denoms[2 * pair + 1]
                o_ref[0, rows, (g * 2 + pair) * LANES:(g * 2 + pair + 1) * LANES] = jnp.where(
                    lo_half, o_even, o_odd).astype(o_ref.dtype)


def _swa_bias():
    W = WINDOW
    slopes = 2.0 ** (-8.0 * (np.arange(SWA_HEADS, dtype=np.float32) + 1.0) / SWA_HEADS)
    a = np.arange(W, dtype=np.float32)[:, None]
    c = np.arange(2 * W, dtype=np.float32)[None, :]
    tables = []
    for k_offset in (0.0, float(W)):
        dist = k_offset + a - c
        bias = -slopes.astype(np.float32)[:, None, None] * dist[None]
        tables.append(np.where((dist >= 0) & (dist < W), bias * np.float32(LOG2E), -np.inf).astype(np.float32))
    return jnp.asarray(np.stack(tables))


def _swa(sinks, qs, kvs):
    B, S, C = qs.shape
    bias = _swa_bias()
    return pl.pallas_call(
        _swa_kernel,
        grid=(B, S // SWA_ROWS),
        in_specs=[pl.BlockSpec(memory_space=pltpu.SMEM), _resident(bias.shape),
                  pl.BlockSpec((1, SWA_ROWS, C), lambda b, i: (b, i, 0)),
                  pl.BlockSpec((1, S, kvs.shape[2]), lambda b, i: (b, 0, 0))],
        out_specs=pl.BlockSpec((1, SWA_ROWS, C), lambda b, i: (b, i, 0)),
        out_shape=jax.ShapeDtypeStruct((B, S, C), BF16),
        compiler_params=_params(2),
        name="swa",
    )(sinks, bias, qs, kvs)


def _mid_kernel(x_ref, om_ref, os_ref, gate_ref, w_out_ref, g_in_ref, w_in_ref, w_f_ref,
                x1_ref, qk_ref, v_ref, g1_ref, f_ref, h_scr):
    @pl.when(pl.program_id(0) == 0)
    def _():
        h_scr[...] = jnp.zeros(h_scr.shape, BF16)

    h = h_scr[...]
    w = FOX_WIDTH
    qk_ref[0, :, :w] = (_dot(h, w_in_ref[:, :w]) * LOG2E).astype(BF16)
    qk_ref[0, :, w:] = _dot(h, w_in_ref[:, w:2 * w]).astype(BF16)
    g1_ref[0] = _dot(h, w_in_ref[:, 3 * w:]).astype(BF16)
    f_ref[0] = _dot(h, w_f_ref[...])
    v = _dot(h, w_in_ref[:, 2 * w:3 * w])
    lo_half = lax.broadcasted_iota(jnp.int32, (v.shape[0], LANES), 1) < HALF
    for pair in range(FOX_HEADS // 2):
        v_pair = v[:, pair * LANES:(pair + 1) * LANES]
        v_ref[0, :, (2 * pair) * LANES:(2 * pair + 1) * LANES] = jnp.where(lo_half, v_pair, 1.0).astype(BF16)
        v_ref[0, :, (2 * pair + 1) * LANES:(2 * pair + 2) * LANES] = jnp.where(lo_half, 1.0, v_pair).astype(BF16)

    half = D_MODEL // 2
    gate = _silu(gate_ref[0].astype(F32))
    a_m = (om_ref[0].astype(F32) * gate[:, :half]).astype(BF16)
    a_s = (os_ref[0].astype(F32) * gate[:, half:]).astype(BF16)
    x1 = x_ref[0] + (_dot(a_m, w_out_ref[:half, :]) + _dot(a_s, w_out_ref[half:, :]))
    x1_ref[0] = x1
    h_scr[...] = _rms(x1, g_in_ref[...]).astype(BF16)


def _mid(x, o_mla, o_swa, gate, w_out, g_in, w_in, w_f):
    B, S, D = x.shape
    tm = PROJ_ROWS
    n_steps, this_tile, prev_tile, params = _pipelined_tiles(B, S // tm)
    this_spec = lambda width: pl.BlockSpec((1, tm, width), lambda g: this_tile(g) + (0,))
    prev_spec = lambda width: pl.BlockSpec((1, tm, width), lambda g: prev_tile(g) + (0,))
    outs = [(D, F32), (2 * FOX_WIDTH, BF16), (FOX_HEADS * LANES, BF16), (FOX_WIDTH, BF16), (LANES, F32)]
    return pl.pallas_call(
        _mid_kernel,
        grid=(n_steps,),
        in_specs=[this_spec(width) for width in (D, D // 2, D // 2, D)]
        + [_resident(a.shape) for a in (w_out, g_in, w_in, w_f)],
        out_specs=[this_spec(outs[0][0])] + [prev_spec(width) for width, _ in outs[1:]],
        out_shape=[jax.ShapeDtypeStruct((B, S, width), dtype) for width, dtype in outs],
        scratch_shapes=[pltpu.VMEM((tm, D), BF16)],
        compiler_params=params,
        name="mid",
    )(x, o_mla, o_swa, gate, w_out, g_in, w_in, w_f)


def _split3(x):
    hi = x.astype(BF16)
    r1 = x - hi.astype(F32)
    mid = r1.astype(BF16)
    lo = (r1 - mid.astype(F32)).astype(BF16)
    return hi, mid, lo


def _lcum_kernel(f_ref, b_ref, lc_ref):
    S = f_ref.shape[1]
    log_f = jax.nn.log_sigmoid(f_ref[0] + b_ref[...])
    log_f_t = log_f.T[:FOX_HEADS]
    n = CUM_CHUNK
    tri = (lax.broadcasted_iota(jnp.int32, (n, n), 0) <= lax.broadcasted_iota(jnp.int32, (n, n), 1)).astype(BF16)
    carry = jnp.zeros((FOX_HEADS, 1), F32)
    for c in range(S // n):
        hi, mid, lo = _split3(log_f_t[:, c * n:(c + 1) * n])
        cs = (_dot(hi, tri) + _dot(mid, tri)) + _dot(lo, tri) + carry
        cs2 = cs * LOG2E
        tile, lane0 = divmod(c * n, ATTN_TILE)
        for head in range(FOX_HEADS):
            lc_ref[0, head, tile, :, lane0:lane0 + n] = cs2[head:head + 1]
        carry = cs[:, n - 1:n]


def _lcum(f, b_pad):
    B, S, _ = f.shape
    out_block = (1, FOX_HEADS, S // ATTN_TILE, 1, ATTN_TILE)
    return pl.pallas_call(
        _lcum_kernel,
        grid=(B,),
        in_specs=[pl.BlockSpec((1, S, LANES), lambda b: (b, 0, 0)), _resident(b_pad.shape)],
        out_specs=pl.BlockSpec(out_block, lambda b: (b, 0, 0, 0, 0)),
        out_shape=jax.ShapeDtypeStruct((B,) + out_block[1:], F32),
        compiler_params=_params(1),
        name="lcum",
    )(f, b_pad)


def _final_kernel(x1_hbm, o_hbm, gate_hbm, w_out_ref, g_ref, y_hbm):
    tm = FINAL_ROWS
    d = x1_hbm.shape[1]

    def tile(x1_ref, o_ref, gate_ref, y_ref):
        a = (o_ref[...].astype(F32) * _silu(gate_ref[...].astype(F32))).astype(BF16)
        x2 = x1_ref[...] + _dot(a, w_out_ref[...])
        y_ref[...] = _rms(x2, g_ref[...])

    row_in = pl.BlockSpec((tm, d), lambda i: (i, 0), pipeline_mode=pl.Buffered(3))
    row_out = pl.BlockSpec((tm, d), lambda i: (i, 0))
    pltpu.emit_pipeline(tile, grid=(x1_hbm.shape[0] // tm,), in_specs=[row_in] * 3,
                        out_specs=[row_out])(x1_hbm, o_hbm, gate_hbm, y_hbm)


def _final(x1, o_fox, gate, w_out, g_final):
    B, S, D = x1.shape
    any_space = pl.BlockSpec(memory_space=pl.ANY)
    vmem = pl.BlockSpec(memory_space=pltpu.VMEM)
    y = pl.pallas_call(
        _final_kernel,
        in_specs=[any_space, any_space, any_space, vmem, vmem],
        out_specs=any_space,
        out_shape=jax.ShapeDtypeStruct((B * S, D), F32),
        compiler_params=pltpu.CompilerParams(vmem_limit_bytes=V7X_VMEM_LIMIT_BYTES),
        name="final",
    )(x1.reshape(B * S, D), o_fox.reshape(B * S, D), gate.reshape(B * S, D), w_out, g_final)
    return y.reshape(B, S, D)


def _pad_cols(w, left, total):
    return jnp.pad(w, ((0, 0), (left, total - left - w.shape[1])))


def _pad_rows(w, top, total):
    return jnp.pad(w, ((top, total - top - w.shape[0]), (0, 0)))


def _layer0_weights(w_in, w_q_up, w_kv_up):
    sizes = (MLA_Q_RANK, MLA_KV_RANK, MLA_ROPE, SWA_HEADS * SWA_HEAD_DIM,
             SWA_KV_HEADS * SWA_HEAD_DIM, SWA_KV_HEADS * SWA_HEAD_DIM, D_MODEL)
    cq, ckv, kpe, qs, ks, vs, gate = jnp.split(w_in.astype(BF16).T, np.cumsum(sizes)[:-1].tolist(), axis=0)
    half = MLA_ROPE // 2
    kpe_block = _pad_rows(jnp.concatenate([kpe, kpe[half:], kpe[:half]], axis=0), MLA_NOPE, LANES)
    a_t = jnp.concatenate([cq, ckv, kpe_block], axis=0)
    d = SWA_HEAD_DIM
    kvs_t = jnp.concatenate([ks[:d], ks[:d], ks[d:], ks[d:], vs[:d], vs[:d], vs[d:], vs[d:]], axis=0)
    swa_scale = SWA_HEAD_DIM ** -0.5
    w_a, w_qs, w_kvs, w_gate = [w_t.astype(BF16).T for w_t in (a_t, qs * swa_scale, kvs_t, gate)]
    qk = MLA_NOPE + MLA_ROPE
    wq3 = w_q_up.reshape(MLA_Q_RANK, MLA_HEADS, qk)
    w_q = jnp.pad(wq3, ((0, 0), (0, 0), (0, LANES - qk))).reshape(MLA_Q_RANK, MLA_HEADS * LANES)
    wkv3 = w_kv_up.reshape(MLA_KV_RANK, MLA_HEADS, MLA_NOPE + MLA_V)
    w_k = jnp.pad(wkv3[:, :, :MLA_NOPE], ((0, 0), (0, 0), (0, LANES - MLA_NOPE))).reshape(MLA_KV_RANK, MLA_HEADS * LANES)
    wv4 = wkv3[:, :, MLA_NOPE:].reshape(MLA_KV_RANK, MLA_HEADS // 2, 2, MLA_V)
    zeros_v = jnp.zeros_like(wv4[:, :, 0])
    w_v = jnp.stack([wv4[:, :, 0], zeros_v, zeros_v, wv4[:, :, 1]], axis=2).reshape(MLA_KV_RANK, MLA_HEADS * LANES)
    return [w_a, w_qs, w_kvs, w_gate] + [w.astype(BF16) for w in (w_q, w_k, w_v)]


def _rope_table():
    half = MLA_ROPE // 2
    inv_freq = 1.0 / (ROPE_THETA ** (jnp.arange(0, MLA_ROPE, 2, dtype=F32) / MLA_ROPE))
    t = jnp.zeros((8, LANES), F32)
    t = t.at[0, MLA_NOPE:MLA_NOPE + half].set(inv_freq).at[0, MLA_NOPE + half:MLA_NOPE + MLA_ROPE].set(inv_freq)
    t = t.at[1, MLA_NOPE:MLA_NOPE + MLA_ROPE].set(1.0)
    t = t.at[2, MLA_NOPE:MLA_NOPE + half].set(-1.0).at[2, MLA_NOPE + half:MLA_NOPE + MLA_ROPE].set(1.0)
    t = t.at[3, :MLA_NOPE].set(1.0)
    t = t.at[4, MLA_NOPE:MLA_NOPE + half].set(-1.0).at[5, MLA_NOPE + half:MLA_NOPE + MLA_ROPE].set(1.0)
    return t


def kernel(x, positions, e_g_in, e_w_in, e_g_q_a, e_w_q_up, e_g_kv_a, e_w_kv_up, e_sinks,
           e_w_out, o_g_in, o_w_in, o_b_f, o_w_out, g_final):
    B, S, D = x.shape
    assert D == D_MODEL and all(S % rows == 0 for rows in (PROJ_ROWS, FINAL_ROWS, ATTN_TILE, SWA_ROWS))
    row = lambda g: g.reshape(1, -1).astype(F32)

    w0 = _layer0_weights(e_w_in[0], e_w_q_up[0], e_w_kv_up[0])
    w_a, w_qs, w_kvs, w_gate, w_q, w_k, w_v = w0
    qm, km, vm, qs, kvs, gate0 = _proj0(
        x, positions.reshape(B, S // PROJ_ROWS, 1, PROJ_ROWS), row(e_g_in[0]), w_a, w_qs, w_kvs, w_gate,
        row(e_g_q_a[0]), w_q, row(e_g_kv_a[0]), w_k, w_v, _rope_table())
    o_mla = _flash(qm, km, vm, None, q_col0=0, k_col0=0, v_col0=0,
                   n_pairs=MLA_HEADS // 2, head_lanes=LANES)
    o_swa = _swa(e_sinks[0].astype(F32), qs, kvs)

    sizes = (FOX_WIDTH, FOX_WIDTH, FOX_WIDTH, FOX_HEADS, FOX_WIDTH)
    w1_q, w1_k, w1_v, w1_f, w1_gate = jnp.split(o_w_in[0].astype(BF16).T, np.cumsum(sizes)[:-1].tolist(), axis=0)
    fox_scale = FOX_HEAD_DIM ** -0.5
    w1 = jnp.concatenate([w1_q * fox_scale, w1_k, w1_v, w1_gate], axis=0).astype(BF16).T
    w1_f_pad = _pad_rows(w1_f, 0, LANES).astype(BF16).T
    x1, qk1, v1, gate1, f_logit = _mid(x, o_mla, o_swa, gate0, e_w_out[0].astype(BF16), row(o_g_in[0]), w1, w1_f_pad)

    lc = _lcum(f_logit, _pad_cols(row(o_b_f[0]), 0, LANES))
    o_fox = _flash(qk1, qk1, v1, lc, q_col0=0, k_col0=FOX_WIDTH // LANES, v_col0=0,
                   n_pairs=FOX_HEADS // 2, head_lanes=HALF)
    return _final(x1, o_fox, gate1, o_w_out[0].astype(BF16), row(g_final))
```
